```python
import jax, jax.numpy as jnp
from jax import lax
import numpy as np

D_MODEL = 4096
BATCH = 4
SEQ = 4096
DEPTH = 1
DEC_BATCH = 4
DEC_SEQ = 2048
PAST_LEN = 128

MIX_A = D_MODEL // 2
MIX_B = D_MODEL - MIX_A
A_HEADS = 4
A_HEAD_DIM = MIX_A // A_HEADS
CHUNK = 128
B_HEAD_DIM = 64
B_HEADS = MIX_B // B_HEAD_DIM
LORA_W = 128
LORA_A = 128
LORA_G = 512
B_COLS = 3 * MIX_B + LORA_W + LORA_A + LORA_G
IN_COLS = 2 * MIX_A + B_COLS
N_MEM = 256
X_HEADS = 4
X_HEAD_DIM = D_MODEL // X_HEADS
D_FF = 11008
RMS_EPS = 1e-6
LN_EPS = 1e-5
GN_EPS = 64e-5
L2_EPS = 1e-12

kernel_name = "hybrid_gmlp_rwkv7_bidir_encoder"


def rms_norm(x, g):
    xf = x.astype(jnp.float32)
    y = xf * lax.rsqrt(jnp.mean(xf * xf, axis=-1, keepdims=True) + RMS_EPS)
    return (y * g.astype(jnp.float32)).astype(x.dtype)


def layer_norm(x, g, b, eps):
    xf = x.astype(jnp.float32)
    mu = jnp.mean(xf, axis=-1, keepdims=True)
    var = jnp.mean(jnp.square(xf - mu), axis=-1, keepdims=True)
    return ((xf - mu) * lax.rsqrt(var + eps) * g.astype(jnp.float32) + b.astype(jnp.float32)).astype(x.dtype)


def shift_prev(x):
    return jnp.pad(x, ((0, 0), (1, 0), (0, 0)))[:, :-1]


def shift_next(x):
    return jnp.pad(x, ((0, 0), (0, 1), (0, 0)))[:, 1:]


def chunk_spatial_gating(za, ln_g, ln_b, w_s, b_s):
    u, v = jnp.split(za, 2, axis=-1)
    v = layer_norm(v, ln_g, ln_b, LN_EPS)
    bsz, t, _ = v.shape
    v = v.reshape(bsz, t // CHUNK, CHUNK, A_HEADS, A_HEAD_DIM)
    mixed = jnp.einsum('hpq,bcqhd->bcphd', w_s, v) + b_s.T[None, None, :, :, None]
    return u * mixed.reshape(bsz, t, MIX_A)


def wkv_scan(r, w, k, v, a_neg, b, reverse):
    def step(S, inp):
        r_t, w_t, k_t, v_t, an_t, b_t = inp
        sa = jnp.einsum('bhij,bhj->bhi', S, an_t)
        S = S * w_t[:, :, None, :] + sa[..., None] * b_t[:, :, None, :] + v_t[..., None] * k_t[:, :, None, :]
        return S, jnp.einsum('bhij,bhj->bhi', S, r_t)
    bsz, _, h, n = r.shape
    s0 = jnp.zeros((bsz, h, n, n), jnp.float32)
    xs = tuple(jnp.moveaxis(z, 1, 0) for z in (r, w, k, v, a_neg, b))
    _, ys = lax.scan(step, s0, xs, reverse=reverse)
    return jnp.moveaxis(ys, 0, 1)


def rwkv7_bidir(zb, mu, w0, w_up_decay, a0, w_up_iclr, w_up_gate, k_k, k_a, r_k, gn_g, gn_b):
    zb = zb + mu * (0.5 * (shift_prev(zb) + shift_next(zb)) - zb)
    r, k, v, xw, xa, xg = jnp.split(
        zb, [MIX_B, 2 * MIX_B, 3 * MIX_B, 3 * MIX_B + LORA_W, 3 * MIX_B + LORA_W + LORA_A], axis=-1)
    bsz, t, _ = r.shape
    f32 = jnp.float32

    def heads(z):
        return z.reshape(bsz, t, B_HEADS, B_HEAD_DIM).astype(f32)

    g = jax.nn.sigmoid(xg) @ w_up_gate
    kk = heads(k * k_k)
    kk = kk * lax.rsqrt(jnp.maximum(jnp.sum(kk * kk, axis=-1, keepdims=True), L2_EPS))
    rh, vh = heads(r), heads(v)
    tw = jnp.tanh(xw)
    y = jnp.zeros_like(rh)
    bonus = jnp.zeros_like(rh)
    for d, rev in ((0, False), (1, True)):
        logw = -jax.nn.softplus(-(w0[d] + tw @ w_up_decay[d])) - 0.5
        decay = jnp.exp(-jnp.exp(logw.astype(f32)))
        a = jax.nn.sigmoid(a0[d] + xa @ w_up_iclr[d])
        k_d = k * (1.0 + (a - 1.0) * k_a)
        a_h, k_h = heads(a), heads(k_d)
        y = y + wkv_scan(rh, heads(decay), k_h, vh, -kk, kk * a_h, rev)
        bonus = bonus + jnp.sum(rh * k_h * r_k.astype(f32), axis=-1, keepdims=True) * vh
    mean = jnp.mean(y, axis=-1, keepdims=True)
    var = jnp.mean(jnp.square(y - mean), axis=-1, keepdims=True)
    yn = (y - mean) * lax.rsqrt(var + GN_EPS)
    yn = yn * gn_g.reshape(B_HEADS, B_HEAD_DIM).astype(f32) + gn_b.reshape(B_HEADS, B_HEAD_DIM).astype(f32)
    out = (yn + bonus).reshape(bsz, t, MIX_B).astype(zb.dtype)
    return out * g


def cross_attend(h, mem, w_q, w_kv, w_o):
    bsz, t, _ = h.shape
    q = (h @ w_q).reshape(bsz, t, X_HEADS, X_HEAD_DIM)
    k, v = jnp.split(mem @ w_kv, 2, axis=-1)
    k = k.reshape(bsz, N_MEM, X_HEADS, X_HEAD_DIM)
    v = v.reshape(bsz, N_MEM, X_HEADS, X_HEAD_DIM)
    s = jnp.einsum('bqhd,bkhd->bhqk', q, k).astype(jnp.float32) * (X_HEAD_DIM ** -0.5)
    p = jax.nn.softmax(s, axis=-1).astype(h.dtype)
    o = jnp.einsum('bhqk,bkhd->bqhd', p, v).reshape(bsz, t, D_MODEL)
    return o @ w_o


def conv_ffn(h, w_up, conv_w, conv_b, w_down):
    z = h @ w_up
    zp = jnp.pad(z, ((0, 0), (1, 1), (0, 0)))
    z = conv_w[0] * zp[:, :-2] + conv_w[1] * zp[:, 1:-1] + conv_w[2] * zp[:, 2:] + conv_b
    gate, val = jnp.split(z, 2, axis=-1)
    return (jax.nn.silu(gate) * val) @ w_down


def encoder_layer(x, mem, norm_mix, w_in, mu_shift, ln_v_g, ln_v_b, w_s, b_s, w0, w_up_decay, a0,
                  w_up_iclr, w_up_gate, k_k, k_a, r_k, gn_g, gn_b, w_out, norm_x, norm_mem, w_q, w_kv,
                  w_o, norm_ffn, w_ffn_up, conv_w, conv_b, w_ffn_down):
    z = rms_norm(x, norm_mix) @ w_in
    ya = chunk_spatial_gating(jax.nn.gelu(z[..., :2 * MIX_A]), ln_v_g, ln_v_b, w_s, b_s)
    yb = rwkv7_bidir(z[..., 2 * MIX_A:], mu_shift, w0, w_up_decay, a0, w_up_iclr, w_up_gate,
                     k_k, k_a, r_k, gn_g, gn_b)
    x = x + jnp.concatenate([ya, yb], axis=-1) @ w_out
    x = x + cross_attend(rms_norm(x, norm_x), rms_norm(mem, norm_mem), w_q, w_kv, w_o)
    x = x + conv_ffn(rms_norm(x, norm_ffn), w_ffn_up, conv_w, conv_b, w_ffn_down)
    return x


def trunk(x, mem, layer_params, norm_out):
    for l in range(DEPTH):
        x = encoder_layer(x, mem, *(p[l] for p in layer_params))
    return rms_norm(x, norm_out)


def setup_inputs(seed: int = 0) -> dict:
    key = jax.random.key(seed)
    ks = jax.random.split(key, 40)
    L, D = DEPTH, D_MODEL
    n = lambda i, shape: jax.random.normal(ks[i], shape, jnp.float32)
    u = lambda i, shape, lo, hi: jax.random.uniform(ks[i], shape, jnp.float32, lo, hi)
    return {
        "x_prompt": n(0, (BATCH, SEQ, D)),
        "x_sample": n(1, (DEC_BATCH, DEC_SEQ, D)),
        "mem_prompt": n(2, (BATCH, N_MEM, D)),
        "mem_sample": n(3, (DEC_BATCH, N_MEM, D)),
        "norm_mix": 1.0 + 0.02 * n(4, (L, D)),
        "w_in": n(5, (L, D, IN_COLS)) * D ** -0.5,
        "mu_shift": u(6, (L, B_COLS), 0.0, 1.0),
        "ln_v_g": 1.0 + 0.02 * n(7, (L, MIX_A)),
        "ln_v_b": 0.02 * n(8, (L, MIX_A)),
        "w_s": n(9, (L, A_HEADS, CHUNK, CHUNK)) * CHUNK ** -0.5,
        "b_s": 1.0 + 0.02 * n(10, (L, A_HEADS, CHUNK)),
        "w0": u(11, (L, 2, MIX_B), -6.0, 1.0),
        "w_up_decay": 0.1 * n(12, (L, 2, LORA_W, MIX_B)) * LORA_W ** -0.5,
        "a0": 0.1 * n(13, (L, 2, MIX_B)),
        "w_up_iclr": 0.1 * n(14, (L, 2, LORA_A, MIX_B)) * LORA_A ** -0.5,
        "w_up_gate": n(15, (L, LORA_G, MIX_B)) * LORA_G ** -0.5,
        "k_k": 0.85 + 0.02 * n(16, (L, MIX_B)),
        "k_a": 1.0 + 0.02 * n(17, (L, MIX_B)),
        "r_k": 0.1 * n(18, (L, B_HEADS, B_HEAD_DIM)),
        "gn_g": 1.0 + 0.02 * n(19, (L, MIX_B)),
        "gn_b": 0.02 * n(20, (L, MIX_B)),
        "w_out": n(21, (L, D, D)) * D ** -0.5,
        "norm_x": 1.0 + 0.02 * n(22, (L, D)),
        "norm_mem": 1.0 + 0.02 * n(23, (L, D)),
        "w_q": n(24, (L, D, D)) * D ** -0.5,
        "w_kv": n(25, (L, D, 2 * D)) * D ** -0.5,
        "w_o": n(26, (L, D, D)) * D ** -0.5,
        "norm_ffn": 1.0 + 0.02 * n(27, (L, D)),
        "w_ffn_up": n(28, (L, D, 2 * D_FF)) * D ** -0.5,
        "conv_w": n(29, (L, 3, 2 * D_FF)) * 3 ** -0.5,
        "conv_b": 0.02 * n(30, (L, 2 * D_FF)),
        "w_ffn_down": n(31, (L, D_FF, D)) * D_FF ** -0.5,
        "norm_out": 1.0 + 0.02 * n(32, (D,)),
    }


def reference(x_prompt, x_sample, mem_prompt, mem_sample, norm_mix, w_in, mu_shift, ln_v_g, ln_v_b,
              w_s, b_s, w0, w_up_decay, a0, w_up_iclr, w_up_gate, k_k, k_a, r_k, gn_g, gn_b, w_out,
              norm_x, norm_mem, w_q, w_kv, w_o, norm_ffn, w_ffn_up, conv_w, conv_b, w_ffn_down, norm_out):
    layer_params = (norm_mix, w_in, mu_shift, ln_v_g, ln_v_b, w_s, b_s, w0, w_up_decay, a0, w_up_iclr,
                    w_up_gate, k_k, k_a, r_k, gn_g, gn_b, w_out, norm_x, norm_mem, w_q, w_kv, w_o,
                    norm_ffn, w_ffn_up, conv_w, conv_b, w_ffn_down)
    y_prompt = trunk(x_prompt, mem_prompt, layer_params, norm_out)
    y_sample = trunk(x_sample, mem_sample, layer_params, norm_out)
    return (y_prompt, y_sample)
```

```python
import functools

import jax
import jax.numpy as jnp
from jax import lax
from jax.experimental import pallas as pl
from jax.experimental.pallas import tpu as pltpu

RMS_EPS = 1e-6
LN_EPS = 1e-5
GN_EPS = 64e-5
L2_EPS = 1e-12
X_HEADS = 4

LANES = 128
BF16_ROWS = 16
SCAN_CHUNK = 64
SCAN_HEADS = 8
INV_BASE = 8
VMEM_LIMIT = 56 * 1024 * 1024

F32 = jnp.float32
BF16 = jnp.bfloat16


def _params(*sem):
    return pltpu.CompilerParams(dimension_semantics=sem, vmem_limit_bytes=VMEM_LIMIT)


def _pick(n, candidates):
    for c in candidates:
        if n % c == 0:
            return c
    raise ValueError(f"no tile for {n} in {candidates}")


def _dot(a, b):
    return jnp.dot(a, b, preferred_element_type=F32)


def _split(x):
    hi = x.astype(BF16)
    lo = (x - hi.astype(F32)).astype(BF16)
    return hi, lo


NN = (((1,), (0,)), ((), ()))
NT = (((1,), (1,)), ((), ()))
TN = (((0,), (0,)), ((), ()))


def _mm3(a, b, dims=NN):
    ah, al = _split(a)
    bh, bl = _split(b)
    dg = lambda x, y: lax.dot_general(x, y, dims, preferred_element_type=F32)
    return dg(ah, bh) + (dg(ah, bl) + dg(al, bh))


def _rmsnorm_body(*refs, n_in):
    g_ref, o_ref = refs[n_in], refs[n_in + 1]
    x = refs[0][...]
    for r in refs[1:n_in]:
        x = x + r[...]
    ms = jnp.mean(x * x, axis=-1, keepdims=True)
    o_ref[...] = (x * lax.rsqrt(ms + RMS_EPS) * g_ref[...]).astype(o_ref.dtype)


def _rmsnorm(xs, g, out_dtype):
    m, d = xs[0].shape
    tm = _pick(m, (256, 128, 64, 32, 16, 8))
    row = pl.BlockSpec((tm, d), lambda i: (i, 0))
    return pl.pallas_call(
        functools.partial(_rmsnorm_body, n_in=len(xs)),
        out_shape=jax.ShapeDtypeStruct((m, d), out_dtype),
        grid=(m // tm,),
        in_specs=[row] * len(xs) + [pl.BlockSpec((1, d), lambda i: (0, 0))],
        out_specs=row,
        compiler_params=_params("parallel"),
        name="rmsnorm",
    )(*xs, g.reshape(1, d))


def _mm_body(*refs, n_lhs, has_res):
    a_refs, w_refs = refs[:n_lhs], refs[n_lhs:2 * n_lhs]
    o_ref = refs[-1]
    acc = _dot(a_refs[0][...], w_refs[0][...])
    for a, w in zip(a_refs[1:], w_refs[1:]):
        acc = acc + _dot(a[...], w[...])
    if has_res:
        acc = acc + refs[2 * n_lhs][...]
    o_ref[...] = acc.astype(o_ref.dtype)


def _matmul(lhs, ws, out_dtype, residual=None):
    m = lhs[0].shape[0]
    n = ws[0].shape[1]
    tm = _pick(m, (1024, 512, 256, 128, 64, 32, 16, 8))
    tn = _pick(n, (512, 768, 256, 128))
    in_specs = [pl.BlockSpec((tm, a.shape[1]), lambda i, j: (i, 0)) for a in lhs]
    in_specs += [pl.BlockSpec((w.shape[0], tn), lambda i, j: (0, j)) for w in ws]
    args = list(lhs) + list(ws)
    if residual is not None:
        in_specs.append(pl.BlockSpec((tm, tn), lambda i, j: (i, j)))
        args.append(residual)
    return pl.pallas_call(
        functools.partial(_mm_body, n_lhs=len(lhs), has_res=residual is not None),
        out_shape=jax.ShapeDtypeStruct((m, n), out_dtype),
        grid=(m // tm, n // tn),
        in_specs=in_specs,
        out_specs=pl.BlockSpec((tm, tn), lambda i, j: (i, j)),
        compiler_params=_params("parallel", "parallel"),
        name="matmul",
    )(*args)


def _gmlp_body(zu_ref, zv_ref, g_ref, b_ref, ws_ref, bs_ref, o_ref, *, chunk):
    u = jax.nn.gelu(zu_ref[...])
    v = jax.nn.gelu(zv_ref[...])
    mu = jnp.mean(v, axis=-1, keepdims=True)
    var = jnp.mean(jnp.square(v - mu), axis=-1, keepdims=True)
    vn = ((v - mu) * lax.rsqrt(var + LN_EPS) * g_ref[...] + b_ref[...]).astype(BF16)
    rows, width = u.shape
    heads = ws_ref.shape[0]
    hd = width // heads
    bs = bs_ref[...]
    for c in range(rows // chunk):
        rs = slice(c * chunk, (c + 1) * chunk)
        mixed = jnp.concatenate(
            [_dot(ws_ref[h], vn[rs, h * hd:(h + 1) * hd]) for h in range(heads)], axis=1)
        o_ref[rs, :] = (u[rs, :] * (mixed + bs)).astype(o_ref.dtype)


def _gmlp(za, ln_g, ln_b, w_s, b_s):
    m, two_a = za.shape
    mix_a = two_a // 2
    heads, chunk, _ = w_s.shape
    rows = _pick(m, (2 * chunk, chunk))
    bs_full = jnp.repeat(b_s.T, mix_a // heads, axis=1)
    vec = pl.BlockSpec((1, mix_a), lambda i: (0, 0))
    return pl.pallas_call(
        functools.partial(_gmlp_body, chunk=chunk),
        out_shape=jax.ShapeDtypeStruct((m, mix_a), BF16),
        grid=(m // rows,),
        in_specs=[pl.BlockSpec((rows, mix_a), lambda i: (i, 0)),
                  pl.BlockSpec((rows, mix_a), lambda i: (i, 1)),
                  vec, vec,
                  pl.BlockSpec((heads, chunk, chunk), lambda i: (0, 0, 0)),
                  pl.BlockSpec((chunk, mix_a), lambda i: (0, 0))],
        out_specs=pl.BlockSpec((rows, mix_a), lambda i: (i, 0)),
        compiler_params=_params("parallel"),
        name="gmlp",
    )(za, za, ln_g.reshape(1, mix_a), ln_b.reshape(1, mix_a), w_s.astype(BF16), bs_full)


def _headsum(x, j_ref):
    tm, w = x.shape
    nb = w // LANES
    xs = jnp.concatenate([x[:, i * LANES:(i + 1) * LANES] for i in range(nb)], axis=0)
    hi, lo = _split(xs)
    j = j_ref[...]
    s = _dot(hi, j) + _dot(lo, j)
    return jnp.concatenate([s[i * tm:(i + 1) * tm] for i in range(nb)], axis=1)


def _tshift(z, zp, zn, mu, first, last):
    tm = z.shape[0]
    rows = lax.broadcasted_iota(jnp.int32, (tm, 1), 0)
    prev_row = jnp.where(first, 0.0, zp[7:8, :])
    next_row = jnp.where(last, 0.0, zn[0:1, :])
    prev = jnp.where(rows == 0, prev_row, pltpu.roll(z, 1, 0))
    nxt = jnp.where(rows == tm - 1, next_row, pltpu.roll(z, tm - 1, 0))
    return z + mu * (0.5 * (prev + nxt) - z)


def _prep_body(zr, zrp, zrn, zk, zkp, zkn, zv, zvp, zvn, zl, zlp, zln,
               mur, muk, muv, mul, w0_ref, wd_ref, a0_ref, wi_ref, wg_ref,
               kk_ref, ka_ref, rk_ref, j_ref,
               r_out, v_out, kk_out, g_out, bonus_out, lw_out, kd_out, a_out,
               *, lora_w, lora_a):
    i = pl.program_id(1)
    first = i == 0
    last = i == pl.num_programs(1) - 1
    r = _tshift(zr[0], zrp[0], zrn[0], mur[...], first, last)
    k = _tshift(zk[0], zkp[0], zkn[0], muk[...], first, last)
    v = _tshift(zv[0], zvp[0], zvn[0], muv[...], first, last)
    lo = _tshift(zl[0], zlp[0], zln[0], mul[...], first, last)
    xw = lo[:, :lora_w]
    xa = lo[:, lora_w:lora_w + lora_a]
    xg = lo[:, lora_w + lora_a:]

    g_out[0] = _dot(jax.nn.sigmoid(xg).astype(BF16), wg_ref[...])
    kkr = k * kk_ref[...]
    ss = _headsum(kkr * kkr, j_ref)
    kk = kkr * lax.rsqrt(jnp.maximum(ss, L2_EPS))
    tw = jnp.tanh(xw).astype(BF16)
    xab = xa.astype(BF16)
    rk = None
    for d in range(2):
        logw = -jax.nn.softplus(-(w0_ref[d:d + 1, :] + _dot(tw, wd_ref[d]))) - 0.5
        lw_out[d, 0] = -jnp.exp(logw)
        a = jax.nn.sigmoid(a0_ref[d:d + 1, :] + _dot(xab, wi_ref[d]))
        kd = k * (1.0 + (a - 1.0) * ka_ref[...])
        a_out[d, 0] = a
        kd_out[d, 0] = kd
        s = _headsum(r * kd * rk_ref[...], j_ref)
        rk = s if rk is None else rk + s
    r_out[0] = r
    v_out[0] = v
    kk_out[0] = kk
    bonus_out[0] = rk * v


def _rwkv_prep(zrkv, zl, mu, w0, w_up_decay, a0, w_up_iclr, w_up_gate, k_k, k_a, r_k, jmat):
    b, t, three_b = zrkv.shape
    mix_b = three_b // 3
    nl = zl.shape[-1]
    lora_w, lora_a = w_up_decay.shape[1], w_up_iclr.shape[1]
    tm = _pick(t, (128, 64, 32, 16, 8))
    hb = tm // 8
    nblk8 = t // 8

    def main(w, col):
        return pl.BlockSpec((1, tm, w), lambda bi, i: (bi, i, col))

    def prev(w, col):
        return pl.BlockSpec((1, 8, w), lambda bi, i: (bi, jnp.maximum(i * hb - 1, 0), col))

    def nxt(w, col):
        return pl.BlockSpec((1, 8, w), lambda bi, i: (bi, jnp.minimum((i + 1) * hb, nblk8 - 1), col))

    def const(shape):
        return pl.BlockSpec(shape, lambda bi, i: (0,) * len(shape))

    in_specs, args = [], []
    for col in range(3):
        in_specs += [main(mix_b, col), prev(mix_b, col), nxt(mix_b, col)]
        args += [zrkv] * 3
    in_specs += [main(nl, 0), prev(nl, 0), nxt(nl, 0)]
    args += [zl] * 3
    mu2 = mu.reshape(1, -1)
    in_specs += [pl.BlockSpec((1, mix_b), lambda bi, i, c=c: (0, c)) for c in range(3)]
    args += [mu2[:, :three_b]] * 3
    in_specs.append(const((1, nl)))
    args.append(mu2[:, three_b:])
    consts = [w0, w_up_decay.astype(BF16), a0, w_up_iclr.astype(BF16), w_up_gate.astype(BF16),
              k_k.reshape(1, mix_b), k_a.reshape(1, mix_b), r_k.reshape(1, mix_b), jmat]
    in_specs += [const(c.shape) for c in consts]
    args += consts

    shared = jax.ShapeDtypeStruct((b, t, mix_b), F32)
    perdir = jax.ShapeDtypeStruct((2, b, t, mix_b), F32)
    o_shared = pl.BlockSpec((1, tm, mix_b), lambda bi, i: (bi, i, 0))
    o_perdir = pl.BlockSpec((2, 1, tm, mix_b), lambda bi, i: (0, bi, i, 0))
    return pl.pallas_call(
        functools.partial(_prep_body, lora_w=lora_w, lora_a=lora_a),
        out_shape=[shared] * 5 + [perdir] * 3,
        grid=(b, t // tm),
        in_specs=in_specs,
        out_specs=[o_shared] * 5 + [o_perdir] * 3,
        compiler_params=_params("parallel", "parallel"),
        name="rwkv_prep",
    )(*args)


def _tri_inverse(l, row, col):
    c = l.shape[0]

    def same_block(size):
        sh = size.bit_length() - 1
        return (row >> sh) == (col >> sh)

    eye = (row == col).astype(F32)
    ld = jnp.where(same_block(INV_BASE), l, 0.0)
    t = eye + ld
    p = ld
    size = 2
    while size < INV_BASE:
        p = _mm3(p, p)
        t = _mm3(t, eye + p)
        size *= 2
    size = INV_BASE
    while size < c:
        off = jnp.where(same_block(2 * size) & jnp.logical_not(same_block(size)), l, 0.0)
        t = t + _mm3(t, _mm3(off, t))
        size *= 2
    return t


def _scan_body(r_ref, v_ref, kk_ref, lw_ref, kd_ref, a_ref, y_ref, s_ref, *, heads, n):
    d = pl.program_id(2)
    c = pl.program_id(3)
    cs = r_ref.shape[1]

    @pl.when(c == 0)
    def _():
        s_ref[...] = jnp.zeros_like(s_ref)

    fwd = d == 0
    row = lax.broadcasted_iota(jnp.int32, (cs, cs), 0)
    col = lax.broadcasted_iota(jnp.int32, (cs, cs), 1)
    incl = (row - col) * (1 - 2 * d) >= 0
    eye_n = lax.broadcasted_iota(jnp.int32, (n, n), 0) == lax.broadcasted_iota(jnp.int32, (n, n), 1)

    lw = lw_ref[0, 0]
    tri = incl.astype(BF16)
    l_hi = lw.astype(BF16)
    l_mid = (lw - l_hi.astype(F32)).astype(BF16)
    l_lo = (lw - l_hi.astype(F32) - l_mid.astype(F32)).astype(BF16)
    cum = _dot(tri, l_hi) + (_dot(tri, l_mid) + _dot(tri, l_lo))
    tot = jnp.where(fwd, cum[cs - 1:cs, :], cum[0:1, :])
    e_tot = jnp.exp(tot)

    r, v, kk = r_ref[0], v_ref[0], kk_ref[0]
    kd, a = kd_ref[0, 0], a_ref[0, 0]
    bvec = kk * a
    e_neg = jnp.exp(-cum)
    e_hat = jnp.exp(tot - cum)
    a_t = -kk * jnp.exp(cum - lw)
    r_t = r * jnp.exp(cum)
    b_t = bvec * e_neg
    k_t = kd * e_neg
    b_h = bvec * e_hat
    k_h = kd * e_hat

    row2 = lax.broadcasted_iota(jnp.int32, (2 * cs, 2 * cs), 0)
    col2 = lax.broadcasted_iota(jnp.int32, (2 * cs, 2 * cs), 1)
    order2 = ((row2 & (cs - 1)) - (col2 & (cs - 1))) * (1 - 2 * d)
    mask2 = order2 >= (row2 < cs).astype(jnp.int32)

    outs = []
    for g in range(heads):
        sl = slice(g * n, (g + 1) * n)
        at, rt, bt, kt, bh, kh, vh = a_t[:, sl], r_t[:, sl], b_t[:, sl], k_t[:, sl], b_h[:, sl], k_h[:, sl], v[:, sl]
        aa = _mm3(jnp.concatenate([at, rt], axis=0), jnp.concatenate([bt, kt], axis=0), NT)
        aa = jnp.where(mask2, aa, 0.0)
        aab, aqb = aa[:cs, :cs], aa[cs:, :cs]
        tinv = _tri_inverse(aab, row, col)
        av = _mm3(aa[:, cs:], vh)
        wu = _mm3(tinv, jnp.concatenate([at, av[:cs]], axis=1))
        qy = _mm3(aqb, wu)
        qp = rt + qy[:, :n]
        y0 = av[cs:] + qy[:, n:]
        bw = _mm3(bh, wu, TN)
        m_mat = jnp.where(eye_n, e_tot[:, sl], 0.0) + bw[:, :n]
        n_mat = _mm3(kh, vh, TN) + bw[:, n:]
        s_in = s_ref[g]
        ms = _mm3(jnp.concatenate([m_mat, qp], axis=0), s_in)
        s_ref[g] = ms[:n] + n_mat
        outs.append(ms[n:] + y0)
    y_ref[0, 0] = jnp.concatenate(outs, axis=1)


def _wkv_scan(r, v, kk, lw, kd, a, n):
    b, t, mix_b = r.shape
    heads = mix_b // n
    hg = SCAN_HEADS if heads % SCAN_HEADS == 0 else heads
    cs = SCAN_CHUNK
    nc = t // cs
    w = hg * n

    def tchunk(di, ci):
        return ci + di * (nc - 1 - 2 * ci)

    shared = pl.BlockSpec((1, cs, w), lambda bi, hi, di, ci: (bi, tchunk(di, ci), hi))
    perdir = pl.BlockSpec((1, 1, cs, w), lambda bi, hi, di, ci: (di, bi, tchunk(di, ci), hi))
    return pl.pallas_call(
        functools.partial(_scan_body, heads=hg, n=n),
        out_shape=jax.ShapeDtypeStruct((2, b, t, mix_b), F32),
        grid=(b, heads // hg, 2, nc),
        in_specs=[shared] * 3 + [perdir] * 3,
        out_specs=perdir,
        scratch_shapes=[pltpu.VMEM((hg, n, n), F32)],
        compiler_params=_params("parallel", "parallel", "arbitrary", "arbitrary"),
        name="wkv_scan",
    )(r, v, kk, lw, kd, a)


def _post_body(y_ref, bonus_ref, g_ref, gg_ref, gb_ref, j_ref, o_ref, *, n):
    y = y_ref[0] + y_ref[1]
    mean = _headsum(y, j_ref) * (1.0 / n)
    yc = y - mean
    var = _headsum(yc * yc, j_ref) * (1.0 / n)
    yn = yc * lax.rsqrt(var + GN_EPS) * gg_ref[...] + gb_ref[...]
    o_ref[...] = ((yn + bonus_ref[...]) * g_ref[...]).astype(o_ref.dtype)


def _rwkv_post(y, bonus, g, gn_g, gn_b, jmat, n):
    _, m, mix_b = y.shape
    tm = _pick(m, (256, 128, 64, 32, 16, 8))
    row = pl.BlockSpec((tm, mix_b), lambda i: (i, 0))
    vec = pl.BlockSpec((1, mix_b), lambda i: (0, 0))
    return pl.pallas_call(
        functools.partial(_post_body, n=n),
        out_shape=jax.ShapeDtypeStruct((m, mix_b), BF16),
        grid=(m // tm,),
        in_specs=[pl.BlockSpec((2, tm, mix_b), lambda i: (0, i, 0)), row, row, vec, vec,
                  pl.BlockSpec(jmat.shape, lambda i: (0, 0))],
        out_specs=row,
        compiler_params=_params("parallel"),
        name="rwkv_post",
    )(y, bonus, g, gn_g.reshape(1, mix_b), gn_b.reshape(1, mix_b), jmat)


def _attn_body(q_ref, kv_ref, o_ref, *, heads):
    d = q_ref.shape[-1]
    hd = d // heads
    scale = hd ** -0.5
    for h in range(heads):
        q = q_ref[0, :, h * hd:(h + 1) * hd]
        k = kv_ref[0, :, h * hd:(h + 1) * hd]
        v = kv_ref[0, :, d + h * hd:d + (h + 1) * hd]
        s = lax.dot_general(q, k, NT, preferred_element_type=F32) * scale
        e = jnp.exp(s - jnp.max(s, axis=-1, keepdims=True))
        p = e / jnp.sum(e, axis=-1, keepdims=True)
        o_ref[0, :, h * hd:(h + 1) * hd] = _dot(p.astype(BF16), v).astype(o_ref.dtype)


def _attention(q, kv):
    b, t, d = q.shape
    n_mem = kv.shape[1]
    tq = _pick(t, (512, 256, 128, 64, 32, 16))
    return pl.pallas_call(
        functools.partial(_attn_body, heads=X_HEADS),
        out_shape=jax.ShapeDtypeStruct((b, t, d), BF16),
        grid=(b, t // tq),
        in_specs=[pl.BlockSpec((1, tq, d), lambda bi, i: (bi, i, 0)),
                  pl.BlockSpec((1, n_mem, 2 * d), lambda bi, i: (bi, 0, 0))],
        out_specs=pl.BlockSpec((1, tq, d), lambda bi, i: (bi, i, 0)),
        compiler_params=_params("parallel", "parallel"),
        name="cross_attention",
    )(q, kv)


def _ffn_body(h_ref, hp_ref, hn_ref, wg_ref, wv_ref, cwg_ref, cwv_ref, cbg_ref, cbv_ref, wd_ref,
              o_ref, hx_ref):
    i = pl.program_id(1)
    f = pl.program_id(2)
    tm = h_ref.shape[1]
    halo = BF16_ROWS

    @pl.when(f == 0)
    def _():
        zero = jnp.zeros((halo, h_ref.shape[2]), BF16)
        hx_ref[0:halo, :] = jnp.where(i == 0, zero, hp_ref[0])
        hx_ref[halo:halo + tm, :] = h_ref[0]
        hx_ref[halo + tm:, :] = jnp.where(i == pl.num_programs(1) - 1, zero, hn_ref[0])

    hx = hx_ref[...]
    rows = tm + 2 * halo

    def conv(w_ref, cw_ref, cb_ref):
        z = _dot(hx, w_ref[...])
        prev = pltpu.roll(z, 1, 0)[halo:halo + tm]
        nxt = pltpu.roll(z, rows - 1, 0)[halo:halo + tm]
        cw = cw_ref[...]
        return cw[0:1] * prev + cw[1:2] * z[halo:halo + tm] + cw[2:3] * nxt + cb_ref[...]

    gate = conv(wg_ref, cwg_ref, cbg_ref)
    val = conv(wv_ref, cwv_ref, cbv_ref)
    act = (jax.nn.silu(gate) * val).astype(BF16)
    part = _dot(act, wd_ref[...])

    @pl.when(f == 0)
    def _():
        o_ref[0] = part

    @pl.when(f != 0)
    def _():
        o_ref[0] += part


def _conv_ffn(h, w_up, conv_w, conv_b, w_down):
    b, t, d = h.shape
    d_ff = w_down.shape[0]
    tm = _pick(t, (512, 256, 128, 64, 32))
    tf = _pick(d_ff, (256, 128))
    nf = d_ff // tf
    hb = tm // BF16_ROWS
    nblk = t // BF16_ROWS
    cb = conv_b.reshape(1, 2 * d_ff)
    return pl.pallas_call(
        _ffn_body,
        out_shape=jax.ShapeDtypeStruct((b, t, d), F32),
        grid=(b, t // tm, nf),
        in_specs=[
            pl.BlockSpec((1, tm, d), lambda bi, i, f: (bi, i, 0)),
            pl.BlockSpec((1, BF16_ROWS, d), lambda bi, i, f: (bi, jnp.maximum(i * hb - 1, 0), 0)),
            pl.BlockSpec((1, BF16_ROWS, d), lambda bi, i, f: (bi, jnp.minimum((i + 1) * hb, nblk - 1), 0)),
            pl.BlockSpec((d, tf), lambda bi, i, f: (0, f)),
            pl.BlockSpec((d, tf), lambda bi, i, f: (0, nf + f)),
            pl.BlockSpec((3, tf), lambda bi, i, f: (0, f)),
            pl.BlockSpec((3, tf), lambda bi, i, f: (0, nf + f)),
            pl.BlockSpec((1, tf), lambda bi, i, f: (0, f)),
            pl.BlockSpec((1, tf), lambda bi, i, f: (0, nf + f)),
            pl.BlockSpec((tf, d), lambda bi, i, f: (f, 0)),
        ],
        out_specs=pl.BlockSpec((1, tm, d), lambda bi, i, f: (bi, i, 0)),
        scratch_shapes=[pltpu.VMEM((tm + 2 * BF16_ROWS, d), BF16)],
        compiler_params=_params("parallel", "parallel", "arbitrary"),
        name="conv_ffn",
    )(h, h, h, w_up, w_up, conv_w, conv_w, cb, cb, w_down)


def _trunk(x, mem, p):
    b, t, d = x.shape
    m = b * t
    n_mem = mem.shape[1]
    mix_a = p["ln_v_g"].shape[-1]
    mix_b = p["k_k"].shape[-1]
    n = p["r_k"].shape[-1]
    x2d = x.reshape(m, d)

    h1 = _rmsnorm([x2d], p["norm_mix"], BF16)
    za = _matmul([h1], [p["w_in_a"]], F32)
    zrkv = _matmul([h1], [p["w_in_b"]], F32)
    zl = _matmul([h1], [p["w_in_c"]], F32)

    ya = _gmlp(za, p["ln_v_g"], p["ln_v_b"], p["w_s"], p["b_s"])

    r, v, kk, g, bonus, lw, kd, a = _rwkv_prep(
        zrkv.reshape(b, t, 3 * mix_b), zl.reshape(b, t, -1), p["mu_shift"], p["w0"], p["w_up_decay"],
        p["a0"], p["w_up_iclr"], p["w_up_gate"], p["k_k"], p["k_a"], p["r_k"], p["jmat"])
    y = _wkv_scan(r, v, kk, lw, kd, a, n)
    yb = _rwkv_post(y.reshape(2, m, mix_b), bonus.reshape(m, mix_b), g.reshape(m, mix_b),
                    p["gn_g"], p["gn_b"], p["jmat"], n)

    x1 = _matmul([ya, yb], [p["w_out_a"], p["w_out_b"]], F32, residual=x2d)

    hq = _rmsnorm([x1], p["norm_x"], BF16)
    q = _matmul([hq], [p["w_q"]], BF16)
    memn = _rmsnorm([mem.reshape(b * n_mem, d)], p["norm_mem"], BF16)
    kv = _matmul([memn], [p["w_kv"]], BF16)
    o = _attention(q.reshape(b, t, d), kv.reshape(b, n_mem, 2 * d))
    x2 = _matmul([o.reshape(m, d)], [p["w_o"]], F32, residual=x1)

    hf = _rmsnorm([x2], p["norm_ffn"], BF16)
    ffn = _conv_ffn(hf.reshape(b, t, d), p["w_ffn_up"], p["conv_w"], p["conv_b"], p["w_ffn_down"])
    y = _rmsnorm([x2, ffn.reshape(m, d)], p["norm_out"], F32)
    return y.reshape(b, t, d)


def kernel(x_prompt, x_sample, mem_prompt, mem_sample, norm_mix, w_in, mu_shift, ln_v_g, ln_v_b, w_s, b_s, w0, w_up_decay, a0, w_up_iclr, w_up_gate, k_k, k_a, r_k, gn_g, gn_b, w_out, norm_x, norm_mem, w_q, w_kv, w_o, norm_ffn, w_ffn_up, conv_w, conv_b, w_ffn_down, norm_out):
    depth = w_in.shape[0]
    mix_a = ln_v_g.shape[-1]
    mix_b = k_k.shape[-1]
    n = r_k.shape[-1]
    head_id = jnp.arange(LANES) // n
    jmat = (head_id[:, None] == head_id[None, :]).astype(BF16)

    layers = []
    for l in range(depth):
        w_in_l = w_in[l].astype(BF16)
        w_out_l = w_out[l].astype(BF16)
        layers.append(dict(
            norm_mix=norm_mix[l], w_in_a=w_in_l[:, :2 * mix_a],
            w_in_b=w_in_l[:, 2 * mix_a:2 * mix_a + 3 * mix_b], w_in_c=w_in_l[:, 2 * mix_a + 3 * mix_b:],
            mu_shift=mu_shift[l], ln_v_g=ln_v_g[l], ln_v_b=ln_v_b[l], w_s=w_s[l], b_s=b_s[l],
            w0=w0[l], w_up_decay=w_up_decay[l], a0=a0[l], w_up_iclr=w_up_iclr[l], w_up_gate=w_up_gate[l],
            k_k=k_k[l], k_a=k_a[l], r_k=r_k[l], gn_g=gn_g[l], gn_b=gn_b[l],
            w_out_a=w_out_l[:mix_a], w_out_b=w_out_l[mix_a:], norm_x=norm_x[l], norm_mem=norm_mem[l],
            w_q=w_q[l].astype(BF16), w_kv=w_kv[l].astype(BF16), w_o=w_o[l].astype(BF16),
            norm_ffn=norm_ffn[l], w_ffn_up=w_ffn_up[l].astype(BF16), conv_w=conv_w[l], conv_b=conv_b[l],
            w_ffn_down=w_ffn_down[l].astype(BF16), jmat=jmat))

    (p,) = layers
    p = dict(p, norm_out=norm_out)
    return _trunk(x_prompt, mem_prompt, p), _trunk(x_sample, mem_sample, p)
```

```python
import functools

import jax
import jax.numpy as jnp
from jax import lax
from jax.experimental import pallas as pl
from jax.experimental.pallas import tpu as pltpu

RMS_EPS = 1e-6
LN_EPS = 1e-5
GN_EPS = 64e-5
L2_EPS = 1e-12
X_HEADS = 4

LANES = 128
BF16_ROWS = 16
SCAN_CHUNK = 64
SCAN_HEADS = 32
SCAN_PASSES = 1
INV_BASE = 8
VMEM_LIMIT = 56 * 1024 * 1024

F32 = jnp.float32
BF16 = jnp.bfloat16


def _params(*sem, flags=None):
    return pltpu.CompilerParams(dimension_semantics=sem, vmem_limit_bytes=VMEM_LIMIT, flags=flags)


def _pick(n, candidates):
    for c in candidates:
        if n % c == 0:
            return c
    raise ValueError(f"no tile for {n} in {candidates}")


def _dot(a, b):
    return jnp.dot(a, b, preferred_element_type=F32)


def _split(x):
    hi = x.astype(BF16)
    lo = (x - hi.astype(F32)).astype(BF16)
    return hi, lo


NN = (((1,), (0,)), ((), ()))
NT = (((1,), (1,)), ((), ()))
TN = (((0,), (0,)), ((), ()))


def _rmsnorm_body(*refs, n_in):
    g_ref, o_ref = refs[n_in], refs[n_in + 1]
    x = refs[0][...]
    for r in refs[1:n_in]:
        x = x + r[...]
    ms = jnp.mean(x * x, axis=-1, keepdims=True)
    o_ref[...] = (x * lax.rsqrt(ms + RMS_EPS) * g_ref[...]).astype(o_ref.dtype)


def _rmsnorm(xs, g, out_dtype):
    m, d = xs[0].shape
    tm = _pick(m, (256, 128, 64, 32, 16, 8))
    row = pl.BlockSpec((tm, d), lambda i: (i, 0))
    return pl.pallas_call(
        functools.partial(_rmsnorm_body, n_in=len(xs)),
        out_shape=jax.ShapeDtypeStruct((m, d), out_dtype),
        grid=(m // tm,),
        in_specs=[row] * len(xs) + [pl.BlockSpec((1, d), lambda i: (0, 0))],
        out_specs=row,
        compiler_params=_params("parallel"),
        name="rmsnorm",
    )(*xs, g.reshape(1, d))


def _mm_body(*refs, n_lhs, has_res):
    a_refs, w_refs = refs[:n_lhs], refs[n_lhs:2 * n_lhs]
    o_ref = refs[-1]
    acc = _dot(a_refs[0][...], w_refs[0][...])
    for a, w in zip(a_refs[1:], w_refs[1:]):
        acc = acc + _dot(a[...], w[...])
    if has_res:
        acc = acc + refs[2 * n_lhs][...]
    o_ref[...] = acc.astype(o_ref.dtype)


def _matmul(lhs, ws, out_dtype, residual=None):
    m = lhs[0].shape[0]
    n = ws[0].shape[1]
    tm = _pick(m, (1024, 512, 256, 128, 64, 32, 16, 8))
    tn = _pick(n, (512, 768, 256, 128))
    in_specs = [pl.BlockSpec((tm, a.shape[1]), lambda i, j: (i, 0)) for a in lhs]
    in_specs += [pl.BlockSpec((w.shape[0], tn), lambda i, j: (0, j)) for w in ws]
    args = list(lhs) + list(ws)
    if residual is not None:
        in_specs.append(pl.BlockSpec((tm, tn), lambda i, j: (i, j)))
        args.append(residual)
    return pl.pallas_call(
        functools.partial(_mm_body, n_lhs=len(lhs), has_res=residual is not None),
        out_shape=jax.ShapeDtypeStruct((m, n), out_dtype),
        grid=(m // tm, n // tn),
        in_specs=in_specs,
        out_specs=pl.BlockSpec((tm, tn), lambda i, j: (i, j)),
        compiler_params=_params("parallel", "parallel"),
        name="matmul",
    )(*args)


def _gmlp_body(zu_ref, zv_ref, g_ref, b_ref, ws_ref, bs_ref, o_ref, *, chunk):
    u = jax.nn.gelu(zu_ref[...])
    v = jax.nn.gelu(zv_ref[...])
    mu = jnp.mean(v, axis=-1, keepdims=True)
    var = jnp.mean(jnp.square(v - mu), axis=-1, keepdims=True)
    vn = ((v - mu) * lax.rsqrt(var + LN_EPS) * g_ref[...] + b_ref[...]).astype(BF16)
    rows, width = u.shape
    heads = ws_ref.shape[0]
    hd = width // heads
    bs = bs_ref[...]
    for c in range(rows // chunk):
        rs = slice(c * chunk, (c + 1) * chunk)
        mixed = jnp.concatenate(
            [_dot(ws_ref[h], vn[rs, h * hd:(h + 1) * hd]) for h in range(heads)], axis=1)
        o_ref[rs, :] = (u[rs, :] * (mixed + bs)).astype(o_ref.dtype)


def _gmlp(za, ln_g, ln_b, w_s, b_s):
    m, two_a = za.shape
    mix_a = two_a // 2
    heads, chunk, _ = w_s.shape
    rows = _pick(m, (2 * chunk, chunk))
    bs_full = jnp.repeat(b_s.T, mix_a // heads, axis=1)
    vec = pl.BlockSpec((1, mix_a), lambda i: (0, 0))
    return pl.pallas_call(
        functools.partial(_gmlp_body, chunk=chunk),
        out_shape=jax.ShapeDtypeStruct((m, mix_a), BF16),
        grid=(m // rows,),
        in_specs=[pl.BlockSpec((rows, mix_a), lambda i: (i, 0)),
                  pl.BlockSpec((rows, mix_a), lambda i: (i, 1)),
                  vec, vec,
                  pl.BlockSpec((heads, chunk, chunk), lambda i: (0, 0, 0)),
                  pl.BlockSpec((chunk, mix_a), lambda i: (0, 0))],
        out_specs=pl.BlockSpec((rows, mix_a), lambda i: (i, 0)),
        compiler_params=_params("parallel"),
        name="gmlp",
    )(za, za, ln_g.reshape(1, mix_a), ln_b.reshape(1, mix_a), w_s.astype(BF16), bs_full)


def _headsum(x, j_ref):
    tm, w = x.shape
    nb = w // LANES
    xs = jnp.concatenate([x[:, i * LANES:(i + 1) * LANES] for i in range(nb)], axis=0)
    hi, lo = _split(xs)
    j = j_ref[...]
    s = _dot(hi, j) + _dot(lo, j)
    return jnp.concatenate([s[i * tm:(i + 1) * tm] for i in range(nb)], axis=1)


def _tshift(z, zp, zn, mu, first, last):
    tm = z.shape[0]
    rows = lax.broadcasted_iota(jnp.int32, (tm, 1), 0)
    prev_row = jnp.where(first, 0.0, zp[7:8, :])
    next_row = jnp.where(last, 0.0, zn[0:1, :])
    prev = jnp.where(rows == 0, prev_row, pltpu.roll(z, 1, 0))
    nxt = jnp.where(rows == tm - 1, next_row, pltpu.roll(z, tm - 1, 0))
    return z + mu * (0.5 * (prev + nxt) - z)


def _prep_body(zr, zrp, zrn, zk, zkp, zkn, zv, zvp, zvn, zl, zlp, zln,
               mur, muk, muv, mul, w0_ref, wd_ref, a0_ref, wi_ref, wg_ref,
               kk_ref, ka_ref, rk_ref, j_ref,
               r_out, v_out, kk_out, g_out, bonus_out, lw_out, kd_out, a_out,
               *, lora_w, lora_a):
    i = pl.program_id(1)
    first = i == 0
    last = i == pl.num_programs(1) - 1
    r = _tshift(zr[0], zrp[0], zrn[0], mur[...], first, last)
    k = _tshift(zk[0], zkp[0], zkn[0], muk[...], first, last)
    v = _tshift(zv[0], zvp[0], zvn[0], muv[...], first, last)
    lo = _tshift(zl[0], zlp[0], zln[0], mul[...], first, last)
    xw = lo[:, :lora_w]
    xa = lo[:, lora_w:lora_w + lora_a]
    xg = lo[:, lora_w + lora_a:]

    g_out[0] = _dot(jax.nn.sigmoid(xg).astype(BF16), wg_ref[...])
    kkr = k * kk_ref[...]
    ss = _headsum(kkr * kkr, j_ref)
    kk = kkr * lax.rsqrt(jnp.maximum(ss, L2_EPS))
    tw = jnp.tanh(xw).astype(BF16)
    xab = xa.astype(BF16)
    rk = None
    for d in range(2):
        logw = -jax.nn.softplus(-(w0_ref[d:d + 1, :] + _dot(tw, wd_ref[d]))) - 0.5
        lw_out[d, 0] = -jnp.exp(logw)
        a = jax.nn.sigmoid(a0_ref[d:d + 1, :] + _dot(xab, wi_ref[d]))
        kd = k * (1.0 + (a - 1.0) * ka_ref[...])
        a_out[d, 0] = a
        kd_out[d, 0] = kd
        s = _headsum(r * kd * rk_ref[...], j_ref)
        rk = s if rk is None else rk + s
    r_out[0] = r
    v_out[0] = v
    kk_out[0] = kk
    bonus_out[0] = rk * v


def _rwkv_prep(zrkv, zl, mu, w0, w_up_decay, a0, w_up_iclr, w_up_gate, k_k, k_a, r_k, jmat):
    b, t, three_b = zrkv.shape
    mix_b = three_b // 3
    nl = zl.shape[-1]
    lora_w, lora_a = w_up_decay.shape[1], w_up_iclr.shape[1]
    tm = _pick(t, (128, 64, 32, 16, 8))
    hb = tm // 8
    nblk8 = t // 8

    def main(w, col):
        return pl.BlockSpec((1, tm, w), lambda bi, i: (bi, i, col))

    def prev(w, col):
        return pl.BlockSpec((1, 8, w), lambda bi, i: (bi, jnp.maximum(i * hb - 1, 0), col))

    def nxt(w, col):
        return pl.BlockSpec((1, 8, w), lambda bi, i: (bi, jnp.minimum((i + 1) * hb, nblk8 - 1), col))

    def const(shape):
        return pl.BlockSpec(shape, lambda bi, i: (0,) * len(shape))

    in_specs, args = [], []
    for col in range(3):
        in_specs += [main(mix_b, col), prev(mix_b, col), nxt(mix_b, col)]
        args += [zrkv] * 3
    in_specs += [main(nl, 0), prev(nl, 0), nxt(nl, 0)]
    args += [zl] * 3
    mu2 = mu.reshape(1, -1)
    in_specs += [pl.BlockSpec((1, mix_b), lambda bi, i, c=c: (0, c)) for c in range(3)]
    args += [mu2[:, :three_b]] * 3
    in_specs.append(const((1, nl)))
    args.append(mu2[:, three_b:])
    consts = [w0, w_up_decay.astype(BF16), a0, w_up_iclr.astype(BF16), w_up_gate.astype(BF16),
              k_k.reshape(1, mix_b), k_a.reshape(1, mix_b), r_k.reshape(1, mix_b), jmat]
    in_specs += [const(c.shape) for c in consts]
    args += consts

    shared = jax.ShapeDtypeStruct((b, t, mix_b), F32)
    perdir = jax.ShapeDtypeStruct((2, b, t, mix_b), F32)
    o_shared = pl.BlockSpec((1, tm, mix_b), lambda bi, i: (bi, i, 0))
    o_perdir = pl.BlockSpec((2, 1, tm, mix_b), lambda bi, i: (0, bi, i, 0))
    return pl.pallas_call(
        functools.partial(_prep_body, lora_w=lora_w, lora_a=lora_a),
        out_shape=[shared] * 5 + [perdir] * 3,
        grid=(b, t // tm),
        in_specs=in_specs,
        out_specs=[o_shared] * 5 + [o_perdir] * 3,
        compiler_params=_params("parallel", "parallel"),
        name="rwkv_prep",
    )(*args)


def _mm(a, b, dims, passes):
    dg = lambda x, y: lax.dot_general(x, y, dims, preferred_element_type=F32)
    if passes == 1:
        return dg(a.astype(BF16), b.astype(BF16))
    ah, al = _split(a)
    bh, bl = _split(b)
    return dg(ah, bh) + (dg(ah, bl) + dg(al, bh))


def _tri_inverse(ls, row, col, limit, passes):
    def same_block(size):
        sh = size.bit_length() - 1
        return (row >> sh) == (col >> sh)

    eye = (row == col).astype(F32)
    base = same_block(INV_BASE)
    ps = [jnp.where(base, l, 0.0) for l in ls]
    ts = [eye + p for p in ps]
    size = 2
    while size < INV_BASE:
        ps = [_mm(p, p, NN, passes) for p in ps]
        ts = [_mm(t, eye + p, NN, passes) for t, p in zip(ts, ps)]
        size *= 2
    size = INV_BASE
    while size < limit:
        sel = same_block(2 * size) & jnp.logical_not(same_block(size))
        offs = [_mm(jnp.where(sel, l, 0.0), t, NN, passes) for l, t in zip(ls, ts)]
        ts = [t + _mm(t, o, NN, passes) for t, o in zip(ts, offs)]
        size *= 2
    return ts


def _scan_body(r_ref, v_ref, kk_ref, lw_ref, kd_ref, a_ref, y_ref, s_ref, *, n, passes):
    d = pl.program_id(2)
    c = pl.program_id(3)
    cs = r_ref.shape[1]

    @pl.when(c == 0)
    def _():
        s_ref[...] = jnp.zeros_like(s_ref)

    fwd = d == 0
    sgn = 1 - 2 * d
    tiles = r_ref.shape[2] // LANES
    hp = LANES // n
    rows = hp * cs

    lw = lw_ref[0, 0]
    rc = lax.broadcasted_iota(jnp.int32, (cs, cs), 0)
    cc = lax.broadcasted_iota(jnp.int32, (cs, cs), 1)
    tri = ((rc - cc) * sgn >= 0).astype(BF16)
    l_hi = lw.astype(BF16)
    l_mid = (lw - l_hi.astype(F32)).astype(BF16)
    l_lo = (lw - l_hi.astype(F32) - l_mid.astype(F32)).astype(BF16)
    cum = _dot(tri, l_hi) + (_dot(tri, l_mid) + _dot(tri, l_lo))
    tot = jnp.where(fwd, cum[cs - 1:cs, :], cum[0:1, :])
    e_tot = jnp.exp(tot)

    r, v, kk = r_ref[0], v_ref[0], kk_ref[0]
    kd, a = kd_ref[0, 0], a_ref[0, 0]
    bvec = kk * a
    e_neg = jnp.exp(-cum)
    e_hat = jnp.exp(tot - cum)
    a_t = -kk * jnp.exp(cum - lw)
    r_t = r * jnp.exp(cum)
    b_t = bvec * e_neg
    k_t = kd * e_neg
    b_h = bvec * e_hat
    k_h = kd * e_hat

    row = lax.broadcasted_iota(jnp.int32, (rows, rows), 0)
    col = lax.broadcasted_iota(jnp.int32, (rows, rows), 1)
    order = ((row & (cs - 1)) - (col & (cs - 1))) * sgn
    strict = order > 0
    incl = order >= 0
    eye_l = (lax.broadcasted_iota(jnp.int32, (LANES, LANES), 0)
             == lax.broadcasted_iota(jnp.int32, (LANES, LANES), 1))
    lane_head = lax.broadcasted_iota(jnp.int32, (cs, LANES), 1) // n
    head_masks = [lane_head == h for h in range(hp)]

    def pack(x, p):
        xt = x[:, p * LANES:(p + 1) * LANES]
        return jnp.concatenate([jnp.where(mk, xt, 0.0) for mk in head_masks], axis=0)

    tl = range(tiles)
    at, rt, bt, kt, bh, kh, vv = ([pack(x, p) for p in tl] for x in (a_t, r_t, b_t, k_t, b_h, k_h, v))
    mm = functools.partial(_mm, passes=passes)
    cat = jnp.concatenate
    aa = [mm(cat([at[p], rt[p]], axis=0), cat([bt[p], kt[p]], axis=0), NT) for p in tl]
    aab = [jnp.where(strict, x[:rows, :rows], 0.0) for x in aa]
    aak = [jnp.where(strict, x[:rows, rows:], 0.0) for x in aa]
    aqb = [jnp.where(incl, x[rows:, :rows], 0.0) for x in aa]
    aqk = [jnp.where(incl, x[rows:, rows:], 0.0) for x in aa]
    tinv = _tri_inverse(aab, row, col, cs, passes)
    av = [mm(cat([aak[p], aqk[p]], axis=0), vv[p], NN) for p in tl]
    wu = [mm(tinv[p], cat([at[p], av[p][:rows]], axis=1), NN) for p in tl]
    qy = [mm(aqb[p], wu[p], NN) for p in tl]
    bw = [mm(bh[p], wu[p], TN) for p in tl]
    kv = [mm(kh[p], vv[p], TN) for p in tl]
    lhs = []
    for p in tl:
        m_mat = jnp.where(eye_l, e_tot[:, p * LANES:(p + 1) * LANES], 0.0) + bw[p][:, :LANES]
        lhs.append(cat([m_mat, rt[p] + qy[p][:, :LANES]], axis=0))
    ms = [mm(lhs[p], s_ref[p], NN) for p in tl]
    outs = []
    for p in tl:
        s_ref[p] = ms[p][:LANES] + (kv[p] + bw[p][:, LANES:])
        ybd = ms[p][LANES:] + (av[p][rows:] + qy[p][:, LANES:])
        y = ybd[:cs]
        for h in range(1, hp):
            y = y + ybd[h * cs:(h + 1) * cs]
        outs.append(y)
    y_ref[0, 0] = cat(outs, axis=1)


def _wkv_scan(r, v, kk, lw, kd, a, n):
    b, t, mix_b = r.shape
    heads = mix_b // n
    hg = SCAN_HEADS if heads % SCAN_HEADS == 0 else heads
    cs = SCAN_CHUNK
    nc = t // cs
    w = hg * n

    def tchunk(di, ci):
        return ci + di * (nc - 1 - 2 * ci)

    shared = pl.BlockSpec((1, cs, w), lambda bi, hi, di, ci: (bi, tchunk(di, ci), hi))
    perdir = pl.BlockSpec((1, 1, cs, w), lambda bi, hi, di, ci: (di, bi, tchunk(di, ci), hi))
    return pl.pallas_call(
        functools.partial(_scan_body, n=n, passes=SCAN_PASSES),
        out_shape=jax.ShapeDtypeStruct((2, b, t, mix_b), F32),
        grid=(b, heads // hg, 2, nc),
        in_specs=[shared] * 3 + [perdir] * 3,
        out_specs=perdir,
        scratch_shapes=[pltpu.VMEM((w // LANES, LANES, LANES), F32)],
        compiler_params=_params("parallel", "parallel", "arbitrary", "arbitrary"),
        name="wkv_scan",
    )(r, v, kk, lw, kd, a)


def _post_body(y_ref, bonus_ref, g_ref, gg_ref, gb_ref, j_ref, o_ref, *, n):
    y = y_ref[0] + y_ref[1]
    mean = _headsum(y, j_ref) * (1.0 / n)
    yc = y - mean
    var = _headsum(yc * yc, j_ref) * (1.0 / n)
    yn = yc * lax.rsqrt(var + GN_EPS) * gg_ref[...] + gb_ref[...]
    o_ref[...] = ((yn + bonus_ref[...]) * g_ref[...]).astype(o_ref.dtype)


def _rwkv_post(y, bonus, g, gn_g, gn_b, jmat, n):
    _, m, mix_b = y.shape
    tm = _pick(m, (256, 128, 64, 32, 16, 8))
    row = pl.BlockSpec((tm, mix_b), lambda i: (i, 0))
    vec = pl.BlockSpec((1, mix_b), lambda i: (0, 0))
    return pl.pallas_call(
        functools.partial(_post_body, n=n),
        out_shape=jax.ShapeDtypeStruct((m, mix_b), BF16),
        grid=(m // tm,),
        in_specs=[pl.BlockSpec((2, tm, mix_b), lambda i: (0, i, 0)), row, row, vec, vec,
                  pl.BlockSpec(jmat.shape, lambda i: (0, 0))],
        out_specs=row,
        compiler_params=_params("parallel"),
        name="rwkv_post",
    )(y, bonus, g, gn_g.reshape(1, mix_b), gn_b.reshape(1, mix_b), jmat)


def _attn_body(q_ref, kv_ref, o_ref, *, heads):
    d = q_ref.shape[-1]
    hd = d // heads
    scale = hd ** -0.5
    for h in range(heads):
        q = q_ref[0, :, h * hd:(h + 1) * hd]
        k = kv_ref[0, :, h * hd:(h + 1) * hd]
        v = kv_ref[0, :, d + h * hd:d + (h + 1) * hd]
        s = lax.dot_general(q, k, NT, preferred_element_type=F32) * scale
        e = jnp.exp(s - jnp.max(s, axis=-1, keepdims=True))
        p = e / jnp.sum(e, axis=-1, keepdims=True)
        o_ref[0, :, h * hd:(h + 1) * hd] = _dot(p.astype(BF16), v).astype(o_ref.dtype)


def _attention(q, kv):
    b, t, d = q.shape
    n_mem = kv.shape[1]
    tq = _pick(t, (512, 256, 128, 64, 32, 16))
    return pl.pallas_call(
        functools.partial(_attn_body, heads=X_HEADS),
        out_shape=jax.ShapeDtypeStruct((b, t, d), BF16),
        grid=(b, t // tq),
        in_specs=[pl.BlockSpec((1, tq, d), lambda bi, i: (bi, i, 0)),
                  pl.BlockSpec((1, n_mem, 2 * d), lambda bi, i: (bi, 0, 0))],
        out_specs=pl.BlockSpec((1, tq, d), lambda bi, i: (bi, i, 0)),
        compiler_params=_params("parallel", "parallel"),
        name="cross_attention",
    )(q, kv)


def _ffn_body(h_ref, hp_ref, hn_ref, wg_ref, wv_ref, cwg_ref, cwv_ref, cbg_ref, cbv_ref, wd_ref,
              o_ref, hx_ref):
    i = pl.program_id(1)
    f = pl.program_id(2)
    tm = h_ref.shape[1]
    halo = BF16_ROWS

    @pl.when(f == 0)
    def _():
        zero = jnp.zeros((halo, h_ref.shape[2]), BF16)
        hx_ref[0:halo, :] = jnp.where(i == 0, zero, hp_ref[0])
        hx_ref[halo:halo + tm, :] = h_ref[0]
        hx_ref[halo + tm:, :] = jnp.where(i == pl.num_programs(1) - 1, zero, hn_ref[0])

    hx = hx_ref[...]
    rows = tm + 2 * halo

    def conv(w_ref, cw_ref, cb_ref):
        z = _dot(hx, w_ref[...])
        prev = pltpu.roll(z, 1, 0)[halo:halo + tm]
        nxt = pltpu.roll(z, rows - 1, 0)[halo:halo + tm]
        cw = cw_ref[...]
        return cw[0:1] * prev + cw[1:2] * z[halo:halo + tm] + cw[2:3] * nxt + cb_ref[...]

    gate = conv(wg_ref, cwg_ref, cbg_ref)
    val = conv(wv_ref, cwv_ref, cbv_ref)
    act = (jax.nn.silu(gate) * val).astype(BF16)
    part = _dot(act, wd_ref[...])

    @pl.when(f == 0)
    def _():
        o_ref[0] = part

    @pl.when(f != 0)
    def _():
        o_ref[0] += part


def _conv_ffn(h, w_up, conv_w, conv_b, w_down):
    b, t, d = h.shape
    d_ff = w_down.shape[0]
    tm = _pick(t, (512, 256, 128, 64, 32))
    tf = _pick(d_ff, (256, 128))
    nf = d_ff // tf
    hb = tm // BF16_ROWS
    nblk = t // BF16_ROWS
    cb = conv_b.reshape(1, 2 * d_ff)
    return pl.pallas_call(
        _ffn_body,
        out_shape=jax.ShapeDtypeStruct((b, t, d), F32),
        grid=(b, t // tm, nf),
        in_specs=[
            pl.BlockSpec((1, tm, d), lambda bi, i, f: (bi, i, 0)),
            pl.BlockSpec((1, BF16_ROWS, d), lambda bi, i, f: (bi, jnp.maximum(i * hb - 1, 0), 0)),
            pl.BlockSpec((1, BF16_ROWS, d), lambda bi, i, f: (bi, jnp.minimum((i + 1) * hb, nblk - 1), 0)),
            pl.BlockSpec((d, tf), lambda bi, i, f: (0, f)),
            pl.BlockSpec((d, tf), lambda bi, i, f: (0, nf + f)),
            pl.BlockSpec((3, tf), lambda bi, i, f: (0, f)),
            pl.BlockSpec((3, tf), lambda bi, i, f: (0, nf + f)),
            pl.BlockSpec((1, tf), lambda bi, i, f: (0, f)),
            pl.BlockSpec((1, tf), lambda bi, i, f: (0, nf + f)),
            pl.BlockSpec((tf, d), lambda bi, i, f: (f, 0)),
        ],
        out_specs=pl.BlockSpec((1, tm, d), lambda bi, i, f: (bi, i, 0)),
        scratch_shapes=[pltpu.VMEM((tm + 2 * BF16_ROWS, d), BF16)],
        compiler_params=_params("parallel", "parallel", "arbitrary"),
        name="conv_ffn",
    )(h, h, h, w_up, w_up, conv_w, conv_w, cb, cb, w_down)


def _trunk(x, mem, p):
    b, t, d = x.shape
    m = b * t
    n_mem = mem.shape[1]
    mix_a = p["ln_v_g"].shape[-1]
    mix_b = p["k_k"].shape[-1]
    n = p["r_k"].shape[-1]
    x2d = x.reshape(m, d)

    h1 = _rmsnorm([x2d], p["norm_mix"], BF16)
    za = _matmul([h1], [p["w_in_a"]], F32)
    zrkv = _matmul([h1], [p["w_in_b"]], F32)
    zl = _matmul([h1], [p["w_in_c"]], F32)

    ya = _gmlp(za, p["ln_v_g"], p["ln_v_b"], p["w_s"], p["b_s"])

    r, v, kk, g, bonus, lw, kd, a = _rwkv_prep(
        zrkv.reshape(b, t, 3 * mix_b), zl.reshape(b, t, -1), p["mu_shift"], p["w0"], p["w_up_decay"],
        p["a0"], p["w_up_iclr"], p["w_up_gate"], p["k_k"], p["k_a"], p["r_k"], p["jmat"])
    y = _wkv_scan(r, v, kk, lw, kd, a, n)
    yb = _rwkv_post(y.reshape(2, m, mix_b), bonus.reshape(m, mix_b), g.reshape(m, mix_b),
                    p["gn_g"], p["gn_b"], p["jmat"], n)

    x1 = _matmul([ya, yb], [p["w_out_a"], p["w_out_b"]], F32, residual=x2d)

    hq = _rmsnorm([x1], p["norm_x"], BF16)
    q = _matmul([hq], [p["w_q"]], BF16)
    memn = _rmsnorm([mem.reshape(b * n_mem, d)], p["norm_mem"], BF16)
    kv = _matmul([memn], [p["w_kv"]], BF16)
    o = _attention(q.reshape(b, t, d), kv.reshape(b, n_mem, 2 * d))
    x2 = _matmul([o.reshape(m, d)], [p["w_o"]], F32, residual=x1)

    hf = _rmsnorm([x2], p["norm_ffn"], BF16)
    ffn = _conv_ffn(hf.reshape(b, t, d), p["w_ffn_up"], p["conv_w"], p["conv_b"], p["w_ffn_down"])
    y = _rmsnorm([x2, ffn.reshape(m, d)], p["norm_out"], F32)
    return y.reshape(b, t, d)


def kernel(x_prompt, x_sample, mem_prompt, mem_sample, norm_mix, w_in, mu_shift, ln_v_g, ln_v_b, w_s, b_s, w0, w_up_decay, a0, w_up_iclr, w_up_gate, k_k, k_a, r_k, gn_g, gn_b, w_out, norm_x, norm_mem, w_q, w_kv, w_o, norm_ffn, w_ffn_up, conv_w, conv_b, w_ffn_down, norm_out):
    depth = w_in.shape[0]
    mix_a = ln_v_g.shape[-1]
    mix_b = k_k.shape[-1]
    n = r_k.shape[-1]
    head_id = jnp.arange(LANES) // n
    jmat = (head_id[:, None] == head_id[None, :]).astype(BF16)

    layers = []
    for l in range(depth):
        w_in_l = w_in[l].astype(BF16)
        w_out_l = w_out[l].astype(BF16)
        layers.append(dict(
            norm_mix=norm_mix[l], w_in_a=w_in_l[:, :2 * mix_a],
            w_in_b=w_in_l[:, 2 * mix_a:2 * mix_a + 3 * mix_b], w_in_c=w_in_l[:, 2 * mix_a + 3 * mix_b:],
            mu_shift=mu_shift[l], ln_v_g=ln_v_g[l], ln_v_b=ln_v_b[l], w_s=w_s[l], b_s=b_s[l],
            w0=w0[l], w_up_decay=w_up_decay[l], a0=a0[l], w_up_iclr=w_up_iclr[l], w_up_gate=w_up_gate[l],
            k_k=k_k[l], k_a=k_a[l], r_k=r_k[l], gn_g=gn_g[l], gn_b=gn_b[l],
            w_out_a=w_out_l[:mix_a], w_out_b=w_out_l[mix_a:], norm_x=norm_x[l], norm_mem=norm_mem[l],
            w_q=w_q[l].astype(BF16), w_kv=w_kv[l].astype(BF16), w_o=w_o[l].astype(BF16),
            norm_ffn=norm_ffn[l], w_ffn_up=w_ffn_up[l].astype(BF16), conv_w=conv_w[l], conv_b=conv_b[l],
            w_ffn_down=w_ffn_down[l].astype(BF16), jmat=jmat))

    (p,) = layers
    p = dict(p, norm_out=norm_out)
    return _trunk(x_prompt, mem_prompt, p), _trunk(x_sample, mem_sample, p)
```

```python
import functools

import jax
import jax.numpy as jnp
from jax import lax
from jax.experimental import pallas as pl
from jax.experimental.pallas import tpu as pltpu

RMS_EPS = 1e-6
LN_EPS = 1e-5
GN_EPS = 64e-5
L2_EPS = 1e-12
X_HEADS = 4

LANES = 128
BF16_ROWS = 16
SCAN_CHUNK = 64
SCAN_HEADS = 32
SCAN_PASSES = 1
INV_BASE = 8
VMEM_LIMIT = 56 * 1024 * 1024

F32 = jnp.float32
BF16 = jnp.bfloat16


def _params(*sem, flags=None):
    return pltpu.CompilerParams(dimension_semantics=sem, vmem_limit_bytes=VMEM_LIMIT, flags=flags)


def _pick(n, candidates):
    for c in candidates:
        if n % c == 0:
            return c
    raise ValueError(f"no tile for {n} in {candidates}")


def _dot(a, b):
    return jnp.dot(a, b, preferred_element_type=F32)


def _split(x):
    hi = x.astype(BF16)
    lo = (x - hi.astype(F32)).astype(BF16)
    return hi, lo


NN = (((1,), (0,)), ((), ()))
NT = (((1,), (1,)), ((), ()))
TN = (((0,), (0,)), ((), ()))


def _rmsnorm_body(*refs, n_in):
    g_ref, o_ref = refs[n_in], refs[n_in + 1]
    x = refs[0][...]
    for r in refs[1:n_in]:
        x = x + r[...]
    ms = jnp.mean(x * x, axis=-1, keepdims=True)
    o_ref[...] = (x * lax.rsqrt(ms + RMS_EPS) * g_ref[...]).astype(o_ref.dtype)


def _rmsnorm(xs, g, out_dtype):
    m, d = xs[0].shape
    tm = _pick(m, (256, 128, 64, 32, 16, 8))
    row = pl.BlockSpec((tm, d), lambda i: (i, 0))
    return pl.pallas_call(
        functools.partial(_rmsnorm_body, n_in=len(xs)),
        out_shape=jax.ShapeDtypeStruct((m, d), out_dtype),
        grid=(m // tm,),
        in_specs=[row] * len(xs) + [pl.BlockSpec((1, d), lambda i: (0, 0))],
        out_specs=row,
        compiler_params=_params("parallel"),
        name="rmsnorm",
    )(*xs, g.reshape(1, d))


def _mm_body(*refs, n_lhs, has_res):
    a_refs, w_refs = refs[:n_lhs], refs[n_lhs:2 * n_lhs]
    o_ref = refs[-1]
    acc = _dot(a_refs[0][...], w_refs[0][...])
    for a, w in zip(a_refs[1:], w_refs[1:]):
        acc = acc + _dot(a[...], w[...])
    if has_res:
        acc = acc + refs[2 * n_lhs][...]
    o_ref[...] = acc.astype(o_ref.dtype)


def _matmul(lhs, ws, out_dtype, residual=None):
    m = lhs[0].shape[0]
    n = ws[0].shape[1]
    tm = _pick(m, (1024, 512, 256, 128, 64, 32, 16, 8))
    tn = _pick(n, (1024, 768, 512, 256, 128))
    in_specs = [pl.BlockSpec((tm, a.shape[1]), lambda i, j: (i, 0)) for a in lhs]
    in_specs += [pl.BlockSpec((w.shape[0], tn), lambda i, j: (0, j)) for w in ws]
    args = list(lhs) + list(ws)
    if residual is not None:
        in_specs.append(pl.BlockSpec((tm, tn), lambda i, j: (i, j)))
        args.append(residual)
    return pl.pallas_call(
        functools.partial(_mm_body, n_lhs=len(lhs), has_res=residual is not None),
        out_shape=jax.ShapeDtypeStruct((m, n), out_dtype),
        grid=(m // tm, n // tn),
        in_specs=in_specs,
        out_specs=pl.BlockSpec((tm, tn), lambda i, j: (i, j)),
        compiler_params=_params("parallel", "parallel"),
        name="matmul",
    )(*args)


def _gmlp_body(zu_ref, zv_ref, g_ref, b_ref, ws_ref, bs_ref, o_ref, *, chunk):
    u = jax.nn.gelu(zu_ref[...])
    v = jax.nn.gelu(zv_ref[...])
    mu = jnp.mean(v, axis=-1, keepdims=True)
    var = jnp.mean(jnp.square(v - mu), axis=-1, keepdims=True)
    vn = ((v - mu) * lax.rsqrt(var + LN_EPS) * g_ref[...] + b_ref[...]).astype(BF16)
    rows, width = u.shape
    heads = ws_ref.shape[0]
    hd = width // heads
    bs = bs_ref[...]
    for c in range(rows // chunk):
        rs = slice(c * chunk, (c + 1) * chunk)
        mixed = jnp.concatenate(
            [_dot(ws_ref[h], vn[rs, h * hd:(h + 1) * hd]) for h in range(heads)], axis=1)
        o_ref[rs, :] = (u[rs, :] * (mixed + bs)).astype(o_ref.dtype)


def _gmlp(za, ln_g, ln_b, w_s, b_s):
    m, two_a = za.shape
    mix_a = two_a // 2
    heads, chunk, _ = w_s.shape
    rows = _pick(m, (2 * chunk, chunk))
    bs_full = jnp.repeat(b_s.T, mix_a // heads, axis=1)
    vec = pl.BlockSpec((1, mix_a), lambda i: (0, 0))
    return pl.pallas_call(
        functools.partial(_gmlp_body, chunk=chunk),
        out_shape=jax.ShapeDtypeStruct((m, mix_a), BF16),
        grid=(m // rows,),
        in_specs=[pl.BlockSpec((rows, mix_a), lambda i: (i, 0)),
                  pl.BlockSpec((rows, mix_a), lambda i: (i, 1)),
                  vec, vec,
                  pl.BlockSpec((heads, chunk, chunk), lambda i: (0, 0, 0)),
                  pl.BlockSpec((chunk, mix_a), lambda i: (0, 0))],
        out_specs=pl.BlockSpec((rows, mix_a), lambda i: (i, 0)),
        compiler_params=_params("parallel"),
        name="gmlp",
    )(za, za, ln_g.reshape(1, mix_a), ln_b.reshape(1, mix_a), w_s.astype(BF16), bs_full)


def _headsum(x, j_ref):
    tm, w = x.shape
    nb = w // LANES
    xs = jnp.concatenate([x[:, i * LANES:(i + 1) * LANES] for i in range(nb)], axis=0)
    hi, lo = _split(xs)
    j = j_ref[...]
    s = _dot(hi, j) + _dot(lo, j)
    return jnp.concatenate([s[i * tm:(i + 1) * tm] for i in range(nb)], axis=1)


def _tshift(z, zp, zn, mu, first, last):
    tm = z.shape[0]
    rows = lax.broadcasted_iota(jnp.int32, (tm, 1), 0)
    prev_row = jnp.where(first, 0.0, zp[7:8, :])
    next_row = jnp.where(last, 0.0, zn[0:1, :])
    prev = jnp.where(rows == 0, prev_row, pltpu.roll(z, 1, 0))
    nxt = jnp.where(rows == tm - 1, next_row, pltpu.roll(z, tm - 1, 0))
    return z + mu * (0.5 * (prev + nxt) - z)


def _prep_body(zr, zrp, zrn, zk, zkp, zkn, zv, zvp, zvn, zl, zlp, zln,
               mur, muk, muv, mul, w0_ref, wd_ref, a0_ref, wi_ref, wg_ref,
               kk_ref, ka_ref, rk_ref, j_ref,
               r_out, v_out, kk_out, g_out, bonus_out, lw_out, kd_out, a_out,
               *, lora_w, lora_a):
    i = pl.program_id(1)
    first = i == 0
    last = i == pl.num_programs(1) - 1
    r = _tshift(zr[0], zrp[0], zrn[0], mur[...], first, last)
    k = _tshift(zk[0], zkp[0], zkn[0], muk[...], first, last)
    v = _tshift(zv[0], zvp[0], zvn[0], muv[...], first, last)
    lo = _tshift(zl[0], zlp[0], zln[0], mul[...], first, last)
    xw = lo[:, :lora_w]
    xa = lo[:, lora_w:lora_w + lora_a]
    xg = lo[:, lora_w + lora_a:]

    g_out[0] = _dot(jax.nn.sigmoid(xg).astype(BF16), wg_ref[...])
    kkr = k * kk_ref[...]
    ss = _headsum(kkr * kkr, j_ref)
    kk = kkr * lax.rsqrt(jnp.maximum(ss, L2_EPS))
    tw = jnp.tanh(xw).astype(BF16)
    xab = xa.astype(BF16)
    rk = None
    for d in range(2):
        logw = -jax.nn.softplus(-(w0_ref[d:d + 1, :] + _dot(tw, wd_ref[d]))) - 0.5
        lw_out[d, 0] = -jnp.exp(logw)
        a = jax.nn.sigmoid(a0_ref[d:d + 1, :] + _dot(xab, wi_ref[d]))
        kd = k * (1.0 + (a - 1.0) * ka_ref[...])
        a_out[d, 0] = a
        kd_out[d, 0] = kd
        s = _headsum(r * kd * rk_ref[...], j_ref)
        rk = s if rk is None else rk + s
    r_out[0] = r
    v_out[0] = v
    kk_out[0] = kk
    bonus_out[0] = rk * v


def _rwkv_prep(zrkv, zl, mu, w0, w_up_decay, a0, w_up_iclr, w_up_gate, k_k, k_a, r_k, jmat):
    b, t, three_b = zrkv.shape
    mix_b = three_b // 3
    nl = zl.shape[-1]
    lora_w, lora_a = w_up_decay.shape[1], w_up_iclr.shape[1]
    tm = _pick(t, (128, 64, 32, 16, 8))
    hb = tm // 8
    nblk8 = t // 8

    def main(w, col):
        return pl.BlockSpec((1, tm, w), lambda bi, i: (bi, i, col))

    def prev(w, col):
        return pl.BlockSpec((1, 8, w), lambda bi, i: (bi, jnp.maximum(i * hb - 1, 0), col))

    def nxt(w, col):
        return pl.BlockSpec((1, 8, w), lambda bi, i: (bi, jnp.minimum((i + 1) * hb, nblk8 - 1), col))

    def const(shape):
        return pl.BlockSpec(shape, lambda bi, i: (0,) * len(shape))

    in_specs, args = [], []
    for col in range(3):
        in_specs += [main(mix_b, col), prev(mix_b, col), nxt(mix_b, col)]
        args += [zrkv] * 3
    in_specs += [main(nl, 0), prev(nl, 0), nxt(nl, 0)]
    args += [zl] * 3
    mu2 = mu.reshape(1, -1)
    in_specs += [pl.BlockSpec((1, mix_b), lambda bi, i, c=c: (0, c)) for c in range(3)]
    args += [mu2[:, :three_b]] * 3
    in_specs.append(const((1, nl)))
    args.append(mu2[:, three_b:])
    consts = [w0, w_up_decay.astype(BF16), a0, w_up_iclr.astype(BF16), w_up_gate.astype(BF16),
              k_k.reshape(1, mix_b), k_a.reshape(1, mix_b), r_k.reshape(1, mix_b), jmat]
    in_specs += [const(c.shape) for c in consts]
    args += consts

    shared = jax.ShapeDtypeStruct((b, t, mix_b), F32)
    perdir = jax.ShapeDtypeStruct((2, b, t, mix_b), F32)
    o_shared = pl.BlockSpec((1, tm, mix_b), lambda bi, i: (bi, i, 0))
    o_perdir = pl.BlockSpec((2, 1, tm, mix_b), lambda bi, i: (0, bi, i, 0))
    return pl.pallas_call(
        functools.partial(_prep_body, lora_w=lora_w, lora_a=lora_a),
        out_shape=[shared] * 5 + [perdir] * 3,
        grid=(b, t // tm),
        in_specs=in_specs,
        out_specs=[o_shared] * 5 + [o_perdir] * 3,
        compiler_params=_params("parallel", "parallel"),
        name="rwkv_prep",
    )(*args)


def _mm(a, b, dims, passes):
    dg = lambda x, y: lax.dot_general(x, y, dims, preferred_element_type=F32)
    if passes == 1:
        return dg(a.astype(BF16), b.astype(BF16))
    ah, al = _split(a)
    bh, bl = _split(b)
    return dg(ah, bh) + (dg(ah, bl) + dg(al, bh))


def _tri_inverse(ls, row, col, limit, passes):
    def same_block(size):
        sh = size.bit_length() - 1
        return (row >> sh) == (col >> sh)

    eye = (row == col).astype(F32)
    base = same_block(INV_BASE)
    ps = [jnp.where(base, l, 0.0) for l in ls]
    ts = [eye + p for p in ps]
    size = 2
    while size < INV_BASE:
        ps = [_mm(p, p, NN, passes) for p in ps]
        ts = [_mm(t, eye + p, NN, passes) for t, p in zip(ts, ps)]
        size *= 2
    size = INV_BASE
    while size < limit:
        sel = same_block(2 * size) & jnp.logical_not(same_block(size))
        offs = [_mm(jnp.where(sel, l, 0.0), t, NN, passes) for l, t in zip(ls, ts)]
        ts = [t + _mm(t, o, NN, passes) for t, o in zip(ts, offs)]
        size *= 2
    return ts


def _scan_body(r_ref, v_ref, kk_ref, lw_ref, kd_ref, a_ref, y_ref, s_ref, *, n, passes):
    d = pl.program_id(2)
    c = pl.program_id(3)
    cs = r_ref.shape[1]

    @pl.when(c == 0)
    def _():
        s_ref[...] = jnp.zeros_like(s_ref)

    fwd = d == 0
    sgn = 1 - 2 * d
    tiles = r_ref.shape[2] // LANES
    hp = LANES // n
    rows = hp * cs

    lw = lw_ref[0, 0]
    rc = lax.broadcasted_iota(jnp.int32, (cs, cs), 0)
    cc = lax.broadcasted_iota(jnp.int32, (cs, cs), 1)
    tri = ((rc - cc) * sgn >= 0).astype(BF16)
    l_hi = lw.astype(BF16)
    l_mid = (lw - l_hi.astype(F32)).astype(BF16)
    l_lo = (lw - l_hi.astype(F32) - l_mid.astype(F32)).astype(BF16)
    cum = _dot(tri, l_hi) + (_dot(tri, l_mid) + _dot(tri, l_lo))
    tot = jnp.where(fwd, cum[cs - 1:cs, :], cum[0:1, :])
    e_tot = jnp.exp(tot)

    r, v, kk = r_ref[0], v_ref[0], kk_ref[0]
    kd, a = kd_ref[0, 0], a_ref[0, 0]
    bvec = kk * a
    e_neg = jnp.exp(-cum)
    e_hat = jnp.exp(tot - cum)
    a_t = -kk * jnp.exp(cum - lw)
    r_t = r * jnp.exp(cum)
    b_t = bvec * e_neg
    k_t = kd * e_neg
    b_h = bvec * e_hat
    k_h = kd * e_hat

    row = lax.broadcasted_iota(jnp.int32, (rows, rows), 0)
    col = lax.broadcasted_iota(jnp.int32, (rows, rows), 1)
    order = ((row & (cs - 1)) - (col & (cs - 1))) * sgn
    strict = order > 0
    incl = order >= 0
    eye_l = (lax.broadcasted_iota(jnp.int32, (LANES, LANES), 0)
             == lax.broadcasted_iota(jnp.int32, (LANES, LANES), 1))
    lane_head = lax.broadcasted_iota(jnp.int32, (cs, LANES), 1) // n
    head_masks = [lane_head == h for h in range(hp)]

    def pack(x, p):
        xt = x[:, p * LANES:(p + 1) * LANES]
        return jnp.concatenate([jnp.where(mk, xt, 0.0) for mk in head_masks], axis=0)

    tl = range(tiles)
    at, rt, bt, kt, bh, kh, vv = ([pack(x, p) for p in tl] for x in (a_t, r_t, b_t, k_t, b_h, k_h, v))
    mm = functools.partial(_mm, passes=passes)
    cat = jnp.concatenate
    aa = [mm(cat([at[p], rt[p]], axis=0), cat([bt[p], kt[p]], axis=0), NT) for p in tl]
    aab = [jnp.where(strict, x[:rows, :rows], 0.0) for x in aa]
    aak = [jnp.where(strict, x[:rows, rows:], 0.0) for x in aa]
    aqb = [jnp.where(incl, x[rows:, :rows], 0.0) for x in aa]
    aqk = [jnp.where(incl, x[rows:, rows:], 0.0) for x in aa]
    tinv = _tri_inverse(aab, row, col, cs, passes)
    av = [mm(cat([aak[p], aqk[p]], axis=0), vv[p], NN) for p in tl]
    wu = [mm(tinv[p], cat([at[p], av[p][:rows]], axis=1), NN) for p in tl]
    qy = [mm(aqb[p], wu[p], NN) for p in tl]
    bw = [mm(bh[p], wu[p], TN) for p in tl]
    kv = [mm(kh[p], vv[p], TN) for p in tl]
    lhs = []
    for p in tl:
        m_mat = jnp.where(eye_l, e_tot[:, p * LANES:(p + 1) * LANES], 0.0) + bw[p][:, :LANES]
        lhs.append(cat([m_mat, rt[p] + qy[p][:, :LANES]], axis=0))
    ms = [mm(lhs[p], s_ref[p], NN) for p in tl]
    outs = []
    for p in tl:
        s_ref[p] = ms[p][:LANES] + (kv[p] + bw[p][:, LANES:])
        ybd = ms[p][LANES:] + (av[p][rows:] + qy[p][:, LANES:])
        y = ybd[:cs]
        for h in range(1, hp):
            y = y + ybd[h * cs:(h + 1) * cs]
        outs.append(y)
    y_ref[0, 0] = cat(outs, axis=1)


def _wkv_scan(r, v, kk, lw, kd, a, n):
    b, t, mix_b = r.shape
    heads = mix_b // n
    hg = SCAN_HEADS if heads % SCAN_HEADS == 0 else heads
    cs = SCAN_CHUNK
    nc = t // cs
    w = hg * n

    def tchunk(di, ci):
        return ci + di * (nc - 1 - 2 * ci)

    shared = pl.BlockSpec((1, cs, w), lambda bi, hi, di, ci: (bi, tchunk(di, ci), hi))
    perdir = pl.BlockSpec((1, 1, cs, w), lambda bi, hi, di, ci: (di, bi, tchunk(di, ci), hi))
    return pl.pallas_call(
        functools.partial(_scan_body, n=n, passes=SCAN_PASSES),
        out_shape=jax.ShapeDtypeStruct((2, b, t, mix_b), F32),
        grid=(b, heads // hg, 2, nc),
        in_specs=[shared] * 3 + [perdir] * 3,
        out_specs=perdir,
        scratch_shapes=[pltpu.VMEM((w // LANES, LANES, LANES), F32)],
        compiler_params=_params("parallel", "parallel", "arbitrary", "arbitrary"),
        name="wkv_scan",
    )(r, v, kk, lw, kd, a)


def _post_body(y_ref, bonus_ref, g_ref, gg_ref, gb_ref, j_ref, o_ref, *, n):
    y = y_ref[0] + y_ref[1]
    mean = _headsum(y, j_ref) * (1.0 / n)
    yc = y - mean
    var = _headsum(yc * yc, j_ref) * (1.0 / n)
    yn = yc * lax.rsqrt(var + GN_EPS) * gg_ref[...] + gb_ref[...]
    o_ref[...] = ((yn + bonus_ref[...]) * g_ref[...]).astype(o_ref.dtype)


def _rwkv_post(y, bonus, g, gn_g, gn_b, jmat, n):
    _, m, mix_b = y.shape
    tm = _pick(m, (256, 128, 64, 32, 16, 8))
    row = pl.BlockSpec((tm, mix_b), lambda i: (i, 0))
    vec = pl.BlockSpec((1, mix_b), lambda i: (0, 0))
    return pl.pallas_call(
        functools.partial(_post_body, n=n),
        out_shape=jax.ShapeDtypeStruct((m, mix_b), BF16),
        grid=(m // tm,),
        in_specs=[pl.BlockSpec((2, tm, mix_b), lambda i: (0, i, 0)), row, row, vec, vec,
                  pl.BlockSpec(jmat.shape, lambda i: (0, 0))],
        out_specs=row,
        compiler_params=_params("parallel"),
        name="rwkv_post",
    )(y, bonus, g, gn_g.reshape(1, mix_b), gn_b.reshape(1, mix_b), jmat)


def _attn_body(q_ref, kv_ref, o_ref, *, heads):
    d = q_ref.shape[-1]
    hd = d // heads
    scale = hd ** -0.5
    for h in range(heads):
        q = q_ref[0, :, h * hd:(h + 1) * hd]
        k = kv_ref[0, :, h * hd:(h + 1) * hd]
        v = kv_ref[0, :, d + h * hd:d + (h + 1) * hd]
        s = lax.dot_general(q, k, NT, preferred_element_type=F32) * scale
        e = jnp.exp(s - jnp.max(s, axis=-1, keepdims=True))
        p = e / jnp.sum(e, axis=-1, keepdims=True)
        o_ref[0, :, h * hd:(h + 1) * hd] = _dot(p.astype(BF16), v).astype(o_ref.dtype)


def _attention(q, kv):
    b, t, d = q.shape
    n_mem = kv.shape[1]
    tq = _pick(t, (512, 256, 128, 64, 32, 16))
    return pl.pallas_call(
        functools.partial(_attn_body, heads=X_HEADS),
        out_shape=jax.ShapeDtypeStruct((b, t, d), BF16),
        grid=(b, t // tq),
        in_specs=[pl.BlockSpec((1, tq, d), lambda bi, i: (bi, i, 0)),
                  pl.BlockSpec((1, n_mem, 2 * d), lambda bi, i: (bi, 0, 0))],
        out_specs=pl.BlockSpec((1, tq, d), lambda bi, i: (bi, i, 0)),
        compiler_params=_params("parallel", "parallel"),
        name="cross_attention",
    )(q, kv)


def _ffn_body(h_ref, hp_ref, hn_ref, wg_ref, wv_ref, cwg_ref, cwv_ref, cbg_ref, cbv_ref, wd_ref,
              o_ref, hx_ref):
    i = pl.program_id(1)
    f = pl.program_id(2)
    tm = h_ref.shape[1]
    halo = BF16_ROWS

    @pl.when(f == 0)
    def _():
        zero = jnp.zeros((halo, h_ref.shape[2]), BF16)
        hx_ref[0:halo, :] = jnp.where(i == 0, zero, hp_ref[0])
        hx_ref[halo:halo + tm, :] = h_ref[0]
        hx_ref[halo + tm:, :] = jnp.where(i == pl.num_programs(1) - 1, zero, hn_ref[0])
        o_ref[...] = jnp.zeros_like(o_ref)

    hx = hx_ref[...]
    rows = tm + 2 * halo

    def conv(w_ref, cw_ref, cb_ref):
        z = _dot(hx, w_ref[...])
        prev = pltpu.roll(z, 1, 0)[halo:halo + tm]
        nxt = pltpu.roll(z, rows - 1, 0)[halo:halo + tm]
        cw = cw_ref[...]
        return cw[0:1] * prev + cw[1:2] * z[halo:halo + tm] + cw[2:3] * nxt + cb_ref[...]

    gate = conv(wg_ref, cwg_ref, cbg_ref)
    val = conv(wv_ref, cwv_ref, cbv_ref)
    act = (jax.nn.silu(gate) * val).astype(BF16)
    o_ref[0] += _dot(act, wd_ref[...])


def _conv_ffn(h, w_up, conv_w, conv_b, w_down):
    b, t, d = h.shape
    d_ff = w_down.shape[0]
    tm = _pick(t, (1024, 512, 256, 128, 64, 32))
    tf = _pick(d_ff, (256, 128))
    nf = d_ff // tf
    hb = tm // BF16_ROWS
    nblk = t // BF16_ROWS
    cb = conv_b.reshape(1, 2 * d_ff)
    once = pl.Buffered(1)
    return pl.pallas_call(
        _ffn_body,
        out_shape=jax.ShapeDtypeStruct((b, t, d), F32),
        grid=(b, t // tm, nf),
        in_specs=[
            pl.BlockSpec((1, tm, d), lambda bi, i, f: (bi, i, 0), pipeline_mode=once),
            pl.BlockSpec((1, BF16_ROWS, d), lambda bi, i, f: (bi, jnp.maximum(i * hb - 1, 0), 0)),
            pl.BlockSpec((1, BF16_ROWS, d), lambda bi, i, f: (bi, jnp.minimum((i + 1) * hb, nblk - 1), 0)),
            pl.BlockSpec((d, tf), lambda bi, i, f: (0, f)),
            pl.BlockSpec((d, tf), lambda bi, i, f: (0, nf + f)),
            pl.BlockSpec((3, tf), lambda bi, i, f: (0, f)),
            pl.BlockSpec((3, tf), lambda bi, i, f: (0, nf + f)),
            pl.BlockSpec((1, tf), lambda bi, i, f: (0, f)),
            pl.BlockSpec((1, tf), lambda bi, i, f: (0, nf + f)),
            pl.BlockSpec((tf, d), lambda bi, i, f: (f, 0)),
        ],
        out_specs=pl.BlockSpec((1, tm, d), lambda bi, i, f: (bi, i, 0), pipeline_mode=once),
        scratch_shapes=[pltpu.VMEM((tm + 2 * BF16_ROWS, d), BF16)],
        compiler_params=_params("parallel", "parallel", "arbitrary"),
        name="conv_ffn",
    )(h, h, h, w_up, w_up, conv_w, conv_w, cb, cb, w_down)


def _trunk(x, mem, p):
    b, t, d = x.shape
    m = b * t
    n_mem = mem.shape[1]
    mix_a = p["ln_v_g"].shape[-1]
    mix_b = p["k_k"].shape[-1]
    n = p["r_k"].shape[-1]
    x2d = x.reshape(m, d)

    h1 = _rmsnorm([x2d], p["norm_mix"], BF16)
    za = _matmul([h1], [p["w_in_a"]], F32)
    zrkv = _matmul([h1], [p["w_in_b"]], F32)
    zl = _matmul([h1], [p["w_in_c"]], F32)

    ya = _gmlp(za, p["ln_v_g"], p["ln_v_b"], p["w_s"], p["b_s"])

    r, v, kk, g, bonus, lw, kd, a = _rwkv_prep(
        zrkv.reshape(b, t, 3 * mix_b), zl.reshape(b, t, -1), p["mu_shift"], p["w0"], p["w_up_decay"],
        p["a0"], p["w_up_iclr"], p["w_up_gate"], p["k_k"], p["k_a"], p["r_k"], p["jmat"])
    y = _wkv_scan(r, v, kk, lw, kd, a, n)
    yb = _rwkv_post(y.reshape(2, m, mix_b), bonus.reshape(m, mix_b), g.reshape(m, mix_b),
                    p["gn_g"], p["gn_b"], p["jmat"], n)

    x1 = _matmul([ya, yb], [p["w_out_a"], p["w_out_b"]], F32, residual=x2d)

    hq = _rmsnorm([x1], p["norm_x"], BF16)
    q = _matmul([hq], [p["w_q"]], BF16)
    memn = _rmsnorm([mem.reshape(b * n_mem, d)], p["norm_mem"], BF16)
    kv = _matmul([memn], [p["w_kv"]], BF16)
    o = _attention(q.reshape(b, t, d), kv.reshape(b, n_mem, 2 * d))
    x2 = _matmul([o.reshape(m, d)], [p["w_o"]], F32, residual=x1)

    hf = _rmsnorm([x2], p["norm_ffn"], BF16)
    ffn = _conv_ffn(hf.reshape(b, t, d), p["w_ffn_up"], p["conv_w"], p["conv_b"], p["w_ffn_down"])
    y = _rmsnorm([x2, ffn.reshape(m, d)], p["norm_out"], F32)
    return y.reshape(b, t, d)


def kernel(x_prompt, x_sample, mem_prompt, mem_sample, norm_mix, w_in, mu_shift, ln_v_g, ln_v_b, w_s, b_s, w0, w_up_decay, a0, w_up_iclr, w_up_gate, k_k, k_a, r_k, gn_g, gn_b, w_out, norm_x, norm_mem, w_q, w_kv, w_o, norm_ffn, w_ffn_up, conv_w, conv_b, w_ffn_down, norm_out):
    depth = w_in.shape[0]
    mix_a = ln_v_g.shape[-1]
    mix_b = k_k.shape[-1]
    n = r_k.shape[-1]
    head_id = jnp.arange(LANES) // n
    jmat = (head_id[:, None] == head_id[None, :]).astype(BF16)

    layers = []
    for l in range(depth):
        w_in_l = w_in[l].astype(BF16)
        w_out_l = w_out[l].astype(BF16)
        layers.append(dict(
            norm_mix=norm_mix[l], w_in_a=w_in_l[:, :2 * mix_a],
            w_in_b=w_in_l[:, 2 * mix_a:2 * mix_a + 3 * mix_b], w_in_c=w_in_l[:, 2 * mix_a + 3 * mix_b:],
            mu_shift=mu_shift[l], ln_v_g=ln_v_g[l], ln_v_b=ln_v_b[l], w_s=w_s[l], b_s=b_s[l],
            w0=w0[l], w_up_decay=w_up_decay[l], a0=a0[l], w_up_iclr=w_up_iclr[l], w_up_gate=w_up_gate[l],
            k_k=k_k[l], k_a=k_a[l], r_k=r_k[l], gn_g=gn_g[l], gn_b=gn_b[l],
            w_out_a=w_out_l[:mix_a], w_out_b=w_out_l[mix_a:], norm_x=norm_x[l], norm_mem=norm_mem[l],
            w_q=w_q[l].astype(BF16), w_kv=w_kv[l].astype(BF16), w_o=w_o[l].astype(BF16),
            norm_ffn=norm_ffn[l], w_ffn_up=w_ffn_up[l].astype(BF16), conv_w=conv_w[l], conv_b=conv_b[l],
            w_ffn_down=w_ffn_down[l].astype(BF16), jmat=jmat))

    (p,) = layers
    p = dict(p, norm_out=norm_out)
    return _trunk(x_prompt, mem_prompt, p), _trunk(x_sample, mem_sample, p)
```

```python
import functools

import jax
import jax.numpy as jnp
from jax import lax
from jax.experimental import pallas as pl
from jax.experimental.pallas import tpu as pltpu

RMS_EPS = 1e-6
LN_EPS = 1e-5
GN_EPS = 64e-5
L2_EPS = 1e-12
X_HEADS = 4

LANES = 128
BF16_ROWS = 16
SCAN_CHUNK = 64
SCAN_HEADS = 32
SCAN_PASSES = 1
INV_BASE = 8
VMEM_LIMIT = 56 * 1024 * 1024

F32 = jnp.float32
BF16 = jnp.bfloat16


def _params(*sem, flags=None):
    return pltpu.CompilerParams(dimension_semantics=sem, vmem_limit_bytes=VMEM_LIMIT, flags=flags)


def _pick(n, candidates):
    for c in candidates:
        if n % c == 0:
            return c
    raise ValueError(f"no tile for {n} in {candidates}")


def _dot(a, b):
    return jnp.dot(a, b, preferred_element_type=F32)


def _split(x):
    hi = x.astype(BF16)
    lo = (x - hi.astype(F32)).astype(BF16)
    return hi, lo


NN = (((1,), (0,)), ((), ()))
NT = (((1,), (1,)), ((), ()))
TN = (((0,), (0,)), ((), ()))


def _rmsnorm_body(*refs, n_in):
    g_ref, o_ref = refs[n_in], refs[n_in + 1]
    x = refs[0][...]
    for r in refs[1:n_in]:
        x = x + r[...]
    ms = jnp.mean(x * x, axis=-1, keepdims=True)
    o_ref[...] = (x * lax.rsqrt(ms + RMS_EPS) * g_ref[...]).astype(o_ref.dtype)


def _rmsnorm(xs, g, out_dtype):
    m, d = xs[0].shape
    tm = _pick(m, (256, 128, 64, 32, 16, 8))
    row = pl.BlockSpec((tm, d), lambda i: (i, 0))
    return pl.pallas_call(
        functools.partial(_rmsnorm_body, n_in=len(xs)),
        out_shape=jax.ShapeDtypeStruct((m, d), out_dtype),
        grid=(m // tm,),
        in_specs=[row] * len(xs) + [pl.BlockSpec((1, d), lambda i: (0, 0))],
        out_specs=row,
        compiler_params=_params("parallel"),
        name="rmsnorm",
    )(*xs, g.reshape(1, d))


def _mm_body(*refs, n_lhs, has_res):
    a_refs, w_refs = refs[:n_lhs], refs[n_lhs:2 * n_lhs]
    o_ref = refs[-1]
    acc = _dot(a_refs[0][...], w_refs[0][...])
    for a, w in zip(a_refs[1:], w_refs[1:]):
        acc = acc + _dot(a[...], w[...])
    if has_res:
        acc = acc + refs[2 * n_lhs][...]
    o_ref[...] = acc.astype(o_ref.dtype)


def _matmul(lhs, ws, out_dtype, residual=None):
    m = lhs[0].shape[0]
    n = ws[0].shape[1]
    tm = _pick(m, (1024, 512, 256, 128, 64, 32, 16, 8))
    tn = _pick(n, (1024, 768, 512, 256, 128))
    in_specs = [pl.BlockSpec((tm, a.shape[1]), lambda i, j: (i, 0)) for a in lhs]
    in_specs += [pl.BlockSpec((w.shape[0], tn), lambda i, j: (0, j)) for w in ws]
    args = list(lhs) + list(ws)
    if residual is not None:
        in_specs.append(pl.BlockSpec((tm, tn), lambda i, j: (i, j)))
        args.append(residual)
    return pl.pallas_call(
        functools.partial(_mm_body, n_lhs=len(lhs), has_res=residual is not None),
        out_shape=jax.ShapeDtypeStruct((m, n), out_dtype),
        grid=(m // tm, n // tn),
        in_specs=in_specs,
        out_specs=pl.BlockSpec((tm, tn), lambda i, j: (i, j)),
        compiler_params=_params("parallel", "parallel"),
        name="matmul",
    )(*args)


def _gmlp_body(zu_ref, zv_ref, g_ref, b_ref, ws_ref, bs_ref, o_ref, *, chunk):
    u = jax.nn.gelu(zu_ref[...])
    v = jax.nn.gelu(zv_ref[...])
    mu = jnp.mean(v, axis=-1, keepdims=True)
    var = jnp.mean(jnp.square(v - mu), axis=-1, keepdims=True)
    vn = ((v - mu) * lax.rsqrt(var + LN_EPS) * g_ref[...] + b_ref[...]).astype(BF16)
    rows, width = u.shape
    heads = ws_ref.shape[0]
    hd = width // heads
    bs = bs_ref[...]
    for c in range(rows // chunk):
        rs = slice(c * chunk, (c + 1) * chunk)
        mixed = jnp.concatenate(
            [_dot(ws_ref[h], vn[rs, h * hd:(h + 1) * hd]) for h in range(heads)], axis=1)
        o_ref[rs, :] = (u[rs, :] * (mixed + bs)).astype(o_ref.dtype)


def _gmlp(za, ln_g, ln_b, w_s, b_s):
    m, two_a = za.shape
    mix_a = two_a // 2
    heads, chunk, _ = w_s.shape
    rows = _pick(m, (2 * chunk, chunk))
    bs_full = jnp.repeat(b_s.T, mix_a // heads, axis=1)
    vec = pl.BlockSpec((1, mix_a), lambda i: (0, 0))
    return pl.pallas_call(
        functools.partial(_gmlp_body, chunk=chunk),
        out_shape=jax.ShapeDtypeStruct((m, mix_a), BF16),
        grid=(m // rows,),
        in_specs=[pl.BlockSpec((rows, mix_a), lambda i: (i, 0)),
                  pl.BlockSpec((rows, mix_a), lambda i: (i, 1)),
                  vec, vec,
                  pl.BlockSpec((heads, chunk, chunk), lambda i: (0, 0, 0)),
                  pl.BlockSpec((chunk, mix_a), lambda i: (0, 0))],
        out_specs=pl.BlockSpec((rows, mix_a), lambda i: (i, 0)),
        compiler_params=_params("parallel"),
        name="gmlp",
    )(za, za, ln_g.reshape(1, mix_a), ln_b.reshape(1, mix_a), w_s.astype(BF16), bs_full)


def _headsum(x, j_ref):
    tm, w = x.shape
    nb = w // LANES
    xs = jnp.concatenate([x[:, i * LANES:(i + 1) * LANES] for i in range(nb)], axis=0)
    hi, lo = _split(xs)
    j = j_ref[...]
    s = _dot(hi, j) + _dot(lo, j)
    return jnp.concatenate([s[i * tm:(i + 1) * tm] for i in range(nb)], axis=1)


def _tshift(z, zp, zn, mu, first, last):
    tm = z.shape[0]
    rows = lax.broadcasted_iota(jnp.int32, (tm, 1), 0)
    prev_row = jnp.where(first, 0.0, zp[7:8, :])
    next_row = jnp.where(last, 0.0, zn[0:1, :])
    prev = jnp.where(rows == 0, prev_row, pltpu.roll(z, 1, 0))
    nxt = jnp.where(rows == tm - 1, next_row, pltpu.roll(z, tm - 1, 0))
    return z + mu * (0.5 * (prev + nxt) - z)


def _prep_body(zr, zrp, zrn, zk, zkp, zkn, zv, zvp, zvn, zl, zlp, zln,
               mur, muk, muv, mul, w0_ref, wd_ref, a0_ref, wi_ref, wg_ref,
               kk_ref, ka_ref, rk_ref, j_ref,
               r_out, v_out, kk_out, g_out, bonus_out, lw_out, kd_out, a_out,
               *, lora_w, lora_a):
    i = pl.program_id(1)
    first = i == 0
    last = i == pl.num_programs(1) - 1
    r = _tshift(zr[0], zrp[0], zrn[0], mur[...], first, last)
    k = _tshift(zk[0], zkp[0], zkn[0], muk[...], first, last)
    v = _tshift(zv[0], zvp[0], zvn[0], muv[...], first, last)
    lo = _tshift(zl[0], zlp[0], zln[0], mul[...], first, last)
    xw = lo[:, :lora_w]
    xa = lo[:, lora_w:lora_w + lora_a]
    xg = lo[:, lora_w + lora_a:]

    g_out[0] = _dot(jax.nn.sigmoid(xg).astype(BF16), wg_ref[...])
    kkr = k * kk_ref[...]
    ss = _headsum(kkr * kkr, j_ref)
    kk = kkr * lax.rsqrt(jnp.maximum(ss, L2_EPS))
    tw = jnp.tanh(xw).astype(BF16)
    xab = xa.astype(BF16)
    rk = None
    for d in range(2):
        logw = -jax.nn.softplus(-(w0_ref[d:d + 1, :] + _dot(tw, wd_ref[d]))) - 0.5
        lw_out[d, 0] = -jnp.exp(logw)
        a = jax.nn.sigmoid(a0_ref[d:d + 1, :] + _dot(xab, wi_ref[d]))
        kd = k * (1.0 + (a - 1.0) * ka_ref[...])
        a_out[d, 0] = a
        kd_out[d, 0] = kd
        s = _headsum(r * kd * rk_ref[...], j_ref)
        rk = s if rk is None else rk + s
    r_out[0] = r
    v_out[0] = v
    kk_out[0] = kk
    bonus_out[0] = rk * v


def _rwkv_prep(zrkv, zl, mu, w0, w_up_decay, a0, w_up_iclr, w_up_gate, k_k, k_a, r_k, jmat):
    b, t, three_b = zrkv.shape
    mix_b = three_b // 3
    nl = zl.shape[-1]
    lora_w, lora_a = w_up_decay.shape[1], w_up_iclr.shape[1]
    tm = _pick(t, (128, 64, 32, 16, 8))
    hb = tm // 8
    nblk8 = t // 8

    def main(w, col):
        return pl.BlockSpec((1, tm, w), lambda bi, i: (bi, i, col))

    def prev(w, col):
        return pl.BlockSpec((1, 8, w), lambda bi, i: (bi, jnp.maximum(i * hb - 1, 0), col))

    def nxt(w, col):
        return pl.BlockSpec((1, 8, w), lambda bi, i: (bi, jnp.minimum((i + 1) * hb, nblk8 - 1), col))

    def const(shape):
        return pl.BlockSpec(shape, lambda bi, i: (0,) * len(shape))

    in_specs, args = [], []
    for col in range(3):
        in_specs += [main(mix_b, col), prev(mix_b, col), nxt(mix_b, col)]
        args += [zrkv] * 3
    in_specs += [main(nl, 0), prev(nl, 0), nxt(nl, 0)]
    args += [zl] * 3
    mu2 = mu.reshape(1, -1)
    in_specs += [pl.BlockSpec((1, mix_b), lambda bi, i, c=c: (0, c)) for c in range(3)]
    args += [mu2[:, :three_b]] * 3
    in_specs.append(const((1, nl)))
    args.append(mu2[:, three_b:])
    consts = [w0, w_up_decay.astype(BF16), a0, w_up_iclr.astype(BF16), w_up_gate.astype(BF16),
              k_k.reshape(1, mix_b), k_a.reshape(1, mix_b), r_k.reshape(1, mix_b), jmat]
    in_specs += [const(c.shape) for c in consts]
    args += consts

    shared = jax.ShapeDtypeStruct((b, t, mix_b), F32)
    perdir = jax.ShapeDtypeStruct((2, b, t, mix_b), F32)
    o_shared = pl.BlockSpec((1, tm, mix_b), lambda bi, i: (bi, i, 0))
    o_perdir = pl.BlockSpec((2, 1, tm, mix_b), lambda bi, i: (0, bi, i, 0))
    return pl.pallas_call(
        functools.partial(_prep_body, lora_w=lora_w, lora_a=lora_a),
        out_shape=[shared] * 5 + [perdir] * 3,
        grid=(b, t // tm),
        in_specs=in_specs,
        out_specs=[o_shared] * 5 + [o_perdir] * 3,
        compiler_params=_params("parallel", "parallel"),
        name="rwkv_prep",
    )(*args)


def _mm(a, b, dims, passes):
    dg = lambda x, y: lax.dot_general(x, y, dims, preferred_element_type=F32)
    if passes == 1:
        return dg(a.astype(BF16), b.astype(BF16))
    ah, al = _split(a)
    bh, bl = _split(b)
    return dg(ah, bh) + (dg(ah, bl) + dg(al, bh))


def _halves(x, size):
    blocks = x.shape[0] // (2 * size)
    lo = jnp.concatenate([x[2 * j * size:(2 * j + 1) * size] for j in range(blocks)], axis=0)
    hi = jnp.concatenate([x[(2 * j + 1) * size:(2 * j + 2) * size] for j in range(blocks)], axis=0)
    return lo, hi


def _interleave(lo, hi, size):
    pieces = []
    for j in range(lo.shape[0] // size):
        pieces += [lo[j * size:(j + 1) * size], hi[j * size:(j + 1) * size]]
    return jnp.concatenate(pieces, axis=0)


def _tri_inverse(ls, limit, reverse, passes):
    rows = ls[0].shape[0]
    row = lax.broadcasted_iota(jnp.int32, (rows, rows), 0)
    col = lax.broadcasted_iota(jnp.int32, (rows, rows), 1)
    eye = (row == col).astype(F32)
    sh = INV_BASE.bit_length() - 1
    base = (row >> sh) == (col >> sh)
    ps = [jnp.where(base, l, 0.0) for l in ls]
    ts = [eye + p for p in ps]
    size = 2
    while size < INV_BASE:
        ps = [_mm(p, p, NN, passes) for p in ps]
        ts = [_mm(t, eye + p, NN, passes) for t, p in zip(ts, ps)]
        size *= 2
    size = INV_BASE
    hrow = lax.broadcasted_iota(jnp.int32, (rows // 2, rows), 0)
    hcol = lax.broadcasted_iota(jnp.int32, (rows // 2, rows), 1)
    zero = jnp.zeros((rows // 2, rows), F32)
    while size < limit:
        sh = size.bit_length() - 1
        partner = ((hcol >> (sh + 1)) == (hrow >> sh)) & (((hcol >> sh) & 1) == (1 if reverse else 0))
        act = 0 if reverse else 1
        l_act = [_halves(l, size)[act] for l in ls]
        t_halves = [_halves(t, size) for t in ts]
        xs = [_mm(jnp.where(partner, la, 0.0), t, NN, passes) for la, t in zip(l_act, ts)]
        xs = [_interleave(x, zero, size) if reverse else _interleave(zero, x, size) for x in xs]
        upd = [th[act] + _mm(th[act], x, NN, passes) for th, x in zip(t_halves, xs)]
        ts = [_interleave(u, th[1], size) if reverse else _interleave(th[0], u, size)
              for th, u in zip(t_halves, upd)]
        size *= 2
    return ts


def _scan_body(r_ref, v_ref, kk_ref, lw_ref, kd_ref, a_ref, y_ref, s_ref, *, n, reverse, passes):
    c = pl.program_id(2)
    cs = r_ref.shape[1]

    @pl.when(c == 0)
    def _():
        s_ref[...] = jnp.zeros_like(s_ref)

    sgn = -1 if reverse else 1
    tiles = r_ref.shape[2] // LANES
    hp = LANES // n
    rows = hp * cs

    lw = lw_ref[0, 0]
    rc = lax.broadcasted_iota(jnp.int32, (cs, cs), 0)
    cc = lax.broadcasted_iota(jnp.int32, (cs, cs), 1)
    tri = ((rc - cc) * sgn >= 0).astype(BF16)
    l_hi = lw.astype(BF16)
    l_mid = (lw - l_hi.astype(F32)).astype(BF16)
    l_lo = (lw - l_hi.astype(F32) - l_mid.astype(F32)).astype(BF16)
    cum = _dot(tri, l_hi) + (_dot(tri, l_mid) + _dot(tri, l_lo))
    tot = cum[0:1, :] if reverse else cum[cs - 1:cs, :]
    e_tot = jnp.exp(tot)

    row = lax.broadcasted_iota(jnp.int32, (rows, rows), 0)
    col = lax.broadcasted_iota(jnp.int32, (rows, rows), 1)
    order = ((row & (cs - 1)) - (col & (cs - 1))) * sgn
    strict = order > 0
    incl = order >= 0
    eye_l = (lax.broadcasted_iota(jnp.int32, (LANES, LANES), 0)
             == lax.broadcasted_iota(jnp.int32, (LANES, LANES), 1))
    lane_head = lax.broadcasted_iota(jnp.int32, (cs, LANES), 1) // n
    head_masks = [lane_head == h for h in range(hp)]
    zero_t = jnp.zeros((rows, LANES), BF16)

    def pack(x):
        return jnp.concatenate([jnp.where(mk, x, jnp.zeros_like(x)) for mk in head_masks], axis=0)

    tl = range(tiles)
    mm = functools.partial(_mm, passes=passes)
    cat = jnp.concatenate
    at, rt, bh, kh, vv, aa = [], [], [], [], [], []
    for p in tl:
        sl = slice(p * LANES, (p + 1) * LANES)
        cum_p, lw_p, tot_p = cum[:, sl], lw[:, sl], tot[:, sl]
        kk_p = kk_ref[0, :, sl]
        kd_p = kd_ref[0, 0, :, sl]
        b_p = kk_p * a_ref[0, 0, :, sl]
        e_neg = jnp.exp(-cum_p)
        e_hat = jnp.exp(tot_p - cum_p)
        at.append(pack(-kk_p * jnp.exp(cum_p - lw_p)))
        rt.append(pack(r_ref[0, :, sl] * jnp.exp(cum_p)))
        bh.append(pack((b_p * e_hat).astype(BF16)))
        kh.append(pack((kd_p * e_hat).astype(BF16)))
        vv.append(pack(v_ref[0, :, sl].astype(BF16)))
        bk = cat([pack((b_p * e_neg).astype(BF16)), pack((kd_p * e_neg).astype(BF16))], axis=0)
        aa.append(mm(cat([at[p], rt[p]], axis=0), bk, NT))
    aab = [jnp.where(strict, x[:rows, :rows], 0.0) for x in aa]
    aak = [jnp.where(strict, x[:rows, rows:], 0.0) for x in aa]
    aq = [cat([jnp.where(incl, x[rows:, :rows], 0.0), jnp.where(incl, x[rows:, rows:], 0.0)], axis=1)
          for x in aa]
    tinv = _tri_inverse(aab, cs, reverse, passes)
    akv = [mm(aak[p], vv[p], NN) for p in tl]
    wu = [mm(tinv[p], cat([at[p], akv[p]], axis=1), NN) for p in tl]
    rhs = [cat([wu[p].astype(BF16), cat([zero_t, vv[p]], axis=1)], axis=0) for p in tl]
    qy = [mm(aq[p], rhs[p], NN) for p in tl]
    mn = [mm(cat([bh[p], kh[p]], axis=0), rhs[p], TN) for p in tl]
    lhs = []
    for p in tl:
        m_mat = jnp.where(eye_l, e_tot[:, p * LANES:(p + 1) * LANES], 0.0) + mn[p][:, :LANES]
        lhs.append(cat([m_mat, rt[p] + qy[p][:, :LANES]], axis=0))
    ms = [mm(lhs[p], s_ref[p], NN) for p in tl]
    outs = []
    for p in tl:
        s_ref[p] = ms[p][:LANES] + mn[p][:, LANES:]
        ybd = ms[p][LANES:] + qy[p][:, LANES:]
        y = ybd[:cs]
        for h in range(1, hp):
            y = y + ybd[h * cs:(h + 1) * cs]
        outs.append(y)
    y_ref[0] = cat(outs, axis=1)


def _wkv_scan(r, v, kk, lw, kd, a, n, reverse):
    b, t, mix_b = r.shape
    heads = mix_b // n
    hg = SCAN_HEADS if heads % SCAN_HEADS == 0 else heads
    cs = SCAN_CHUNK
    nc = t // cs
    w = hg * n
    di = 1 if reverse else 0
    tchunk = (lambda ci: nc - 1 - ci) if reverse else (lambda ci: ci)
    shared = pl.BlockSpec((1, cs, w), lambda bi, hi, ci: (bi, tchunk(ci), hi))
    perdir = pl.BlockSpec((1, 1, cs, w), lambda bi, hi, ci: (di, bi, tchunk(ci), hi))
    return pl.pallas_call(
        functools.partial(_scan_body, n=n, reverse=reverse, passes=SCAN_PASSES),
        out_shape=jax.ShapeDtypeStruct((b, t, mix_b), F32),
        grid=(b, heads // hg, nc),
        in_specs=[shared] * 3 + [perdir] * 3,
        out_specs=shared,
        scratch_shapes=[pltpu.VMEM((w // LANES, LANES, LANES), F32)],
        compiler_params=_params("parallel", "parallel", "arbitrary"),
        name="wkv_scan",
    )(r, v, kk, lw, kd, a)


def _post_body(yf_ref, yb_ref, bonus_ref, g_ref, gg_ref, gb_ref, j_ref, o_ref, *, n):
    y = yf_ref[...] + yb_ref[...]
    mean = _headsum(y, j_ref) * (1.0 / n)
    yc = y - mean
    var = _headsum(yc * yc, j_ref) * (1.0 / n)
    yn = yc * lax.rsqrt(var + GN_EPS) * gg_ref[...] + gb_ref[...]
    o_ref[...] = ((yn + bonus_ref[...]) * g_ref[...]).astype(o_ref.dtype)


def _rwkv_post(y_fwd, y_bwd, bonus, g, gn_g, gn_b, jmat, n):
    m, mix_b = y_fwd.shape
    tm = _pick(m, (256, 128, 64, 32, 16, 8))
    row = pl.BlockSpec((tm, mix_b), lambda i: (i, 0))
    vec = pl.BlockSpec((1, mix_b), lambda i: (0, 0))
    return pl.pallas_call(
        functools.partial(_post_body, n=n),
        out_shape=jax.ShapeDtypeStruct((m, mix_b), BF16),
        grid=(m // tm,),
        in_specs=[row, row, row, row, vec, vec, pl.BlockSpec(jmat.shape, lambda i: (0, 0))],
        out_specs=row,
        compiler_params=_params("parallel"),
        name="rwkv_post",
    )(y_fwd, y_bwd, bonus, g, gn_g.reshape(1, mix_b), gn_b.reshape(1, mix_b), jmat)


def _attn_body(q_ref, kv_ref, o_ref, *, heads):
    d = q_ref.shape[-1]
    hd = d // heads
    scale = hd ** -0.5
    for h in range(heads):
        q = q_ref[0, :, h * hd:(h + 1) * hd]
        k = kv_ref[0, :, h * hd:(h + 1) * hd]
        v = kv_ref[0, :, d + h * hd:d + (h + 1) * hd]
        s = lax.dot_general(q, k, NT, preferred_element_type=F32) * scale
        e = jnp.exp(s - jnp.max(s, axis=-1, keepdims=True))
        p = e / jnp.sum(e, axis=-1, keepdims=True)
        o_ref[0, :, h * hd:(h + 1) * hd] = _dot(p.astype(BF16), v).astype(o_ref.dtype)


def _attention(q, kv):
    b, t, d = q.shape
    n_mem = kv.shape[1]
    tq = _pick(t, (512, 256, 128, 64, 32, 16))
    return pl.pallas_call(
        functools.partial(_attn_body, heads=X_HEADS),
        out_shape=jax.ShapeDtypeStruct((b, t, d), BF16),
        grid=(b, t // tq),
        in_specs=[pl.BlockSpec((1, tq, d), lambda bi, i: (bi, i, 0)),
                  pl.BlockSpec((1, n_mem, 2 * d), lambda bi, i: (bi, 0, 0))],
        out_specs=pl.BlockSpec((1, tq, d), lambda bi, i: (bi, i, 0)),
        compiler_params=_params("parallel", "parallel"),
        name="cross_attention",
    )(q, kv)


def _ffn_body(h_ref, hp_ref, hn_ref, wg_ref, wv_ref, cwg_ref, cwv_ref, cbg_ref, cbv_ref, wd_ref,
              o_ref, hx_ref):
    i = pl.program_id(1)
    f = pl.program_id(2)
    tm = h_ref.shape[1]
    halo = BF16_ROWS

    @pl.when(f == 0)
    def _():
        zero = jnp.zeros((halo, h_ref.shape[2]), BF16)
        hx_ref[0:halo, :] = jnp.where(i == 0, zero, hp_ref[0])
        hx_ref[halo:halo + tm, :] = h_ref[0]
        hx_ref[halo + tm:, :] = jnp.where(i == pl.num_programs(1) - 1, zero, hn_ref[0])
        o_ref[...] = jnp.zeros_like(o_ref)

    hx = hx_ref[...]
    rows = tm + 2 * halo

    def conv(w_ref, cw_ref, cb_ref):
        z = _dot(hx, w_ref[...])
        prev = pltpu.roll(z, 1, 0)[halo:halo + tm]
        nxt = pltpu.roll(z, rows - 1, 0)[halo:halo + tm]
        cw = cw_ref[...]
        return cw[0:1] * prev + cw[1:2] * z[halo:halo + tm] + cw[2:3] * nxt + cb_ref[...]

    gate = conv(wg_ref, cwg_ref, cbg_ref)
    val = conv(wv_ref, cwv_ref, cbv_ref)
    act = (jax.nn.silu(gate) * val).astype(BF16)
    o_ref[0] += _dot(act, wd_ref[...])


def _conv_ffn(h, w_up, conv_w, conv_b, w_down):
    b, t, d = h.shape
    d_ff = w_down.shape[0]
    tm = _pick(t, (1024, 512, 256, 128, 64, 32))
    tf = _pick(d_ff, (256, 128))
    nf = d_ff // tf
    hb = tm // BF16_ROWS
    nblk = t // BF16_ROWS
    cb = conv_b.reshape(1, 2 * d_ff)
    once = pl.Buffered(1)
    return pl.pallas_call(
        _ffn_body,
        out_shape=jax.ShapeDtypeStruct((b, t, d), F32),
        grid=(b, t // tm, nf),
        in_specs=[
            pl.BlockSpec((1, tm, d), lambda bi, i, f: (bi, i, 0), pipeline_mode=once),
            pl.BlockSpec((1, BF16_ROWS, d), lambda bi, i, f: (bi, jnp.maximum(i * hb - 1, 0), 0)),
            pl.BlockSpec((1, BF16_ROWS, d), lambda bi, i, f: (bi, jnp.minimum((i + 1) * hb, nblk - 1), 0)),
            pl.BlockSpec((d, tf), lambda bi, i, f: (0, f)),
            pl.BlockSpec((d, tf), lambda bi, i, f: (0, nf + f)),
            pl.BlockSpec((3, tf), lambda bi, i, f: (0, f)),
            pl.BlockSpec((3, tf), lambda bi, i, f: (0, nf + f)),
            pl.BlockSpec((1, tf), lambda bi, i, f: (0, f)),
            pl.BlockSpec((1, tf), lambda bi, i, f: (0, nf + f)),
            pl.BlockSpec((tf, d), lambda bi, i, f: (f, 0)),
        ],
        out_specs=pl.BlockSpec((1, tm, d), lambda bi, i, f: (bi, i, 0), pipeline_mode=once),
        scratch_shapes=[pltpu.VMEM((tm + 2 * BF16_ROWS, d), BF16)],
        compiler_params=_params("parallel", "parallel", "arbitrary"),
        name="conv_ffn",
    )(h, h, h, w_up, w_up, conv_w, conv_w, cb, cb, w_down)


def _trunk(x, mem, p):
    b, t, d = x.shape
    m = b * t
    n_mem = mem.shape[1]
    mix_a = p["ln_v_g"].shape[-1]
    mix_b = p["k_k"].shape[-1]
    n = p["r_k"].shape[-1]
    x2d = x.reshape(m, d)

    h1 = _rmsnorm([x2d], p["norm_mix"], BF16)
    za = _matmul([h1], [p["w_in_a"]], F32)
    zrkv = _matmul([h1], [p["w_in_b"]], F32)
    zl = _matmul([h1], [p["w_in_c"]], F32)

    ya = _gmlp(za, p["ln_v_g"], p["ln_v_b"], p["w_s"], p["b_s"])

    r, v, kk, g, bonus, lw, kd, a = _rwkv_prep(
        zrkv.reshape(b, t, 3 * mix_b), zl.reshape(b, t, -1), p["mu_shift"], p["w0"], p["w_up_decay"],
        p["a0"], p["w_up_iclr"], p["w_up_gate"], p["k_k"], p["k_a"], p["r_k"], p["jmat"])
    y_fwd = _wkv_scan(r, v, kk, lw, kd, a, n, reverse=False)
    y_bwd = _wkv_scan(r, v, kk, lw, kd, a, n, reverse=True)
    yb = _rwkv_post(y_fwd.reshape(m, mix_b), y_bwd.reshape(m, mix_b), bonus.reshape(m, mix_b),
                    g.reshape(m, mix_b), p["gn_g"], p["gn_b"], p["jmat"], n)

    x1 = _matmul([ya, yb], [p["w_out_a"], p["w_out_b"]], F32, residual=x2d)

    hq = _rmsnorm([x1], p["norm_x"], BF16)
    q = _matmul([hq], [p["w_q"]], BF16)
    memn = _rmsnorm([mem.reshape(b * n_mem, d)], p["norm_mem"], BF16)
    kv = _matmul([memn], [p["w_kv"]], BF16)
    o = _attention(q.reshape(b, t, d), kv.reshape(b, n_mem, 2 * d))
    x2 = _matmul([o.reshape(m, d)], [p["w_o"]], F32, residual=x1)

    hf = _rmsnorm([x2], p["norm_ffn"], BF16)
    ffn = _conv_ffn(hf.reshape(b, t, d), p["w_ffn_up"], p["conv_w"], p["conv_b"], p["w_ffn_down"])
    y = _rmsnorm([x2, ffn.reshape(m, d)], p["norm_out"], F32)
    return y.reshape(b, t, d)


def kernel(x_prompt, x_sample, mem_prompt, mem_sample, norm_mix, w_in, mu_shift, ln_v_g, ln_v_b, w_s, b_s, w0, w_up_decay, a0, w_up_iclr, w_up_gate, k_k, k_a, r_k, gn_g, gn_b, w_out, norm_x, norm_mem, w_q, w_kv, w_o, norm_ffn, w_ffn_up, conv_w, conv_b, w_ffn_down, norm_out):
    depth = w_in.shape[0]
    mix_a = ln_v_g.shape[-1]
    mix_b = k_k.shape[-1]
    n = r_k.shape[-1]
    head_id = jnp.arange(LANES) // n
    jmat = (head_id[:, None] == head_id[None, :]).astype(BF16)

    layers = []
    for l in range(depth):
        w_in_l = w_in[l].astype(BF16)
        w_out_l = w_out[l].astype(BF16)
        layers.append(dict(
            norm_mix=norm_mix[l], w_in_a=w_in_l[:, :2 * mix_a],
            w_in_b=w_in_l[:, 2 * mix_a:2 * mix_a + 3 * mix_b], w_in_c=w_in_l[:, 2 * mix_a + 3 * mix_b:],
            mu_shift=mu_shift[l], ln_v_g=ln_v_g[l], ln_v_b=ln_v_b[l], w_s=w_s[l], b_s=b_s[l],
            w0=w0[l], w_up_decay=w_up_decay[l], a0=a0[l], w_up_iclr=w_up_iclr[l], w_up_gate=w_up_gate[l],
            k_k=k_k[l], k_a=k_a[l], r_k=r_k[l], gn_g=gn_g[l], gn_b=gn_b[l],
            w_out_a=w_out_l[:mix_a], w_out_b=w_out_l[mix_a:], norm_x=norm_x[l], norm_mem=norm_mem[l],
            w_q=w_q[l].astype(BF16), w_kv=w_kv[l].astype(BF16), w_o=w_o[l].astype(BF16),
            norm_ffn=norm_ffn[l], w_ffn_up=w_ffn_up[l].astype(BF16), conv_w=conv_w[l], conv_b=conv_b[l],
            w_ffn_down=w_ffn_down[l].astype(BF16), jmat=jmat))

    (p,) = layers
    p = dict(p, norm_out=norm_out)
    return _trunk(x_prompt, mem_prompt, p), _trunk(x_sample, mem_sample, p)
```

```python
import functools

import jax
import jax.numpy as jnp
from jax import lax
from jax.experimental import pallas as pl
from jax.experimental.pallas import tpu as pltpu

RMS_EPS = 1e-6
LN_EPS = 1e-5
GN_EPS = 64e-5
L2_EPS = 1e-12
X_HEADS = 4

LANES = 128
BF16_ROWS = 16
SCAN_CHUNK = 64
SCAN_HEADS = 32
SCAN_PASSES = 1
INV_BASE = 8
VMEM_LIMIT = 56 * 1024 * 1024

F32 = jnp.float32
BF16 = jnp.bfloat16


def _params(*sem, flags=None):
    return pltpu.CompilerParams(dimension_semantics=sem, vmem_limit_bytes=VMEM_LIMIT, flags=flags)


def _pick(n, candidates):
    for c in candidates:
        if n % c == 0:
            return c
    raise ValueError(f"no tile for {n} in {candidates}")


def _dot(a, b):
    return jnp.dot(a, b, preferred_element_type=F32)


def _split(x):
    hi = x.astype(BF16)
    lo = (x - hi.astype(F32)).astype(BF16)
    return hi, lo


NN = (((1,), (0,)), ((), ()))
NT = (((1,), (1,)), ((), ()))
TN = (((0,), (0,)), ((), ()))


def _rmsnorm_body(*refs, n_in):
    g_ref, o_ref = refs[n_in], refs[n_in + 1]
    x = refs[0][...]
    for r in refs[1:n_in]:
        x = x + r[...]
    ms = jnp.mean(x * x, axis=-1, keepdims=True)
    o_ref[...] = (x * lax.rsqrt(ms + RMS_EPS) * g_ref[...]).astype(o_ref.dtype)


def _rmsnorm(xs, g, out_dtype):
    m, d = xs[0].shape
    tm = _pick(m, (256, 128, 64, 32, 16, 8))
    row = pl.BlockSpec((tm, d), lambda i: (i, 0))
    return pl.pallas_call(
        functools.partial(_rmsnorm_body, n_in=len(xs)),
        out_shape=jax.ShapeDtypeStruct((m, d), out_dtype),
        grid=(m // tm,),
        in_specs=[row] * len(xs) + [pl.BlockSpec((1, d), lambda i: (0, 0))],
        out_specs=row,
        compiler_params=_params("parallel"),
        name="rmsnorm",
    )(*xs, g.reshape(1, d))


def _mm_body(*refs, n_lhs, has_res):
    a_refs, w_refs = refs[:n_lhs], refs[n_lhs:2 * n_lhs]
    o_ref = refs[-1]
    acc = _dot(a_refs[0][...], w_refs[0][...])
    for a, w in zip(a_refs[1:], w_refs[1:]):
        acc = acc + _dot(a[...], w[...])
    if has_res:
        acc = acc + refs[2 * n_lhs][...]
    o_ref[...] = acc.astype(o_ref.dtype)


def _matmul(lhs, w, out_dtype, residual=None, col0=0, n=None):
    m = lhs[0].shape[0]
    n = w.shape[1] if n is None else n
    tm = _pick(m, (1024, 512, 256, 128, 64, 32, 16, 8))
    tn = next(c for c in (1024, 768, 512, 256, 128) if n % c == 0 and col0 % c == 0)
    cb = col0 // tn
    in_specs = [pl.BlockSpec((tm, a.shape[1]), lambda i, j: (i, 0)) for a in lhs]
    in_specs += [pl.BlockSpec((a.shape[1], tn), lambda i, j, rb=rb: (rb, cb + j)) for rb, a in enumerate(lhs)]
    assert all(a.shape[1] == lhs[0].shape[1] for a in lhs) and len(lhs) * lhs[0].shape[1] == w.shape[0]
    args = list(lhs) + [w] * len(lhs)
    if residual is not None:
        in_specs.append(pl.BlockSpec((tm, tn), lambda i, j: (i, j)))
        args.append(residual)
    return pl.pallas_call(
        functools.partial(_mm_body, n_lhs=len(lhs), has_res=residual is not None),
        out_shape=jax.ShapeDtypeStruct((m, n), out_dtype),
        grid=(m // tm, n // tn),
        in_specs=in_specs,
        out_specs=pl.BlockSpec((tm, tn), lambda i, j: (i, j)),
        compiler_params=_params("parallel", "parallel"),
        name="matmul",
    )(*args)


def _gmlp_body(zu_ref, zv_ref, g_ref, b_ref, ws_ref, bs_ref, o_ref, *, chunk):
    u = jax.nn.gelu(zu_ref[...])
    v = jax.nn.gelu(zv_ref[...])
    mu = jnp.mean(v, axis=-1, keepdims=True)
    var = jnp.mean(jnp.square(v - mu), axis=-1, keepdims=True)
    vn = ((v - mu) * lax.rsqrt(var + LN_EPS) * g_ref[...] + b_ref[...]).astype(BF16)
    rows, width = u.shape
    heads = ws_ref.shape[0]
    hd = width // heads
    bs = bs_ref[...]
    for c in range(rows // chunk):
        rs = slice(c * chunk, (c + 1) * chunk)
        mixed = jnp.concatenate(
            [_dot(ws_ref[h], vn[rs, h * hd:(h + 1) * hd]) for h in range(heads)], axis=1)
        o_ref[rs, :] = (u[rs, :] * (mixed + bs)).astype(o_ref.dtype)


def _gmlp(za, ln_g, ln_b, w_s, b_s):
    m, two_a = za.shape
    mix_a = two_a // 2
    heads, chunk, _ = w_s.shape
    rows = _pick(m, (2 * chunk, chunk))
    bs_full = jnp.repeat(b_s.T, mix_a // heads, axis=1)
    vec = pl.BlockSpec((1, mix_a), lambda i: (0, 0))
    return pl.pallas_call(
        functools.partial(_gmlp_body, chunk=chunk),
        out_shape=jax.ShapeDtypeStruct((m, mix_a), BF16),
        grid=(m // rows,),
        in_specs=[pl.BlockSpec((rows, mix_a), lambda i: (i, 0)),
                  pl.BlockSpec((rows, mix_a), lambda i: (i, 1)),
                  vec, vec,
                  pl.BlockSpec((heads, chunk, chunk), lambda i: (0, 0, 0)),
                  pl.BlockSpec((chunk, mix_a), lambda i: (0, 0))],
        out_specs=pl.BlockSpec((rows, mix_a), lambda i: (i, 0)),
        compiler_params=_params("parallel"),
        name="gmlp",
    )(za, za, ln_g.reshape(1, mix_a), ln_b.reshape(1, mix_a), w_s.astype(BF16), bs_full)


def _headsum(x, j_ref):
    tm, w = x.shape
    nb = w // LANES
    xs = jnp.concatenate([x[:, i * LANES:(i + 1) * LANES] for i in range(nb)], axis=0)
    hi, lo = _split(xs)
    j = j_ref[...]
    s = _dot(hi, j) + _dot(lo, j)
    return jnp.concatenate([s[i * tm:(i + 1) * tm] for i in range(nb)], axis=1)


def _tshift(z, zp, zn, mu, first, last):
    tm = z.shape[0]
    rows = lax.broadcasted_iota(jnp.int32, (tm, 1), 0)
    prev_row = jnp.where(first, 0.0, zp[7:8, :])
    next_row = jnp.where(last, 0.0, zn[0:1, :])
    prev = jnp.where(rows == 0, prev_row, pltpu.roll(z, 1, 0))
    nxt = jnp.where(rows == tm - 1, next_row, pltpu.roll(z, tm - 1, 0))
    return z + mu * (0.5 * (prev + nxt) - z)


def _prep_body(zr, zrp, zrn, zk, zkp, zkn, zv, zvp, zvn, zl, zlp, zln,
               mur, muk, muv, mul, w0_ref, wd_ref, a0_ref, wi_ref, wg_ref,
               kk_ref, ka_ref, rk_ref, j_ref,
               r_out, v_out, kk_out, g_out, bonus_out, lw_out, kd_out, a_out,
               *, lora_w, lora_a):
    i = pl.program_id(1)
    first = i == 0
    last = i == pl.num_programs(1) - 1
    r = _tshift(zr[0], zrp[0], zrn[0], mur[...], first, last)
    k = _tshift(zk[0], zkp[0], zkn[0], muk[...], first, last)
    v = _tshift(zv[0], zvp[0], zvn[0], muv[...], first, last)
    lo = _tshift(zl[0], zlp[0], zln[0], mul[...], first, last)
    xw = lo[:, :lora_w]
    xa = lo[:, lora_w:lora_w + lora_a]
    xg = lo[:, lora_w + lora_a:]

    g_out[0] = _dot(jax.nn.sigmoid(xg).astype(BF16), wg_ref[...]).astype(g_out.dtype)
    kkr = k * kk_ref[...]
    ss = _headsum(kkr * kkr, j_ref)
    kk = kkr * lax.rsqrt(jnp.maximum(ss, L2_EPS))
    tw = jnp.tanh(xw).astype(BF16)
    xab = xa.astype(BF16)
    rk = None
    for d in range(2):
        logw = -jax.nn.softplus(-(w0_ref[d:d + 1, :] + _dot(tw, wd_ref[d]))) - 0.5
        lw_out[d, 0] = -jnp.exp(logw)
        a = jax.nn.sigmoid(a0_ref[d:d + 1, :] + _dot(xab, wi_ref[d]))
        kd = k * (1.0 + (a - 1.0) * ka_ref[...])
        a_out[d, 0] = a.astype(a_out.dtype)
        kd_out[d, 0] = kd.astype(kd_out.dtype)
        s = _headsum(r * kd * rk_ref[...], j_ref)
        rk = s if rk is None else rk + s
    r_out[0] = r.astype(r_out.dtype)
    v_out[0] = v.astype(v_out.dtype)
    kk_out[0] = kk.astype(kk_out.dtype)
    bonus_out[0] = (rk * v).astype(bonus_out.dtype)


def _rwkv_prep(zrkv, zl, mu, w0, w_up_decay, a0, w_up_iclr, w_up_gate, k_k, k_a, r_k, jmat):
    b, t, three_b = zrkv.shape
    mix_b = three_b // 3
    nl = zl.shape[-1]
    lora_w, lora_a = w_up_decay.shape[1], w_up_iclr.shape[1]
    tm = _pick(t, (128, 64, 32, 16, 8))
    hb = tm // 8
    nblk8 = t // 8

    def main(w, col):
        return pl.BlockSpec((1, tm, w), lambda bi, i: (bi, i, col))

    def prev(w, col):
        return pl.BlockSpec((1, 8, w), lambda bi, i: (bi, jnp.maximum(i * hb - 1, 0), col))

    def nxt(w, col):
        return pl.BlockSpec((1, 8, w), lambda bi, i: (bi, jnp.minimum((i + 1) * hb, nblk8 - 1), col))

    def const(shape):
        return pl.BlockSpec(shape, lambda bi, i: (0,) * len(shape))

    in_specs, args = [], []
    for col in range(3):
        in_specs += [main(mix_b, col), prev(mix_b, col), nxt(mix_b, col)]
        args += [zrkv] * 3
    in_specs += [main(nl, 0), prev(nl, 0), nxt(nl, 0)]
    args += [zl] * 3
    mu2 = mu.reshape(1, -1)
    in_specs += [pl.BlockSpec((1, mix_b), lambda bi, i, c=c: (0, c)) for c in range(3)]
    args += [mu2[:, :three_b]] * 3
    in_specs.append(const((1, nl)))
    args.append(mu2[:, three_b:])
    consts = [w0, w_up_decay.astype(BF16), a0, w_up_iclr.astype(BF16), w_up_gate.astype(BF16),
              k_k.reshape(1, mix_b), k_a.reshape(1, mix_b), r_k.reshape(1, mix_b), jmat]
    in_specs += [const(c.shape) for c in consts]
    args += consts

    shared = jax.ShapeDtypeStruct((b, t, mix_b), BF16)
    perdir = lambda dt: jax.ShapeDtypeStruct((2, b, t, mix_b), dt)
    o_shared = pl.BlockSpec((1, tm, mix_b), lambda bi, i: (bi, i, 0))
    o_perdir = pl.BlockSpec((2, 1, tm, mix_b), lambda bi, i: (0, bi, i, 0))
    return pl.pallas_call(
        functools.partial(_prep_body, lora_w=lora_w, lora_a=lora_a),
        out_shape=[shared] * 5 + [perdir(F32), perdir(BF16), perdir(BF16)],
        grid=(b, t // tm),
        in_specs=in_specs,
        out_specs=[o_shared] * 5 + [o_perdir] * 3,
        compiler_params=_params("parallel", "parallel"),
        name="rwkv_prep",
    )(*args)


def _mm(a, b, dims, passes):
    dg = lambda x, y: lax.dot_general(x, y, dims, preferred_element_type=F32)
    if passes == 1:
        return dg(a.astype(BF16), b.astype(BF16))
    ah, al = _split(a)
    bh, bl = _split(b)
    return dg(ah, bh) + (dg(ah, bl) + dg(al, bh))


def _halves(x, size):
    blocks = x.shape[0] // (2 * size)
    lo = jnp.concatenate([x[2 * j * size:(2 * j + 1) * size] for j in range(blocks)], axis=0)
    hi = jnp.concatenate([x[(2 * j + 1) * size:(2 * j + 2) * size] for j in range(blocks)], axis=0)
    return lo, hi


def _interleave(lo, hi, size):
    pieces = []
    for j in range(lo.shape[0] // size):
        pieces += [lo[j * size:(j + 1) * size], hi[j * size:(j + 1) * size]]
    return jnp.concatenate(pieces, axis=0)


def _tri_inverse(ls, limit, reverse, passes):
    rows = ls[0].shape[0]
    row = lax.broadcasted_iota(jnp.int32, (rows, rows), 0)
    col = lax.broadcasted_iota(jnp.int32, (rows, rows), 1)
    eye = (row == col).astype(F32)
    sh = INV_BASE.bit_length() - 1
    base = (row >> sh) == (col >> sh)
    ps = [jnp.where(base, l, 0.0) for l in ls]
    ts = [eye + p for p in ps]
    size = 2
    while size < INV_BASE:
        ps = [_mm(p, p, NN, passes) for p in ps]
        ts = [_mm(t, eye + p, NN, passes) for t, p in zip(ts, ps)]
        size *= 2
    size = INV_BASE
    hrow = lax.broadcasted_iota(jnp.int32, (rows // 2, rows), 0)
    hcol = lax.broadcasted_iota(jnp.int32, (rows // 2, rows), 1)
    zero = jnp.zeros((rows // 2, rows), F32)
    while size < limit:
        sh = size.bit_length() - 1
        partner = ((hcol >> (sh + 1)) == (hrow >> sh)) & (((hcol >> sh) & 1) == (1 if reverse else 0))
        act = 0 if reverse else 1
        l_act = [_halves(l, size)[act] for l in ls]
        t_halves = [_halves(t, size) for t in ts]
        xs = [_mm(jnp.where(partner, la, 0.0), t, NN, passes) for la, t in zip(l_act, ts)]
        xs = [_interleave(x, zero, size) if reverse else _interleave(zero, x, size) for x in xs]
        upd = [th[act] + _mm(th[act], x, NN, passes) for th, x in zip(t_halves, xs)]
        ts = [_interleave(u, th[1], size) if reverse else _interleave(th[0], u, size)
              for th, u in zip(t_halves, upd)]
        size *= 2
    return ts


def _scan_body(r_ref, v_ref, kk_ref, lw_ref, kd_ref, a_ref, y_ref, s_ref, *, n, reverse, passes):
    c = pl.program_id(2)
    cs = r_ref.shape[1]

    @pl.when(c == 0)
    def _():
        s_ref[...] = jnp.zeros_like(s_ref)

    sgn = -1 if reverse else 1
    tiles = r_ref.shape[2] // LANES
    hp = LANES // n
    rows = hp * cs

    lw = lw_ref[0, 0]
    rc = lax.broadcasted_iota(jnp.int32, (cs, cs), 0)
    cc = lax.broadcasted_iota(jnp.int32, (cs, cs), 1)
    tri = ((rc - cc) * sgn >= 0).astype(BF16)
    l_hi = lw.astype(BF16)
    l_mid = (lw - l_hi.astype(F32)).astype(BF16)
    l_lo = (lw - l_hi.astype(F32) - l_mid.astype(F32)).astype(BF16)
    cum = _dot(tri, l_hi) + (_dot(tri, l_mid) + _dot(tri, l_lo))
    tot = cum[0:1, :] if reverse else cum[cs - 1:cs, :]
    e_tot = jnp.exp(tot)

    row = lax.broadcasted_iota(jnp.int32, (rows, rows), 0)
    col = lax.broadcasted_iota(jnp.int32, (rows, rows), 1)
    order = ((row & (cs - 1)) - (col & (cs - 1))) * sgn
    strict = order > 0
    incl = order >= 0
    eye_l = (lax.broadcasted_iota(jnp.int32, (LANES, LANES), 0)
             == lax.broadcasted_iota(jnp.int32, (LANES, LANES), 1))
    lane_head = lax.broadcasted_iota(jnp.int32, (cs, LANES), 1) // n
    head_masks = [lane_head == h for h in range(hp)]
    zero_t = jnp.zeros((rows, LANES), BF16)

    def pack(x):
        return jnp.concatenate([jnp.where(mk, x, jnp.zeros_like(x)) for mk in head_masks], axis=0)

    tl = range(tiles)
    mm = functools.partial(_mm, passes=passes)
    cat = jnp.concatenate
    at, rt, bh, kh, vv, aa = [], [], [], [], [], []
    for p in tl:
        sl = slice(p * LANES, (p + 1) * LANES)
        cum_p, lw_p, tot_p = cum[:, sl], lw[:, sl], tot[:, sl]
        kk_p = kk_ref[0, :, sl].astype(F32)
        kd_p = kd_ref[0, 0, :, sl].astype(F32)
        b_p = kk_p * a_ref[0, 0, :, sl].astype(F32)
        e_neg = jnp.exp(-cum_p)
        e_hat = jnp.exp(tot_p - cum_p)
        at.append(pack(-kk_p * jnp.exp(cum_p - lw_p)))
        rt.append(pack(r_ref[0, :, sl].astype(F32) * jnp.exp(cum_p)))
        bh.append(pack((b_p * e_hat).astype(BF16)))
        kh.append(pack((kd_p * e_hat).astype(BF16)))
        vv.append(pack(v_ref[0, :, sl].astype(BF16)))
        bk = cat([pack((b_p * e_neg).astype(BF16)), pack((kd_p * e_neg).astype(BF16))], axis=0)
        aa.append(mm(cat([at[p], rt[p]], axis=0), bk, NT))
    aab = [jnp.where(strict, x[:rows, :rows], 0.0) for x in aa]
    aak = [jnp.where(strict, x[:rows, rows:], 0.0) for x in aa]
    aq = [cat([jnp.where(incl, x[rows:, :rows], 0.0), jnp.where(incl, x[rows:, rows:], 0.0)], axis=1)
          for x in aa]
    tinv = _tri_inverse(aab, cs, reverse, passes)
    akv = [mm(aak[p], vv[p], NN) for p in tl]
    wu = [mm(tinv[p], cat([at[p], akv[p]], axis=1), NN) for p in tl]
    rhs = [cat([wu[p].astype(BF16), cat([zero_t, vv[p]], axis=1)], axis=0) for p in tl]
    qy = [mm(aq[p], rhs[p], NN) for p in tl]
    mn = [mm(cat([bh[p], kh[p]], axis=0), rhs[p], TN) for p in tl]
    lhs = []
    for p in tl:
        m_mat = jnp.where(eye_l, e_tot[:, p * LANES:(p + 1) * LANES], 0.0) + mn[p][:, :LANES]
        lhs.append(cat([m_mat, rt[p] + qy[p][:, :LANES]], axis=0))
    ms = [mm(lhs[p], s_ref[p], NN) for p in tl]
    outs = []
    for p in tl:
        s_ref[p] = ms[p][:LANES] + mn[p][:, LANES:]
        ybd = ms[p][LANES:] + qy[p][:, LANES:]
        y = ybd[:cs]
        for h in range(1, hp):
            y = y + ybd[h * cs:(h + 1) * cs]
        outs.append(y)
    y_ref[0] = cat(outs, axis=1)


def _wkv_scan(r, v, kk, lw, kd, a, n, reverse):
    b, t, mix_b = r.shape
    heads = mix_b // n
    hg = SCAN_HEADS if heads % SCAN_HEADS == 0 else heads
    cs = SCAN_CHUNK
    nc = t // cs
    w = hg * n
    di = 1 if reverse else 0
    tchunk = (lambda ci: nc - 1 - ci) if reverse else (lambda ci: ci)
    shared = pl.BlockSpec((1, cs, w), lambda bi, hi, ci: (bi, tchunk(ci), hi))
    perdir = pl.BlockSpec((1, 1, cs, w), lambda bi, hi, ci: (di, bi, tchunk(ci), hi))
    return pl.pallas_call(
        functools.partial(_scan_body, n=n, reverse=reverse, passes=SCAN_PASSES),
        out_shape=jax.ShapeDtypeStruct((b, t, mix_b), F32),
        grid=(b, heads // hg, nc),
        in_specs=[shared] * 3 + [perdir] * 3,
        out_specs=shared,
        scratch_shapes=[pltpu.VMEM((w // LANES, LANES, LANES), F32)],
        compiler_params=_params("parallel", "parallel", "arbitrary"),
        name="wkv_scan",
    )(r, v, kk, lw, kd, a)


def _post_body(yf_ref, yb_ref, bonus_ref, g_ref, gg_ref, gb_ref, j_ref, o_ref, *, n):
    y = yf_ref[...] + yb_ref[...]
    mean = _headsum(y, j_ref) * (1.0 / n)
    yc = y - mean
    var = _headsum(yc * yc, j_ref) * (1.0 / n)
    yn = yc * lax.rsqrt(var + GN_EPS) * gg_ref[...] + gb_ref[...]
    o_ref[...] = ((yn + bonus_ref[...]) * g_ref[...]).astype(o_ref.dtype)


def _rwkv_post(y_fwd, y_bwd, bonus, g, gn_g, gn_b, jmat, n):
    m, mix_b = y_fwd.shape
    tm = _pick(m, (256, 128, 64, 32, 16, 8))
    row = pl.BlockSpec((tm, mix_b), lambda i: (i, 0))
    vec = pl.BlockSpec((1, mix_b), lambda i: (0, 0))
    return pl.pallas_call(
        functools.partial(_post_body, n=n),
        out_shape=jax.ShapeDtypeStruct((m, mix_b), BF16),
        grid=(m // tm,),
        in_specs=[row, row, row, row, vec, vec, pl.BlockSpec(jmat.shape, lambda i: (0, 0))],
        out_specs=row,
        compiler_params=_params("parallel"),
        name="rwkv_post",
    )(y_fwd, y_bwd, bonus, g, gn_g.reshape(1, mix_b), gn_b.reshape(1, mix_b), jmat)


def _attn_body(q_ref, kv_ref, o_ref, *, heads):
    d = q_ref.shape[-1]
    hd = d // heads
    scale = hd ** -0.5
    for h in range(heads):
        q = q_ref[0, :, h * hd:(h + 1) * hd]
        k = kv_ref[0, :, h * hd:(h + 1) * hd]
        v = kv_ref[0, :, d + h * hd:d + (h + 1) * hd]
        s = lax.dot_general(q, k, NT, preferred_element_type=F32) * scale
        e = jnp.exp(s - jnp.max(s, axis=-1, keepdims=True))
        p = e / jnp.sum(e, axis=-1, keepdims=True)
        o_ref[0, :, h * hd:(h + 1) * hd] = _dot(p.astype(BF16), v).astype(o_ref.dtype)


def _attention(q, kv):
    b, t, d = q.shape
    n_mem = kv.shape[1]
    tq = _pick(t, (512, 256, 128, 64, 32, 16))
    return pl.pallas_call(
        functools.partial(_attn_body, heads=X_HEADS),
        out_shape=jax.ShapeDtypeStruct((b, t, d), BF16),
        grid=(b, t // tq),
        in_specs=[pl.BlockSpec((1, tq, d), lambda bi, i: (bi, i, 0)),
                  pl.BlockSpec((1, n_mem, 2 * d), lambda bi, i: (bi, 0, 0))],
        out_specs=pl.BlockSpec((1, tq, d), lambda bi, i: (bi, i, 0)),
        compiler_params=_params("parallel", "parallel"),
        name="cross_attention",
    )(q, kv)


def _ffn_body(h_ref, hp_ref, hn_ref, wg_ref, wv_ref, cwg_ref, cwv_ref, cbg_ref, cbv_ref, wd_ref,
              o_ref, hx_ref):
    i = pl.program_id(1)
    f = pl.program_id(2)
    tm = h_ref.shape[1]
    halo = BF16_ROWS

    @pl.when(f == 0)
    def _():
        zero = jnp.zeros((halo, h_ref.shape[2]), BF16)
        hx_ref[0:halo, :] = jnp.where(i == 0, zero, hp_ref[0])
        hx_ref[halo:halo + tm, :] = h_ref[0]
        hx_ref[halo + tm:, :] = jnp.where(i == pl.num_programs(1) - 1, zero, hn_ref[0])
        o_ref[...] = jnp.zeros_like(o_ref)

    hx = hx_ref[...]
    rows = tm + 2 * halo

    def conv(w_ref, cw_ref, cb_ref):
        z = _dot(hx, w_ref[...])
        prev = pltpu.roll(z, 1, 0)[halo:halo + tm]
        nxt = pltpu.roll(z, rows - 1, 0)[halo:halo + tm]
        cw = cw_ref[...]
        return cw[0:1] * prev + cw[1:2] * z[halo:halo + tm] + cw[2:3] * nxt + cb_ref[...]

    gate = conv(wg_ref, cwg_ref, cbg_ref)
    val = conv(wv_ref, cwv_ref, cbv_ref)
    act = (jax.nn.silu(gate) * val).astype(BF16)
    o_ref[0] += _dot(act, wd_ref[...])


def _conv_ffn(h, w_up, conv_w, conv_b, w_down):
    b, t, d = h.shape
    d_ff = w_down.shape[0]
    tm = _pick(t, (1024, 512, 256, 128, 64, 32))
    tf = _pick(d_ff, (256, 128))
    nf = d_ff // tf
    hb = tm // BF16_ROWS
    nblk = t // BF16_ROWS
    cb = conv_b.reshape(1, 2 * d_ff)
    once = pl.Buffered(1)
    return pl.pallas_call(
        _ffn_body,
        out_shape=jax.ShapeDtypeStruct((b, t, d), F32),
        grid=(b, t // tm, nf),
        in_specs=[
            pl.BlockSpec((1, tm, d), lambda bi, i, f: (bi, i, 0), pipeline_mode=once),
            pl.BlockSpec((1, BF16_ROWS, d), lambda bi, i, f: (bi, jnp.maximum(i * hb - 1, 0), 0)),
            pl.BlockSpec((1, BF16_ROWS, d), lambda bi, i, f: (bi, jnp.minimum((i + 1) * hb, nblk - 1), 0)),
            pl.BlockSpec((d, tf), lambda bi, i, f: (0, f)),
            pl.BlockSpec((d, tf), lambda bi, i, f: (0, nf + f)),
            pl.BlockSpec((3, tf), lambda bi, i, f: (0, f)),
            pl.BlockSpec((3, tf), lambda bi, i, f: (0, nf + f)),
            pl.BlockSpec((1, tf), lambda bi, i, f: (0, f)),
            pl.BlockSpec((1, tf), lambda bi, i, f: (0, nf + f)),
            pl.BlockSpec((tf, d), lambda bi, i, f: (f, 0)),
        ],
        out_specs=pl.BlockSpec((1, tm, d), lambda bi, i, f: (bi, i, 0), pipeline_mode=once),
        scratch_shapes=[pltpu.VMEM((tm + 2 * BF16_ROWS, d), BF16)],
        compiler_params=_params("parallel", "parallel", "arbitrary"),
        name="conv_ffn",
    )(h, h, h, w_up, w_up, conv_w, conv_w, cb, cb, w_down)


def _trunk(x, mem, p):
    b, t, d = x.shape
    m = b * t
    n_mem = mem.shape[1]
    mix_a = p["ln_v_g"].shape[-1]
    mix_b = p["k_k"].shape[-1]
    n = p["r_k"].shape[-1]
    x2d = x.reshape(m, d)

    h1 = _rmsnorm([x2d], p["norm_mix"], BF16)
    za = _matmul([h1], p["w_in"], F32, col0=0, n=2 * mix_a)
    zrkv = _matmul([h1], p["w_in"], F32, col0=2 * mix_a, n=3 * mix_b)
    zl = _matmul([h1], p["w_in"], F32, col0=2 * mix_a + 3 * mix_b, n=p["w_in"].shape[1] - 2 * mix_a - 3 * mix_b)

    ya = _gmlp(za, p["ln_v_g"], p["ln_v_b"], p["w_s"], p["b_s"])

    r, v, kk, g, bonus, lw, kd, a = _rwkv_prep(
        zrkv.reshape(b, t, 3 * mix_b), zl.reshape(b, t, -1), p["mu_shift"], p["w0"], p["w_up_decay"],
        p["a0"], p["w_up_iclr"], p["w_up_gate"], p["k_k"], p["k_a"], p["r_k"], p["jmat"])
    y_fwd = _wkv_scan(r, v, kk, lw, kd, a, n, reverse=False)
    y_bwd = _wkv_scan(r, v, kk, lw, kd, a, n, reverse=True)
    yb = _rwkv_post(y_fwd.reshape(m, mix_b), y_bwd.reshape(m, mix_b), bonus.reshape(m, mix_b),
                    g.reshape(m, mix_b), p["gn_g"], p["gn_b"], p["jmat"], n)

    x1 = _matmul([ya, yb], p["w_out"], F32, residual=x2d)

    hq = _rmsnorm([x1], p["norm_x"], BF16)
    q = _matmul([hq], p["w_q"], BF16)
    memn = _rmsnorm([mem.reshape(b * n_mem, d)], p["norm_mem"], BF16)
    kv = _matmul([memn], p["w_kv"], BF16)
    o = _attention(q.reshape(b, t, d), kv.reshape(b, n_mem, 2 * d))
    x2 = _matmul([o.reshape(m, d)], p["w_o"], F32, residual=x1)

    hf = _rmsnorm([x2], p["norm_ffn"], BF16)
    ffn = _conv_ffn(hf.reshape(b, t, d), p["w_ffn_up"], p["conv_w"], p["conv_b"], p["w_ffn_down"])
    y = _rmsnorm([x2, ffn.reshape(m, d)], p["norm_out"], F32)
    return y.reshape(b, t, d)


def kernel(x_prompt, x_sample, mem_prompt, mem_sample, norm_mix, w_in, mu_shift, ln_v_g, ln_v_b, w_s, b_s, w0, w_up_decay, a0, w_up_iclr, w_up_gate, k_k, k_a, r_k, gn_g, gn_b, w_out, norm_x, norm_mem, w_q, w_kv, w_o, norm_ffn, w_ffn_up, conv_w, conv_b, w_ffn_down, norm_out):
    depth = w_in.shape[0]
    mix_a = ln_v_g.shape[-1]
    mix_b = k_k.shape[-1]
    n = r_k.shape[-1]
    head_id = jnp.arange(LANES) // n
    jmat = (head_id[:, None] == head_id[None, :]).astype(BF16)

    layers = []
    for l in range(depth):
        layers.append(dict(
            norm_mix=norm_mix[l], w_in=w_in[l].astype(BF16), mu_shift=mu_shift[l], ln_v_g=ln_v_g[l], ln_v_b=ln_v_b[l], w_s=w_s[l], b_s=b_s[l],
            w0=w0[l], w_up_decay=w_up_decay[l], a0=a0[l], w_up_iclr=w_up_iclr[l], w_up_gate=w_up_gate[l],
            k_k=k_k[l], k_a=k_a[l], r_k=r_k[l], gn_g=gn_g[l], gn_b=gn_b[l],
            w_out=w_out[l].astype(BF16), norm_x=norm_x[l], norm_mem=norm_mem[l],
            w_q=w_q[l].astype(BF16), w_kv=w_kv[l].astype(BF16), w_o=w_o[l].astype(BF16),
            norm_ffn=norm_ffn[l], w_ffn_up=w_ffn_up[l].astype(BF16), conv_w=conv_w[l], conv_b=conv_b[l],
            w_ffn_down=w_ffn_down[l].astype(BF16), jmat=jmat))

    (p,) = layers
    p = dict(p, norm_out=norm_out)
    return _trunk(x_prompt, mem_prompt, p), _trunk(x_sample, mem_sample, p)
```

```python
import functools

import jax
import jax.numpy as jnp
from jax import lax
from jax.experimental import pallas as pl
from jax.experimental.pallas import tpu as pltpu

RMS_EPS = 1e-6
LN_EPS = 1e-5
GN_EPS = 64e-5
L2_EPS = 1e-12
X_HEADS = 4

LANES = 128
BF16_ROWS = 16
SCAN_CHUNK = 64
SCAN_HEADS = 32
SCAN_PASSES = 1
NORM_ROWS = 128
INV_BASE = 8
VMEM_LIMIT = 60 * 1024 * 1024

F32 = jnp.float32
BF16 = jnp.bfloat16


def _params(*sem, flags=None):
    return pltpu.CompilerParams(dimension_semantics=sem, vmem_limit_bytes=VMEM_LIMIT, flags=flags)


def _pick(n, candidates):
    for c in candidates:
        if n % c == 0:
            return c
    raise ValueError(f"no tile for {n} in {candidates}")


def _dot(a, b):
    return jnp.dot(a, b, preferred_element_type=F32)


def _split(x):
    hi = x.astype(BF16)
    lo = (x - hi.astype(F32)).astype(BF16)
    return hi, lo


NN = (((1,), (0,)), ((), ()))
NT = (((1,), (1,)), ((), ()))
TN = (((0,), (0,)), ((), ()))


def _rmsnorm_body(*refs, n_in):
    g_ref, o_ref = refs[n_in], refs[n_in + 1]
    x = refs[0][...]
    for r in refs[1:n_in]:
        x = x + r[...]
    ms = jnp.mean(x * x, axis=-1, keepdims=True)
    o_ref[...] = (x * lax.rsqrt(ms + RMS_EPS) * g_ref[...]).astype(o_ref.dtype)


def _rmsnorm(xs, g, out_dtype):
    m, d = xs[0].shape
    tm = _pick(m, (256, 128, 64, 32, 16, 8))
    row = pl.BlockSpec((tm, d), lambda i: (i, 0))
    return pl.pallas_call(
        functools.partial(_rmsnorm_body, n_in=len(xs)),
        out_shape=jax.ShapeDtypeStruct((m, d), out_dtype),
        grid=(m // tm,),
        in_specs=[row] * len(xs) + [pl.BlockSpec((1, d), lambda i: (0, 0))],
        out_specs=row,
        compiler_params=_params("parallel"),
        name="rmsnorm",
    )(*xs, g.reshape(1, d))


def _mm_body(*refs, n_lhs, has_res):
    a_refs, w_refs = refs[:n_lhs], refs[n_lhs:2 * n_lhs]
    o_ref = refs[-1]
    acc = _dot(a_refs[0][...], w_refs[0][...])
    for a, w in zip(a_refs[1:], w_refs[1:]):
        acc = acc + _dot(a[...], w[...])
    if has_res:
        acc = acc + refs[2 * n_lhs][...]
    o_ref[...] = acc.astype(o_ref.dtype)


def _matmul(lhs, w, out_dtype, residual=None, col0=0, n=None):
    m = lhs[0].shape[0]
    n = w.shape[1] if n is None else n
    tm = _pick(m, (1024, 512, 256, 128, 64, 32, 16, 8))
    tn = next(c for c in (1024, 768, 512, 256, 128) if n % c == 0 and col0 % c == 0)
    cb = col0 // tn
    in_specs = [pl.BlockSpec((tm, a.shape[1]), lambda i, j: (i, 0)) for a in lhs]
    in_specs += [pl.BlockSpec((a.shape[1], tn), lambda i, j, rb=rb: (rb, cb + j)) for rb, a in enumerate(lhs)]
    assert all(a.shape[1] == lhs[0].shape[1] for a in lhs) and len(lhs) * lhs[0].shape[1] == w.shape[0]
    args = list(lhs) + [w] * len(lhs)
    if residual is not None:
        in_specs.append(pl.BlockSpec((tm, tn), lambda i, j: (i, j)))
        args.append(residual)
    return pl.pallas_call(
        functools.partial(_mm_body, n_lhs=len(lhs), has_res=residual is not None),
        out_shape=jax.ShapeDtypeStruct((m, n), out_dtype),
        grid=(m // tm, n // tn),
        in_specs=in_specs,
        out_specs=pl.BlockSpec((tm, tn), lambda i, j: (i, j)),
        compiler_params=_params("parallel", "parallel"),
        name="matmul",
    )(*args)


def _gmlp_body(zu_ref, zv_ref, g_ref, b_ref, ws_ref, bs_ref, o_ref, *, chunk):
    u = jax.nn.gelu(zu_ref[...])
    v = jax.nn.gelu(zv_ref[...])
    mu = jnp.mean(v, axis=-1, keepdims=True)
    var = jnp.mean(jnp.square(v - mu), axis=-1, keepdims=True)
    vn = ((v - mu) * lax.rsqrt(var + LN_EPS) * g_ref[...] + b_ref[...]).astype(BF16)
    rows, width = u.shape
    heads = ws_ref.shape[0]
    hd = width // heads
    bs = bs_ref[...]
    for c in range(rows // chunk):
        rs = slice(c * chunk, (c + 1) * chunk)
        mixed = jnp.concatenate(
            [_dot(ws_ref[h], vn[rs, h * hd:(h + 1) * hd]) for h in range(heads)], axis=1)
        o_ref[rs, :] = (u[rs, :] * (mixed + bs)).astype(o_ref.dtype)


def _gmlp(za, ln_g, ln_b, w_s, b_s):
    m, two_a = za.shape
    mix_a = two_a // 2
    heads, chunk, _ = w_s.shape
    rows = _pick(m, (2 * chunk, chunk))
    bs_full = jnp.repeat(b_s.T, mix_a // heads, axis=1)
    vec = pl.BlockSpec((1, mix_a), lambda i: (0, 0))
    return pl.pallas_call(
        functools.partial(_gmlp_body, chunk=chunk),
        out_shape=jax.ShapeDtypeStruct((m, mix_a), BF16),
        grid=(m // rows,),
        in_specs=[pl.BlockSpec((rows, mix_a), lambda i: (i, 0)),
                  pl.BlockSpec((rows, mix_a), lambda i: (i, 1)),
                  vec, vec,
                  pl.BlockSpec((heads, chunk, chunk), lambda i: (0, 0, 0)),
                  pl.BlockSpec((chunk, mix_a), lambda i: (0, 0))],
        out_specs=pl.BlockSpec((rows, mix_a), lambda i: (i, 0)),
        compiler_params=_params("parallel"),
        name="gmlp",
    )(za, za, ln_g.reshape(1, mix_a), ln_b.reshape(1, mix_a), w_s.astype(BF16), bs_full)


def _headsum(x, j_ref):
    tm, w = x.shape
    nb = w // LANES
    xs = jnp.concatenate([x[:, i * LANES:(i + 1) * LANES] for i in range(nb)], axis=0)
    hi, lo = _split(xs)
    j = j_ref[...]
    s = _dot(hi, j) + _dot(lo, j)
    return jnp.concatenate([s[i * tm:(i + 1) * tm] for i in range(nb)], axis=1)


def _tshift(z, zp, zn, mu, first, last):
    tm = z.shape[0]
    rows = lax.broadcasted_iota(jnp.int32, (tm, 1), 0)
    prev_row = jnp.where(first, 0.0, zp[7:8, :])
    next_row = jnp.where(last, 0.0, zn[0:1, :])
    prev = jnp.where(rows == 0, prev_row, pltpu.roll(z, 1, 0))
    nxt = jnp.where(rows == tm - 1, next_row, pltpu.roll(z, tm - 1, 0))
    return z + mu * (0.5 * (prev + nxt) - z)


def _prep_body(zr, zrp, zrn, zk, zkp, zkn, zv, zvp, zvn, zl, zlp, zln,
               mur, muk, muv, mul, w0_ref, wd_ref, a0_ref, wi_ref, wg_ref,
               kk_ref, ka_ref, rk_ref, j_ref,
               r_out, v_out, kk_out, g_out, bonus_out, lw_out, kd_out, a_out,
               *, lora_w, lora_a):
    i = pl.program_id(1)
    first = i == 0
    last = i == pl.num_programs(1) - 1
    r = _tshift(zr[0], zrp[0], zrn[0], mur[...], first, last)
    k = _tshift(zk[0], zkp[0], zkn[0], muk[...], first, last)
    v = _tshift(zv[0], zvp[0], zvn[0], muv[...], first, last)
    lo = _tshift(zl[0], zlp[0], zln[0], mul[...], first, last)
    xw = lo[:, :lora_w]
    xa = lo[:, lora_w:lora_w + lora_a]
    xg = lo[:, lora_w + lora_a:]

    g_out[0] = _dot(jax.nn.sigmoid(xg).astype(BF16), wg_ref[...]).astype(g_out.dtype)
    kkr = k * kk_ref[...]
    ss = _headsum(kkr * kkr, j_ref)
    kk = kkr * lax.rsqrt(jnp.maximum(ss, L2_EPS))
    tw = jnp.tanh(xw).astype(BF16)
    xab = xa.astype(BF16)
    rk = None
    for d in range(2):
        logw = -jax.nn.softplus(-(w0_ref[d:d + 1, :] + _dot(tw, wd_ref[d]))) - 0.5
        lw_out[d, 0] = -jnp.exp(logw)
        a = jax.nn.sigmoid(a0_ref[d:d + 1, :] + _dot(xab, wi_ref[d]))
        kd = k * (1.0 + (a - 1.0) * ka_ref[...])
        a_out[d, 0] = a.astype(a_out.dtype)
        kd_out[d, 0] = kd.astype(kd_out.dtype)
        s = _headsum(r * kd * rk_ref[...], j_ref)
        rk = s if rk is None else rk + s
    r_out[0] = r.astype(r_out.dtype)
    v_out[0] = v.astype(v_out.dtype)
    kk_out[0] = kk.astype(kk_out.dtype)
    bonus_out[0] = (rk * v).astype(bonus_out.dtype)


def _rwkv_prep(zrkv, zl, mu, w0, w_up_decay, a0, w_up_iclr, w_up_gate, k_k, k_a, r_k, jmat):
    b, t, three_b = zrkv.shape
    mix_b = three_b // 3
    nl = zl.shape[-1]
    lora_w, lora_a = w_up_decay.shape[1], w_up_iclr.shape[1]
    tm = _pick(t, (128, 64, 32, 16, 8))
    hb = tm // 8
    nblk8 = t // 8

    def main(w, col):
        return pl.BlockSpec((1, tm, w), lambda bi, i: (bi, i, col))

    def prev(w, col):
        return pl.BlockSpec((1, 8, w), lambda bi, i: (bi, jnp.maximum(i * hb - 1, 0), col))

    def nxt(w, col):
        return pl.BlockSpec((1, 8, w), lambda bi, i: (bi, jnp.minimum((i + 1) * hb, nblk8 - 1), col))

    def const(shape):
        return pl.BlockSpec(shape, lambda bi, i: (0,) * len(shape))

    in_specs, args = [], []
    for col in range(3):
        in_specs += [main(mix_b, col), prev(mix_b, col), nxt(mix_b, col)]
        args += [zrkv] * 3
    in_specs += [main(nl, 0), prev(nl, 0), nxt(nl, 0)]
    args += [zl] * 3
    mu2 = mu.reshape(1, -1)
    in_specs += [pl.BlockSpec((1, mix_b), lambda bi, i, c=c: (0, c)) for c in range(3)]
    args += [mu2[:, :three_b]] * 3
    in_specs.append(const((1, nl)))
    args.append(mu2[:, three_b:])
    consts = [w0, w_up_decay.astype(BF16), a0, w_up_iclr.astype(BF16), w_up_gate.astype(BF16),
              k_k.reshape(1, mix_b), k_a.reshape(1, mix_b), r_k.reshape(1, mix_b), jmat]
    in_specs += [const(c.shape) for c in consts]
    args += consts

    shared = jax.ShapeDtypeStruct((b, t, mix_b), BF16)
    perdir = lambda dt: jax.ShapeDtypeStruct((2, b, t, mix_b), dt)
    o_shared = pl.BlockSpec((1, tm, mix_b), lambda bi, i: (bi, i, 0))
    o_perdir = pl.BlockSpec((2, 1, tm, mix_b), lambda bi, i: (0, bi, i, 0))
    return pl.pallas_call(
        functools.partial(_prep_body, lora_w=lora_w, lora_a=lora_a),
        out_shape=[shared] * 5 + [perdir(F32), perdir(BF16), perdir(BF16)],
        grid=(b, t // tm),
        in_specs=in_specs,
        out_specs=[o_shared] * 5 + [o_perdir] * 3,
        compiler_params=_params("parallel", "parallel"),
        name="rwkv_prep",
    )(*args)


def _mm(a, b, dims, passes):
    dg = lambda x, y: lax.dot_general(x, y, dims, preferred_element_type=F32)
    if passes == 1:
        return dg(a.astype(BF16), b.astype(BF16))
    ah, al = _split(a)
    bh, bl = _split(b)
    return dg(ah, bh) + (dg(ah, bl) + dg(al, bh))


def _halves(x, size):
    blocks = x.shape[0] // (2 * size)
    lo = jnp.concatenate([x[2 * j * size:(2 * j + 1) * size] for j in range(blocks)], axis=0)
    hi = jnp.concatenate([x[(2 * j + 1) * size:(2 * j + 2) * size] for j in range(blocks)], axis=0)
    return lo, hi


def _interleave(lo, hi, size):
    pieces = []
    for j in range(lo.shape[0] // size):
        pieces += [lo[j * size:(j + 1) * size], hi[j * size:(j + 1) * size]]
    return jnp.concatenate(pieces, axis=0)


def _tri_inverse(ls, limit, reverse, passes):
    rows = ls[0].shape[0]
    row = lax.broadcasted_iota(jnp.int32, (rows, rows), 0)
    col = lax.broadcasted_iota(jnp.int32, (rows, rows), 1)
    eye = (row == col).astype(F32)
    sh = INV_BASE.bit_length() - 1
    base = (row >> sh) == (col >> sh)
    ps = [jnp.where(base, l, 0.0) for l in ls]
    ts = [eye + p for p in ps]
    size = 2
    while size < INV_BASE:
        ps = [_mm(p, p, NN, passes) for p in ps]
        ts = [_mm(t, eye + p, NN, passes) for t, p in zip(ts, ps)]
        size *= 2
    size = INV_BASE
    hrow = lax.broadcasted_iota(jnp.int32, (rows // 2, rows), 0)
    hcol = lax.broadcasted_iota(jnp.int32, (rows // 2, rows), 1)
    zero = jnp.zeros((rows // 2, rows), F32)
    while size < limit:
        sh = size.bit_length() - 1
        partner = ((hcol >> (sh + 1)) == (hrow >> sh)) & (((hcol >> sh) & 1) == (1 if reverse else 0))
        act = 0 if reverse else 1
        l_act = [_halves(l, size)[act] for l in ls]
        t_halves = [_halves(t, size) for t in ts]
        xs = [_mm(jnp.where(partner, la, 0.0), t, NN, passes) for la, t in zip(l_act, ts)]
        xs = [_interleave(x, zero, size) if reverse else _interleave(zero, x, size) for x in xs]
        upd = [th[act] + _mm(th[act], x, NN, passes) for th, x in zip(t_halves, xs)]
        ts = [_interleave(u, th[1], size) if reverse else _interleave(th[0], u, size)
              for th, u in zip(t_halves, upd)]
        size *= 2
    return ts


def _scan_body(r_ref, v_ref, kk_ref, lw_ref, kd_ref, a_ref, y_ref, s_ref, *, n, reverse, passes):
    c = pl.program_id(2)
    cs = r_ref.shape[1]

    @pl.when(c == 0)
    def _():
        s_ref[...] = jnp.zeros_like(s_ref)

    sgn = -1 if reverse else 1
    tiles = r_ref.shape[2] // LANES
    hp = LANES // n
    rows = hp * cs

    lw = lw_ref[0, 0]
    rc = lax.broadcasted_iota(jnp.int32, (cs, cs), 0)
    cc = lax.broadcasted_iota(jnp.int32, (cs, cs), 1)
    tri = ((rc - cc) * sgn >= 0).astype(BF16)
    l_hi = lw.astype(BF16)
    l_mid = (lw - l_hi.astype(F32)).astype(BF16)
    l_lo = (lw - l_hi.astype(F32) - l_mid.astype(F32)).astype(BF16)
    cum = _dot(tri, l_hi) + (_dot(tri, l_mid) + _dot(tri, l_lo))
    tot = cum[0:1, :] if reverse else cum[cs - 1:cs, :]
    e_tot = jnp.exp(tot)

    row = lax.broadcasted_iota(jnp.int32, (rows, rows), 0)
    col = lax.broadcasted_iota(jnp.int32, (rows, rows), 1)
    order = ((row & (cs - 1)) - (col & (cs - 1))) * sgn
    strict = order > 0
    incl = order >= 0
    eye_l = (lax.broadcasted_iota(jnp.int32, (LANES, LANES), 0)
             == lax.broadcasted_iota(jnp.int32, (LANES, LANES), 1))
    lane_head = lax.broadcasted_iota(jnp.int32, (cs, LANES), 1) // n
    head_masks = [lane_head == h for h in range(hp)]
    zero_t = jnp.zeros((rows, LANES), BF16)

    def pack(x):
        return jnp.concatenate([jnp.where(mk, x, jnp.zeros_like(x)) for mk in head_masks], axis=0)

    tl = range(tiles)
    mm = functools.partial(_mm, passes=passes)
    cat = jnp.concatenate
    at, rt, bh, kh, vv, aa = [], [], [], [], [], []
    for p in tl:
        sl = slice(p * LANES, (p + 1) * LANES)
        cum_p, lw_p, tot_p = cum[:, sl], lw[:, sl], tot[:, sl]
        kk_p = kk_ref[0, :, sl].astype(F32)
        kd_p = kd_ref[0, 0, :, sl].astype(F32)
        b_p = kk_p * a_ref[0, 0, :, sl].astype(F32)
        e_neg = jnp.exp(-cum_p)
        e_hat = jnp.exp(tot_p - cum_p)
        at.append(pack(-kk_p * jnp.exp(cum_p - lw_p)))
        rt.append(pack(r_ref[0, :, sl].astype(F32) * jnp.exp(cum_p)))
        bh.append(pack((b_p * e_hat).astype(BF16)))
        kh.append(pack((kd_p * e_hat).astype(BF16)))
        vv.append(pack(v_ref[0, :, sl].astype(BF16)))
        bk = cat([pack((b_p * e_neg).astype(BF16)), pack((kd_p * e_neg).astype(BF16))], axis=0)
        aa.append(mm(cat([at[p], rt[p]], axis=0), bk, NT))
    aab = [jnp.where(strict, x[:rows, :rows], 0.0) for x in aa]
    aak = [jnp.where(strict, x[:rows, rows:], 0.0) for x in aa]
    aq = [cat([jnp.where(incl, x[rows:, :rows], 0.0), jnp.where(incl, x[rows:, rows:], 0.0)], axis=1)
          for x in aa]
    tinv = _tri_inverse(aab, cs, reverse, passes)
    akv = [mm(aak[p], vv[p], NN) for p in tl]
    wu = [mm(tinv[p], cat([at[p], akv[p]], axis=1), NN) for p in tl]
    rhs = [cat([wu[p].astype(BF16), cat([zero_t, vv[p]], axis=1)], axis=0) for p in tl]
    qy = [mm(aq[p], rhs[p], NN) for p in tl]
    mn = [mm(cat([bh[p], kh[p]], axis=0), rhs[p], TN) for p in tl]
    lhs = []
    for p in tl:
        m_mat = jnp.where(eye_l, e_tot[:, p * LANES:(p + 1) * LANES], 0.0) + mn[p][:, :LANES]
        lhs.append(cat([m_mat, rt[p] + qy[p][:, :LANES]], axis=0))
    ms = [mm(lhs[p], s_ref[p], NN) for p in tl]
    outs = []
    for p in tl:
        s_ref[p] = ms[p][:LANES] + mn[p][:, LANES:]
        ybd = ms[p][LANES:] + qy[p][:, LANES:]
        y = ybd[:cs]
        for h in range(1, hp):
            y = y + ybd[h * cs:(h + 1) * cs]
        outs.append(y)
    y_ref[0] = cat(outs, axis=1)


def _wkv_scan(r, v, kk, lw, kd, a, n, reverse):
    b, t, mix_b = r.shape
    heads = mix_b // n
    hg = SCAN_HEADS if heads % SCAN_HEADS == 0 else heads
    cs = SCAN_CHUNK
    nc = t // cs
    w = hg * n
    di = 1 if reverse else 0
    tchunk = (lambda ci: nc - 1 - ci) if reverse else (lambda ci: ci)
    shared = pl.BlockSpec((1, cs, w), lambda bi, hi, ci: (bi, tchunk(ci), hi))
    perdir = pl.BlockSpec((1, 1, cs, w), lambda bi, hi, ci: (di, bi, tchunk(ci), hi))
    return pl.pallas_call(
        functools.partial(_scan_body, n=n, reverse=reverse, passes=SCAN_PASSES),
        out_shape=jax.ShapeDtypeStruct((b, t, mix_b), F32),
        grid=(b, heads // hg, nc),
        in_specs=[shared] * 3 + [perdir] * 3,
        out_specs=shared,
        scratch_shapes=[pltpu.VMEM((w // LANES, LANES, LANES), F32)],
        compiler_params=_params("parallel", "parallel", "arbitrary"),
        name="wkv_scan",
    )(r, v, kk, lw, kd, a)


def _post_body(yf_ref, yb_ref, bonus_ref, g_ref, gg_ref, gb_ref, j_ref, o_ref, *, n):
    y = yf_ref[...] + yb_ref[...]
    mean = _headsum(y, j_ref) * (1.0 / n)
    yc = y - mean
    var = _headsum(yc * yc, j_ref) * (1.0 / n)
    yn = yc * lax.rsqrt(var + GN_EPS) * gg_ref[...] + gb_ref[...]
    o_ref[...] = ((yn + bonus_ref[...]) * g_ref[...]).astype(o_ref.dtype)


def _rwkv_post(y_fwd, y_bwd, bonus, g, gn_g, gn_b, jmat, n):
    m, mix_b = y_fwd.shape
    tm = _pick(m, (256, 128, 64, 32, 16, 8))
    row = pl.BlockSpec((tm, mix_b), lambda i: (i, 0))
    vec = pl.BlockSpec((1, mix_b), lambda i: (0, 0))
    return pl.pallas_call(
        functools.partial(_post_body, n=n),
        out_shape=jax.ShapeDtypeStruct((m, mix_b), BF16),
        grid=(m // tm,),
        in_specs=[row, row, row, row, vec, vec, pl.BlockSpec(jmat.shape, lambda i: (0, 0))],
        out_specs=row,
        compiler_params=_params("parallel"),
        name="rwkv_post",
    )(y_fwd, y_bwd, bonus, g, gn_g.reshape(1, mix_b), gn_b.reshape(1, mix_b), jmat)


def _attn_body(q_ref, kv_ref, o_ref, *, heads):
    d = q_ref.shape[-1]
    hd = d // heads
    scale = hd ** -0.5
    for h in range(heads):
        q = q_ref[0, :, h * hd:(h + 1) * hd]
        k = kv_ref[0, :, h * hd:(h + 1) * hd]
        v = kv_ref[0, :, d + h * hd:d + (h + 1) * hd]
        s = lax.dot_general(q, k, NT, preferred_element_type=F32) * scale
        e = jnp.exp(s - jnp.max(s, axis=-1, keepdims=True))
        p = e / jnp.sum(e, axis=-1, keepdims=True)
        o_ref[0, :, h * hd:(h + 1) * hd] = _dot(p.astype(BF16), v).astype(o_ref.dtype)


def _attention(q, kv):
    b, t, d = q.shape
    n_mem = kv.shape[1]
    tq = _pick(t, (512, 256, 128, 64, 32, 16))
    return pl.pallas_call(
        functools.partial(_attn_body, heads=X_HEADS),
        out_shape=jax.ShapeDtypeStruct((b, t, d), BF16),
        grid=(b, t // tq),
        in_specs=[pl.BlockSpec((1, tq, d), lambda bi, i: (bi, i, 0)),
                  pl.BlockSpec((1, n_mem, 2 * d), lambda bi, i: (bi, 0, 0))],
        out_specs=pl.BlockSpec((1, tq, d), lambda bi, i: (bi, i, 0)),
        compiler_params=_params("parallel", "parallel"),
        name="cross_attention",
    )(q, kv)


def _rms(x, g):
    return x * lax.rsqrt(jnp.mean(x * x, axis=-1, keepdims=True) + RMS_EPS) * g


def _ffn_body(x_ref, xp_ref, xn_ref, gin_ref, gout_ref, wg_ref, wv_ref, cwg_ref, cwv_ref, cbg_ref,
              cbv_ref, wd_ref, o_ref, hx_ref):
    i = pl.program_id(1)
    f = pl.program_id(2)
    tm = x_ref.shape[1]
    halo = BF16_ROWS

    @pl.when(f == 0)
    def _():
        g = gin_ref[...]
        zero = jnp.zeros((halo, x_ref.shape[2]), BF16)
        hx_ref[0:halo, :] = jnp.where(i == 0, zero, _rms(xp_ref[0], g).astype(BF16))
        hx_ref[halo + tm:, :] = jnp.where(i == pl.num_programs(1) - 1, zero, _rms(xn_ref[0], g).astype(BF16))

        def norm_rows(c, carry):
            r0 = pl.multiple_of(c * NORM_ROWS, NORM_ROWS)
            hx_ref[pl.ds(halo + r0, NORM_ROWS), :] = _rms(x_ref[0, pl.ds(r0, NORM_ROWS), :], g).astype(BF16)
            o_ref[0, pl.ds(r0, NORM_ROWS), :] = jnp.zeros((NORM_ROWS, x_ref.shape[2]), F32)
            return carry

        lax.fori_loop(0, tm // NORM_ROWS, norm_rows, 0)

    hx = hx_ref[...]
    rows = tm + 2 * halo

    def conv(w_ref, cw_ref, cb_ref):
        z = _dot(hx, w_ref[...])
        prev = pltpu.roll(z, 1, 0)[halo:halo + tm]
        nxt = pltpu.roll(z, rows - 1, 0)[halo:halo + tm]
        cw = cw_ref[...]
        return cw[0:1] * prev + cw[1:2] * z[halo:halo + tm] + cw[2:3] * nxt + cb_ref[...]

    gate = conv(wg_ref, cwg_ref, cbg_ref)
    val = conv(wv_ref, cwv_ref, cbv_ref)
    act = (jax.nn.silu(gate) * val).astype(BF16)
    o_ref[0] += _dot(act, wd_ref[...])

    @pl.when(f == pl.num_programs(2) - 1)
    def _():
        g = gout_ref[...]

        def out_rows(c, carry):
            rs = pl.ds(pl.multiple_of(c * NORM_ROWS, NORM_ROWS), NORM_ROWS)
            o_ref[0, rs, :] = _rms(x_ref[0, rs, :] + o_ref[0, rs, :], g)
            return carry

        lax.fori_loop(0, tm // NORM_ROWS, out_rows, 0)


def _conv_ffn(x, g_in, g_out, w_up, conv_w, conv_b, w_down):
    b, t, d = x.shape
    d_ff = w_down.shape[0]
    tm = _pick(t, (1024, 512, 256, 128, 64, 32))
    tf = _pick(d_ff, (256, 128))
    nf = d_ff // tf
    hb = tm // BF16_ROWS
    nblk = t // BF16_ROWS
    cb = conv_b.reshape(1, 2 * d_ff)
    once = pl.Buffered(1)
    return pl.pallas_call(
        _ffn_body,
        out_shape=jax.ShapeDtypeStruct((b, t, d), F32),
        grid=(b, t // tm, nf),
        in_specs=[
            pl.BlockSpec((1, tm, d), lambda bi, i, f: (bi, i, 0), pipeline_mode=once),
            pl.BlockSpec((1, BF16_ROWS, d), lambda bi, i, f: (bi, jnp.maximum(i * hb - 1, 0), 0)),
            pl.BlockSpec((1, BF16_ROWS, d), lambda bi, i, f: (bi, jnp.minimum((i + 1) * hb, nblk - 1), 0)),
            pl.BlockSpec((1, d), lambda bi, i, f: (0, 0)),
            pl.BlockSpec((1, d), lambda bi, i, f: (0, 0)),
            pl.BlockSpec((d, tf), lambda bi, i, f: (0, f)),
            pl.BlockSpec((d, tf), lambda bi, i, f: (0, nf + f)),
            pl.BlockSpec((3, tf), lambda bi, i, f: (0, f)),
            pl.BlockSpec((3, tf), lambda bi, i, f: (0, nf + f)),
            pl.BlockSpec((1, tf), lambda bi, i, f: (0, f)),
            pl.BlockSpec((1, tf), lambda bi, i, f: (0, nf + f)),
            pl.BlockSpec((tf, d), lambda bi, i, f: (f, 0)),
        ],
        out_specs=pl.BlockSpec((1, tm, d), lambda bi, i, f: (bi, i, 0), pipeline_mode=once),
        scratch_shapes=[pltpu.VMEM((tm + 2 * BF16_ROWS, d), BF16)],
        compiler_params=_params("parallel", "parallel", "arbitrary"),
        name="conv_ffn",
    )(x, x, x, g_in.reshape(1, d), g_out.reshape(1, d), w_up, w_up, conv_w, conv_w, cb, cb, w_down)


def _trunk(x, mem, p):
    b, t, d = x.shape
    m = b * t
    n_mem = mem.shape[1]
    mix_a = p["ln_v_g"].shape[-1]
    mix_b = p["k_k"].shape[-1]
    n = p["r_k"].shape[-1]
    x2d = x.reshape(m, d)

    h1 = _rmsnorm([x2d], p["norm_mix"], BF16)
    za = _matmul([h1], p["w_in"], F32, col0=0, n=2 * mix_a)
    zrkv = _matmul([h1], p["w_in"], F32, col0=2 * mix_a, n=3 * mix_b)
    zl = _matmul([h1], p["w_in"], F32, col0=2 * mix_a + 3 * mix_b, n=p["w_in"].shape[1] - 2 * mix_a - 3 * mix_b)

    ya = _gmlp(za, p["ln_v_g"], p["ln_v_b"], p["w_s"], p["b_s"])

    r, v, kk, g, bonus, lw, kd, a = _rwkv_prep(
        zrkv.reshape(b, t, 3 * mix_b), zl.reshape(b, t, -1), p["mu_shift"], p["w0"], p["w_up_decay"],
        p["a0"], p["w_up_iclr"], p["w_up_gate"], p["k_k"], p["k_a"], p["r_k"], p["jmat"])
    y_fwd = _wkv_scan(r, v, kk, lw, kd, a, n, reverse=False)
    y_bwd = _wkv_scan(r, v, kk, lw, kd, a, n, reverse=True)
    yb = _rwkv_post(y_fwd.reshape(m, mix_b), y_bwd.reshape(m, mix_b), bonus.reshape(m, mix_b),
                    g.reshape(m, mix_b), p["gn_g"], p["gn_b"], p["jmat"], n)

    x1 = _matmul([ya, yb], p["w_out"], F32, residual=x2d)

    hq = _rmsnorm([x1], p["norm_x"], BF16)
    q = _matmul([hq], p["w_q"], BF16)
    memn = _rmsnorm([mem.reshape(b * n_mem, d)], p["norm_mem"], BF16)
    kv = _matmul([memn], p["w_kv"], BF16)
    o = _attention(q.reshape(b, t, d), kv.reshape(b, n_mem, 2 * d))
    x2 = _matmul([o.reshape(m, d)], p["w_o"], F32, residual=x1)

    return _conv_ffn(x2.reshape(b, t, d), p["norm_ffn"], p["norm_out"], p["w_ffn_up"], p["conv_w"],
                     p["conv_b"], p["w_ffn_down"])


def kernel(x_prompt, x_sample, mem_prompt, mem_sample, norm_mix, w_in, mu_shift, ln_v_g, ln_v_b, w_s, b_s, w0, w_up_decay, a0, w_up_iclr, w_up_gate, k_k, k_a, r_k, gn_g, gn_b, w_out, norm_x, norm_mem, w_q, w_kv, w_o, norm_ffn, w_ffn_up, conv_w, conv_b, w_ffn_down, norm_out):
    depth = w_in.shape[0]
    mix_a = ln_v_g.shape[-1]
    mix_b = k_k.shape[-1]
    n = r_k.shape[-1]
    head_id = jnp.arange(LANES) // n
    jmat = (head_id[:, None] == head_id[None, :]).astype(BF16)

    layers = []
    for l in range(depth):
        layers.append(dict(
            norm_mix=norm_mix[l], w_in=w_in[l].astype(BF16), mu_shift=mu_shift[l], ln_v_g=ln_v_g[l], ln_v_b=ln_v_b[l], w_s=w_s[l], b_s=b_s[l],
            w0=w0[l], w_up_decay=w_up_decay[l], a0=a0[l], w_up_iclr=w_up_iclr[l], w_up_gate=w_up_gate[l],
            k_k=k_k[l], k_a=k_a[l], r_k=r_k[l], gn_g=gn_g[l], gn_b=gn_b[l],
            w_out=w_out[l].astype(BF16), norm_x=norm_x[l], norm_mem=norm_mem[l],
            w_q=w_q[l].astype(BF16), w_kv=w_kv[l].astype(BF16), w_o=w_o[l].astype(BF16),
            norm_ffn=norm_ffn[l], w_ffn_up=w_ffn_up[l].astype(BF16), conv_w=conv_w[l], conv_b=conv_b[l],
            w_ffn_down=w_ffn_down[l].astype(BF16), jmat=jmat))

    (p,) = layers
    p = dict(p, norm_out=norm_out)
    return _trunk(x_prompt, mem_prompt, p), _trunk(x_sample, mem_sample, p)
```

```python
import functools

import jax
import jax.numpy as jnp
from jax import lax
from jax.experimental import pallas as pl
from jax.experimental.pallas import tpu as pltpu

RMS_EPS = 1e-6
LN_EPS = 1e-5
GN_EPS = 64e-5
L2_EPS = 1e-12
LOGW_SCALE = 0.6065306597126334
X_HEADS = 4

LANES = 128
BF16_ROWS = 16
SCAN_CHUNK = 64
SCAN_HEADS = 32
SCAN_PASSES = 1
NORM_ROWS = 128
INV_BASE = 8
VMEM_LIMIT = 60 * 1024 * 1024

F32 = jnp.float32
BF16 = jnp.bfloat16


def _params(*sem, flags=None):
    return pltpu.CompilerParams(dimension_semantics=sem, vmem_limit_bytes=VMEM_LIMIT, flags=flags)


def _pick(n, candidates):
    for c in candidates:
        if n % c == 0:
            return c
    raise ValueError(f"no tile for {n} in {candidates}")


def _dot(a, b):
    return jnp.dot(a, b, preferred_element_type=F32)


def _split(x):
    hi = x.astype(BF16)
    lo = (x - hi.astype(F32)).astype(BF16)
    return hi, lo


NN = (((1,), (0,)), ((), ()))
NT = (((1,), (1,)), ((), ()))
TN = (((0,), (0,)), ((), ()))


def _rmsnorm_body(*refs, n_in):
    g_ref, o_ref = refs[n_in], refs[n_in + 1]
    x = refs[0][...]
    for r in refs[1:n_in]:
        x = x + r[...]
    ms = jnp.mean(x * x, axis=-1, keepdims=True)
    o_ref[...] = (x * lax.rsqrt(ms + RMS_EPS) * g_ref[...]).astype(o_ref.dtype)


def _rmsnorm(xs, g, out_dtype):
    m, d = xs[0].shape
    tm = _pick(m, (256, 128, 64, 32, 16, 8))
    row = pl.BlockSpec((tm, d), lambda i: (i, 0))
    return pl.pallas_call(
        functools.partial(_rmsnorm_body, n_in=len(xs)),
        out_shape=jax.ShapeDtypeStruct((m, d), out_dtype),
        grid=(m // tm,),
        in_specs=[row] * len(xs) + [pl.BlockSpec((1, d), lambda i: (0, 0))],
        out_specs=row,
        compiler_params=_params("parallel"),
        name="rmsnorm",
    )(*xs, g.reshape(1, d))


def _mm_body(*refs, n_lhs, has_res):
    a_refs, w_refs = refs[:n_lhs], refs[n_lhs:2 * n_lhs]
    o_ref = refs[-1]
    acc = _dot(a_refs[0][...], w_refs[0][...])
    for a, w in zip(a_refs[1:], w_refs[1:]):
        acc = acc + _dot(a[...], w[...])
    if has_res:
        acc = acc + refs[2 * n_lhs][...]
    o_ref[...] = acc.astype(o_ref.dtype)


def _matmul(lhs, w, out_dtype, residual=None, col0=0, n=None):
    m = lhs[0].shape[0]
    n = w.shape[1] if n is None else n
    tm = _pick(m, (1024, 512, 256, 128, 64, 32, 16, 8))
    tn = next(c for c in (1024, 768, 512, 256, 128) if n % c == 0 and col0 % c == 0)
    cb = col0 // tn
    in_specs = [pl.BlockSpec((tm, a.shape[1]), lambda i, j: (i, 0)) for a in lhs]
    in_specs += [pl.BlockSpec((a.shape[1], tn), lambda i, j, rb=rb: (rb, cb + j)) for rb, a in enumerate(lhs)]
    assert all(a.shape[1] == lhs[0].shape[1] for a in lhs) and len(lhs) * lhs[0].shape[1] == w.shape[0]
    args = list(lhs) + [w] * len(lhs)
    if residual is not None:
        in_specs.append(pl.BlockSpec((tm, tn), lambda i, j: (i, j)))
        args.append(residual)
    return pl.pallas_call(
        functools.partial(_mm_body, n_lhs=len(lhs), has_res=residual is not None),
        out_shape=jax.ShapeDtypeStruct((m, n), out_dtype),
        grid=(m // tm, n // tn),
        in_specs=in_specs,
        out_specs=pl.BlockSpec((tm, tn), lambda i, j: (i, j)),
        compiler_params=_params("parallel", "parallel"),
        name="matmul",
    )(*args)


def _gmlp_body(zu_ref, zv_ref, g_ref, b_ref, ws_ref, bs_ref, o_ref, *, chunk):
    u = jax.nn.gelu(zu_ref[...])
    v = jax.nn.gelu(zv_ref[...])
    mu = jnp.mean(v, axis=-1, keepdims=True)
    var = jnp.mean(jnp.square(v - mu), axis=-1, keepdims=True)
    vn = ((v - mu) * lax.rsqrt(var + LN_EPS) * g_ref[...] + b_ref[...]).astype(BF16)
    rows, width = u.shape
    heads = ws_ref.shape[0]
    hd = width // heads
    bs = bs_ref[...]
    for c in range(rows // chunk):
        rs = slice(c * chunk, (c + 1) * chunk)
        mixed = jnp.concatenate(
            [_dot(ws_ref[h], vn[rs, h * hd:(h + 1) * hd]) for h in range(heads)], axis=1)
        o_ref[rs, :] = (u[rs, :] * (mixed + bs)).astype(o_ref.dtype)


def _gmlp(za, ln_g, ln_b, w_s, b_s):
    m, two_a = za.shape
    mix_a = two_a // 2
    heads, chunk, _ = w_s.shape
    rows = _pick(m, (2 * chunk, chunk))
    bs_full = jnp.repeat(b_s.T, mix_a // heads, axis=1)
    vec = pl.BlockSpec((1, mix_a), lambda i: (0, 0))
    return pl.pallas_call(
        functools.partial(_gmlp_body, chunk=chunk),
        out_shape=jax.ShapeDtypeStruct((m, mix_a), BF16),
        grid=(m // rows,),
        in_specs=[pl.BlockSpec((rows, mix_a), lambda i: (i, 0)),
                  pl.BlockSpec((rows, mix_a), lambda i: (i, 1)),
                  vec, vec,
                  pl.BlockSpec((heads, chunk, chunk), lambda i: (0, 0, 0)),
                  pl.BlockSpec((chunk, mix_a), lambda i: (0, 0))],
        out_specs=pl.BlockSpec((rows, mix_a), lambda i: (i, 0)),
        compiler_params=_params("parallel"),
        name="gmlp",
    )(za, za, ln_g.reshape(1, mix_a), ln_b.reshape(1, mix_a), w_s.astype(BF16), bs_full)


def _headsum(x, j_ref):
    tm, w = x.shape
    nb = w // LANES
    xs = jnp.concatenate([x[:, i * LANES:(i + 1) * LANES] for i in range(nb)], axis=0)
    hi, lo = _split(xs)
    j = j_ref[...]
    s = _dot(hi, j) + _dot(lo, j)
    return jnp.concatenate([s[i * tm:(i + 1) * tm] for i in range(nb)], axis=1)


def _tshift(z, zp, zn, mu, first, last):
    tm = z.shape[0]
    rows = lax.broadcasted_iota(jnp.int32, (8, 1), 0)
    prev_row = jnp.where(first, 0.0, zp[7:8, :])
    next_row = jnp.where(last, 0.0, zn[0:1, :])
    prev = pltpu.roll(z, 1, 0)
    nxt = pltpu.roll(z, tm - 1, 0)
    prev = jnp.concatenate([jnp.where(rows == 0, prev_row, prev[:8]), prev[8:]], axis=0)
    nxt = jnp.concatenate([nxt[:tm - 8], jnp.where(rows == 7, next_row, nxt[tm - 8:])], axis=0)
    return z * (1.0 - mu) + (0.5 * mu) * (prev + nxt)


def _prep_body(zr, zrp, zrn, zk, zkp, zkn, zv, zvp, zvn, zl, zlp, zln,
               mur, muk, muv, mul, w0_ref, wd_ref, a0_ref, wi_ref, wg_ref,
               kk_ref, ka_ref, rk_ref, j_ref,
               r_out, v_out, kk_out, g_out, bonus_out, lw_out, kd_out, a_out,
               *, lora_w, lora_a):
    i = pl.program_id(1)
    first = i == 0
    last = i == pl.num_programs(1) - 1
    r = _tshift(zr[0], zrp[0], zrn[0], mur[...], first, last)
    k = _tshift(zk[0], zkp[0], zkn[0], muk[...], first, last)
    v = _tshift(zv[0], zvp[0], zvn[0], muv[...], first, last)
    lo = _tshift(zl[0], zlp[0], zln[0], mul[...], first, last)
    xw = lo[:, :lora_w]
    xa = lo[:, lora_w:lora_w + lora_a]
    xg = lo[:, lora_w + lora_a:]

    g_out[0] = _dot(jax.nn.sigmoid(xg).astype(BF16), wg_ref[...]).astype(g_out.dtype)
    kkr = k * kk_ref[...]
    ss = _headsum(kkr * kkr, j_ref)
    kk = kkr * lax.rsqrt(jnp.maximum(ss, L2_EPS))
    tw = jnp.tanh(xw).astype(BF16)
    xab = xa.astype(BF16)
    rk = None
    for d in range(2):
        lw_out[d, 0] = -LOGW_SCALE * jax.nn.sigmoid(w0_ref[d:d + 1, :] + _dot(tw, wd_ref[d]))
        a = jax.nn.sigmoid(a0_ref[d:d + 1, :] + _dot(xab, wi_ref[d]))
        kd = k * (1.0 + (a - 1.0) * ka_ref[...])
        a_out[d, 0] = a.astype(a_out.dtype)
        kd_out[d, 0] = kd.astype(kd_out.dtype)
        s = _headsum(r * kd * rk_ref[...], j_ref)
        rk = s if rk is None else rk + s
    r_out[0] = r.astype(r_out.dtype)
    v_out[0] = v.astype(v_out.dtype)
    kk_out[0] = kk.astype(kk_out.dtype)
    bonus_out[0] = (rk * v).astype(bonus_out.dtype)


def _rwkv_prep(zrkv, zl, mu, w0, w_up_decay, a0, w_up_iclr, w_up_gate, k_k, k_a, r_k, jmat):
    b, t, three_b = zrkv.shape
    mix_b = three_b // 3
    nl = zl.shape[-1]
    lora_w, lora_a = w_up_decay.shape[1], w_up_iclr.shape[1]
    tm = _pick(t, (128, 64, 32, 16, 8))
    hb = tm // 8
    nblk8 = t // 8

    def main(w, col):
        return pl.BlockSpec((1, tm, w), lambda bi, i: (bi, i, col))

    def prev(w, col):
        return pl.BlockSpec((1, 8, w), lambda bi, i: (bi, jnp.maximum(i * hb - 1, 0), col))

    def nxt(w, col):
        return pl.BlockSpec((1, 8, w), lambda bi, i: (bi, jnp.minimum((i + 1) * hb, nblk8 - 1), col))

    def const(shape):
        return pl.BlockSpec(shape, lambda bi, i: (0,) * len(shape))

    in_specs, args = [], []
    for col in range(3):
        in_specs += [main(mix_b, col), prev(mix_b, col), nxt(mix_b, col)]
        args += [zrkv] * 3
    in_specs += [main(nl, 0), prev(nl, 0), nxt(nl, 0)]
    args += [zl] * 3
    mu2 = mu.reshape(1, -1)
    in_specs += [pl.BlockSpec((1, mix_b), lambda bi, i, c=c: (0, c)) for c in range(3)]
    args += [mu2[:, :three_b]] * 3
    in_specs.append(const((1, nl)))
    args.append(mu2[:, three_b:])
    consts = [w0, w_up_decay.astype(BF16), a0, w_up_iclr.astype(BF16), w_up_gate.astype(BF16),
              k_k.reshape(1, mix_b), k_a.reshape(1, mix_b), r_k.reshape(1, mix_b), jmat]
    in_specs += [const(c.shape) for c in consts]
    args += consts

    shared = jax.ShapeDtypeStruct((b, t, mix_b), BF16)
    perdir = lambda dt: jax.ShapeDtypeStruct((2, b, t, mix_b), dt)
    o_shared = pl.BlockSpec((1, tm, mix_b), lambda bi, i: (bi, i, 0))
    o_perdir = pl.BlockSpec((2, 1, tm, mix_b), lambda bi, i: (0, bi, i, 0))
    return pl.pallas_call(
        functools.partial(_prep_body, lora_w=lora_w, lora_a=lora_a),
        out_shape=[shared] * 5 + [perdir(F32), perdir(BF16), perdir(BF16)],
        grid=(b, t // tm),
        in_specs=in_specs,
        out_specs=[o_shared] * 5 + [o_perdir] * 3,
        compiler_params=_params("parallel", "parallel"),
        name="rwkv_prep",
    )(*args)


def _mm(a, b, dims, passes):
    dg = lambda x, y: lax.dot_general(x, y, dims, preferred_element_type=F32)
    if passes == 1:
        return dg(a.astype(BF16), b.astype(BF16))
    ah, al = _split(a)
    bh, bl = _split(b)
    return dg(ah, bh) + (dg(ah, bl) + dg(al, bh))


def _halves(x, size):
    blocks = x.shape[0] // (2 * size)
    lo = jnp.concatenate([x[2 * j * size:(2 * j + 1) * size] for j in range(blocks)], axis=0)
    hi = jnp.concatenate([x[(2 * j + 1) * size:(2 * j + 2) * size] for j in range(blocks)], axis=0)
    return lo, hi


def _interleave(lo, hi, size):
    pieces = []
    for j in range(lo.shape[0] // size):
        pieces += [lo[j * size:(j + 1) * size], hi[j * size:(j + 1) * size]]
    return jnp.concatenate(pieces, axis=0)


def _tri_inverse(ls, limit, reverse, passes):
    rows = ls[0].shape[0]
    row = lax.broadcasted_iota(jnp.int32, (rows, rows), 0)
    col = lax.broadcasted_iota(jnp.int32, (rows, rows), 1)
    eye = (row == col).astype(F32)
    sh = INV_BASE.bit_length() - 1
    base = (row >> sh) == (col >> sh)
    ps = [jnp.where(base, l, 0.0) for l in ls]
    ts = [eye + p for p in ps]
    size = 2
    while size < INV_BASE:
        ps = [_mm(p, p, NN, passes) for p in ps]
        ts = [_mm(t, eye + p, NN, passes) for t, p in zip(ts, ps)]
        size *= 2
    size = INV_BASE
    hrow = lax.broadcasted_iota(jnp.int32, (rows // 2, rows), 0)
    hcol = lax.broadcasted_iota(jnp.int32, (rows // 2, rows), 1)
    zero = jnp.zeros((rows // 2, rows), F32)
    while size < limit:
        sh = size.bit_length() - 1
        partner = ((hcol >> (sh + 1)) == (hrow >> sh)) & (((hcol >> sh) & 1) == (1 if reverse else 0))
        act = 0 if reverse else 1
        l_act = [_halves(l, size)[act] for l in ls]
        t_halves = [_halves(t, size) for t in ts]
        xs = [_mm(jnp.where(partner, la, 0.0), t, NN, passes) for la, t in zip(l_act, ts)]
        xs = [_interleave(x, zero, size) if reverse else _interleave(zero, x, size) for x in xs]
        upd = [th[act] + _mm(th[act], x, NN, passes) for th, x in zip(t_halves, xs)]
        ts = [_interleave(u, th[1], size) if reverse else _interleave(th[0], u, size)
              for th, u in zip(t_halves, upd)]
        size *= 2
    return ts


def _scan_body(r_ref, v_ref, kk_ref, lw_ref, kd_ref, a_ref, y_ref, s_ref, *, n, reverse, passes):
    c = pl.program_id(2)
    cs = r_ref.shape[1]

    @pl.when(c == 0)
    def _():
        s_ref[...] = jnp.zeros_like(s_ref)

    sgn = -1 if reverse else 1
    tiles = r_ref.shape[2] // LANES
    hp = LANES // n
    rows = hp * cs

    lw = lw_ref[0, 0]
    rc = lax.broadcasted_iota(jnp.int32, (cs, cs), 0)
    cc = lax.broadcasted_iota(jnp.int32, (cs, cs), 1)
    tri = ((rc - cc) * sgn >= 0).astype(BF16)
    l_hi = lw.astype(BF16)
    l_mid = (lw - l_hi.astype(F32)).astype(BF16)
    l_lo = (lw - l_hi.astype(F32) - l_mid.astype(F32)).astype(BF16)
    cum = _dot(tri, l_hi) + (_dot(tri, l_mid) + _dot(tri, l_lo))
    tot = cum[0:1, :] if reverse else cum[cs - 1:cs, :]
    e_tot = jnp.exp(tot)

    row = lax.broadcasted_iota(jnp.int32, (rows, rows), 0)
    col = lax.broadcasted_iota(jnp.int32, (rows, rows), 1)
    order = ((row & (cs - 1)) - (col & (cs - 1))) * sgn
    strict = order > 0
    incl = order >= 0
    eye_l = (lax.broadcasted_iota(jnp.int32, (LANES, LANES), 0)
             == lax.broadcasted_iota(jnp.int32, (LANES, LANES), 1))
    lane_head = lax.broadcasted_iota(jnp.int32, (cs, LANES), 1) // n
    head_masks = [lane_head == h for h in range(hp)]
    zero_t = jnp.zeros((rows, LANES), BF16)

    def pack(x):
        return jnp.concatenate([jnp.where(mk, x, jnp.zeros_like(x)) for mk in head_masks], axis=0)

    tl = range(tiles)
    mm = functools.partial(_mm, passes=passes)
    cat = jnp.concatenate
    at, rt, bh, kh, vv, aa = [], [], [], [], [], []
    for p in tl:
        sl = slice(p * LANES, (p + 1) * LANES)
        cum_p, lw_p, tot_p = cum[:, sl], lw[:, sl], tot[:, sl]
        kk_p = kk_ref[0, :, sl].astype(F32)
        kd_p = kd_ref[0, 0, :, sl].astype(F32)
        b_p = kk_p * a_ref[0, 0, :, sl].astype(F32)
        e_neg = jnp.exp(-cum_p)
        e_hat = jnp.exp(tot_p - cum_p)
        at.append(pack(-kk_p * jnp.exp(cum_p - lw_p)))
        rt.append(pack(r_ref[0, :, sl].astype(F32) * jnp.exp(cum_p)))
        bh.append(pack((b_p * e_hat).astype(BF16)))
        kh.append(pack((kd_p * e_hat).astype(BF16)))
        vv.append(pack(v_ref[0, :, sl].astype(BF16)))
        bk = cat([pack((b_p * e_neg).astype(BF16)), pack((kd_p * e_neg).astype(BF16))], axis=0)
        aa.append(mm(cat([at[p], rt[p]], axis=0), bk, NT))
    aab = [jnp.where(strict, x[:rows, :rows], 0.0) for x in aa]
    aak = [jnp.where(strict, x[:rows, rows:], 0.0) for x in aa]
    aq = [cat([jnp.where(incl, x[rows:, :rows], 0.0), jnp.where(incl, x[rows:, rows:], 0.0)], axis=1)
          for x in aa]
    tinv = _tri_inverse(aab, cs, reverse, passes)
    akv = [mm(aak[p], vv[p], NN) for p in tl]
    wu = [mm(tinv[p], cat([at[p], akv[p]], axis=1), NN) for p in tl]
    rhs = [cat([wu[p].astype(BF16), cat([zero_t, vv[p]], axis=1)], axis=0) for p in tl]
    qy = [mm(aq[p], rhs[p], NN) for p in tl]
    mn = [mm(cat([bh[p], kh[p]], axis=0), rhs[p], TN) for p in tl]
    lhs = []
    for p in tl:
        m_mat = jnp.where(eye_l, e_tot[:, p * LANES:(p + 1) * LANES], 0.0) + mn[p][:, :LANES]
        lhs.append(cat([m_mat, rt[p] + qy[p][:, :LANES]], axis=0))
    ms = [mm(lhs[p], s_ref[p], NN) for p in tl]
    outs = []
    for p in tl:
        s_ref[p] = ms[p][:LANES] + mn[p][:, LANES:]
        ybd = ms[p][LANES:] + qy[p][:, LANES:]
        y = ybd[:cs]
        for h in range(1, hp):
            y = y + ybd[h * cs:(h + 1) * cs]
        outs.append(y)
    y_ref[0] = cat(outs, axis=1)


def _wkv_scan(r, v, kk, lw, kd, a, n, reverse):
    b, t, mix_b = r.shape
    heads = mix_b // n
    hg = SCAN_HEADS if heads % SCAN_HEADS == 0 else heads
    cs = SCAN_CHUNK
    nc = t // cs
    w = hg * n
    di = 1 if reverse else 0
    tchunk = (lambda ci: nc - 1 - ci) if reverse else (lambda ci: ci)
    shared = pl.BlockSpec((1, cs, w), lambda bi, hi, ci: (bi, tchunk(ci), hi))
    perdir = pl.BlockSpec((1, 1, cs, w), lambda bi, hi, ci: (di, bi, tchunk(ci), hi))
    return pl.pallas_call(
        functools.partial(_scan_body, n=n, reverse=reverse, passes=SCAN_PASSES),
        out_shape=jax.ShapeDtypeStruct((b, t, mix_b), F32),
        grid=(b, heads // hg, nc),
        in_specs=[shared] * 3 + [perdir] * 3,
        out_specs=shared,
        scratch_shapes=[pltpu.VMEM((w // LANES, LANES, LANES), F32)],
        compiler_params=_params("parallel", "parallel", "arbitrary"),
        name="wkv_scan",
    )(r, v, kk, lw, kd, a)


def _post_body(yf_ref, yb_ref, bonus_ref, g_ref, gg_ref, gb_ref, j_ref, o_ref, *, n):
    y = yf_ref[...] + yb_ref[...]
    mean = _headsum(y, j_ref) * (1.0 / n)
    yc = y - mean
    var = _headsum(yc * yc, j_ref) * (1.0 / n)
    yn = yc * lax.rsqrt(var + GN_EPS) * gg_ref[...] + gb_ref[...]
    o_ref[...] = ((yn + bonus_ref[...]) * g_ref[...]).astype(o_ref.dtype)


def _rwkv_post(y_fwd, y_bwd, bonus, g, gn_g, gn_b, jmat, n):
    m, mix_b = y_fwd.shape
    tm = _pick(m, (256, 128, 64, 32, 16, 8))
    row = pl.BlockSpec((tm, mix_b), lambda i: (i, 0))
    vec = pl.BlockSpec((1, mix_b), lambda i: (0, 0))
    return pl.pallas_call(
        functools.partial(_post_body, n=n),
        out_shape=jax.ShapeDtypeStruct((m, mix_b), BF16),
        grid=(m // tm,),
        in_specs=[row, row, row, row, vec, vec, pl.BlockSpec(jmat.shape, lambda i: (0, 0))],
        out_specs=row,
        compiler_params=_params("parallel"),
        name="rwkv_post",
    )(y_fwd, y_bwd, bonus, g, gn_g.reshape(1, mix_b), gn_b.reshape(1, mix_b), jmat)


def _attn_body(q_ref, kv_ref, o_ref, *, heads):
    d = q_ref.shape[-1]
    hd = d // heads
    scale = hd ** -0.5
    for h in range(heads):
        q = q_ref[0, :, h * hd:(h + 1) * hd]
        k = kv_ref[0, :, h * hd:(h + 1) * hd]
        v = kv_ref[0, :, d + h * hd:d + (h + 1) * hd]
        s = lax.dot_general(q, k, NT, preferred_element_type=F32) * scale
        e = jnp.exp(s - jnp.max(s, axis=-1, keepdims=True))
        p = e / jnp.sum(e, axis=-1, keepdims=True)
        o_ref[0, :, h * hd:(h + 1) * hd] = _dot(p.astype(BF16), v).astype(o_ref.dtype)


def _attention(q, kv):
    b, t, d = q.shape
    n_mem = kv.shape[1]
    tq = _pick(t, (512, 256, 128, 64, 32, 16))
    return pl.pallas_call(
        functools.partial(_attn_body, heads=X_HEADS),
        out_shape=jax.ShapeDtypeStruct((b, t, d), BF16),
        grid=(b, t // tq),
        in_specs=[pl.BlockSpec((1, tq, d), lambda bi, i: (bi, i, 0)),
                  pl.BlockSpec((1, n_mem, 2 * d), lambda bi, i: (bi, 0, 0))],
        out_specs=pl.BlockSpec((1, tq, d), lambda bi, i: (bi, i, 0)),
        compiler_params=_params("parallel", "parallel"),
        name="cross_attention",
    )(q, kv)


def _rms(x, g):
    return x * lax.rsqrt(jnp.mean(x * x, axis=-1, keepdims=True) + RMS_EPS) * g


def _ffn_body(x_ref, xp_ref, xn_ref, gin_ref, gout_ref, wg_ref, wv_ref, cwg_ref, cwv_ref, cbg_ref,
              cbv_ref, wd_ref, o_ref, hx_ref):
    i = pl.program_id(1)
    f = pl.program_id(2)
    tm = x_ref.shape[1]
    halo = BF16_ROWS

    @pl.when(f == 0)
    def _():
        g = gin_ref[...]
        zero = jnp.zeros((halo, x_ref.shape[2]), BF16)
        hx_ref[0:halo, :] = jnp.where(i == 0, zero, _rms(xp_ref[0], g).astype(BF16))
        hx_ref[halo + tm:, :] = jnp.where(i == pl.num_programs(1) - 1, zero, _rms(xn_ref[0], g).astype(BF16))

        def norm_rows(c, carry):
            r0 = pl.multiple_of(c * NORM_ROWS, NORM_ROWS)
            hx_ref[pl.ds(halo + r0, NORM_ROWS), :] = _rms(x_ref[0, pl.ds(r0, NORM_ROWS), :], g).astype(BF16)
            o_ref[0, pl.ds(r0, NORM_ROWS), :] = jnp.zeros((NORM_ROWS, x_ref.shape[2]), F32)
            return carry

        lax.fori_loop(0, tm // NORM_ROWS, norm_rows, 0)

    hx = hx_ref[...]
    rows = tm + 2 * halo

    def conv(w_ref, cw_ref, cb_ref):
        z = _dot(hx, w_ref[...])
        prev = pltpu.roll(z, 1, 0)[halo:halo + tm]
        nxt = pltpu.roll(z, rows - 1, 0)[halo:halo + tm]
        cw = cw_ref[...]
        return cw[0:1] * prev + cw[1:2] * z[halo:halo + tm] + cw[2:3] * nxt + cb_ref[...]

    gate = conv(wg_ref, cwg_ref, cbg_ref)
    val = conv(wv_ref, cwv_ref, cbv_ref)
    act = (jax.nn.silu(gate) * val).astype(BF16)
    o_ref[0] += _dot(act, wd_ref[...])

    @pl.when(f == pl.num_programs(2) - 1)
    def _():
        g = gout_ref[...]

        def out_rows(c, carry):
            rs = pl.ds(pl.multiple_of(c * NORM_ROWS, NORM_ROWS), NORM_ROWS)
            o_ref[0, rs, :] = _rms(x_ref[0, rs, :] + o_ref[0, rs, :], g)
            return carry

        lax.fori_loop(0, tm // NORM_ROWS, out_rows, 0)


def _conv_ffn(x, g_in, g_out, w_up, conv_w, conv_b, w_down):
    b, t, d = x.shape
    d_ff = w_down.shape[0]
    tm = _pick(t, (1024, 512, 256, 128, 64, 32))
    tf = _pick(d_ff, (256, 128))
    nf = d_ff // tf
    hb = tm // BF16_ROWS
    nblk = t // BF16_ROWS
    cb = conv_b.reshape(1, 2 * d_ff)
    once = pl.Buffered(1)
    return pl.pallas_call(
        _ffn_body,
        out_shape=jax.ShapeDtypeStruct((b, t, d), F32),
        grid=(b, t // tm, nf),
        in_specs=[
            pl.BlockSpec((1, tm, d), lambda bi, i, f: (bi, i, 0), pipeline_mode=once),
            pl.BlockSpec((1, BF16_ROWS, d), lambda bi, i, f: (bi, jnp.maximum(i * hb - 1, 0), 0)),
            pl.BlockSpec((1, BF16_ROWS, d), lambda bi, i, f: (bi, jnp.minimum((i + 1) * hb, nblk - 1), 0)),
            pl.BlockSpec((1, d), lambda bi, i, f: (0, 0)),
            pl.BlockSpec((1, d), lambda bi, i, f: (0, 0)),
            pl.BlockSpec((d, tf), lambda bi, i, f: (0, f)),
            pl.BlockSpec((d, tf), lambda bi, i, f: (0, nf + f)),
            pl.BlockSpec((3, tf), lambda bi, i, f: (0, f)),
            pl.BlockSpec((3, tf), lambda bi, i, f: (0, nf + f)),
            pl.BlockSpec((1, tf), lambda bi, i, f: (0, f)),
            pl.BlockSpec((1, tf), lambda bi, i, f: (0, nf + f)),
            pl.BlockSpec((tf, d), lambda bi, i, f: (f, 0)),
        ],
        out_specs=pl.BlockSpec((1, tm, d), lambda bi, i, f: (bi, i, 0), pipeline_mode=once),
        scratch_shapes=[pltpu.VMEM((tm + 2 * BF16_ROWS, d), BF16)],
        compiler_params=_params("parallel", "parallel", "arbitrary"),
        name="conv_ffn",
    )(x, x, x, g_in.reshape(1, d), g_out.reshape(1, d), w_up, w_up, conv_w, conv_w, cb, cb, w_down)


def _trunk(x, mem, p):
    b, t, d = x.shape
    m = b * t
    n_mem = mem.shape[1]
    mix_a = p["ln_v_g"].shape[-1]
    mix_b = p["k_k"].shape[-1]
    n = p["r_k"].shape[-1]
    x2d = x.reshape(m, d)

    h1 = _rmsnorm([x2d], p["norm_mix"], BF16)
    za = _matmul([h1], p["w_in"], F32, col0=0, n=2 * mix_a)
    zrkv = _matmul([h1], p["w_in"], F32, col0=2 * mix_a, n=3 * mix_b)
    zl = _matmul([h1], p["w_in"], F32, col0=2 * mix_a + 3 * mix_b, n=p["w_in"].shape[1] - 2 * mix_a - 3 * mix_b)

    ya = _gmlp(za, p["ln_v_g"], p["ln_v_b"], p["w_s"], p["b_s"])

    r, v, kk, g, bonus, lw, kd, a = _rwkv_prep(
        zrkv.reshape(b, t, 3 * mix_b), zl.reshape(b, t, -1), p["mu_shift"], p["w0"], p["w_up_decay"],
        p["a0"], p["w_up_iclr"], p["w_up_gate"], p["k_k"], p["k_a"], p["r_k"], p["jmat"])
    y_fwd = _wkv_scan(r, v, kk, lw, kd, a, n, reverse=False)
    y_bwd = _wkv_scan(r, v, kk, lw, kd, a, n, reverse=True)
    yb = _rwkv_post(y_fwd.reshape(m, mix_b), y_bwd.reshape(m, mix_b), bonus.reshape(m, mix_b),
                    g.reshape(m, mix_b), p["gn_g"], p["gn_b"], p["jmat"], n)

    x1 = _matmul([ya, yb], p["w_out"], F32, residual=x2d)

    hq = _rmsnorm([x1], p["norm_x"], BF16)
    q = _matmul([hq], p["w_q"], BF16)
    memn = _rmsnorm([mem.reshape(b * n_mem, d)], p["norm_mem"], BF16)
    kv = _matmul([memn], p["w_kv"], BF16)
    o = _attention(q.reshape(b, t, d), kv.reshape(b, n_mem, 2 * d))
    x2 = _matmul([o.reshape(m, d)], p["w_o"], F32, residual=x1)

    return _conv_ffn(x2.reshape(b, t, d), p["norm_ffn"], p["norm_out"], p["w_ffn_up"], p["conv_w"],
                     p["conv_b"], p["w_ffn_down"])


def kernel(x_prompt, x_sample, mem_prompt, mem_sample, norm_mix, w_in, mu_shift, ln_v_g, ln_v_b, w_s, b_s, w0, w_up_decay, a0, w_up_iclr, w_up_gate, k_k, k_a, r_k, gn_g, gn_b, w_out, norm_x, norm_mem, w_q, w_kv, w_o, norm_ffn, w_ffn_up, conv_w, conv_b, w_ffn_down, norm_out):
    depth = w_in.shape[0]
    mix_a = ln_v_g.shape[-1]
    mix_b = k_k.shape[-1]
    n = r_k.shape[-1]
    head_id = jnp.arange(LANES) // n
    jmat = (head_id[:, None] == head_id[None, :]).astype(BF16)

    layers = []
    for l in range(depth):
        layers.append(dict(
            norm_mix=norm_mix[l], w_in=w_in[l].astype(BF16), mu_shift=mu_shift[l], ln_v_g=ln_v_g[l], ln_v_b=ln_v_b[l], w_s=w_s[l], b_s=b_s[l],
            w0=w0[l], w_up_decay=w_up_decay[l], a0=a0[l], w_up_iclr=w_up_iclr[l], w_up_gate=w_up_gate[l],
            k_k=k_k[l], k_a=k_a[l], r_k=r_k[l], gn_g=gn_g[l], gn_b=gn_b[l],
            w_out=w_out[l].astype(BF16), norm_x=norm_x[l], norm_mem=norm_mem[l],
            w_q=w_q[l].astype(BF16), w_kv=w_kv[l].astype(BF16), w_o=w_o[l].astype(BF16),
            norm_ffn=norm_ffn[l], w_ffn_up=w_ffn_up[l].astype(BF16), conv_w=conv_w[l], conv_b=conv_b[l],
            w_ffn_down=w_ffn_down[l].astype(BF16), jmat=jmat))

    (p,) = layers
    p = dict(p, norm_out=norm_out)
    return _trunk(x_prompt, mem_prompt, p), _trunk(x_sample, mem_sample, p)
```

```python
import functools

import jax
import jax.numpy as jnp
from jax import lax
from jax.experimental import pallas as pl
from jax.experimental.pallas import tpu as pltpu

RMS_EPS = 1e-6
LN_EPS = 1e-5
GN_EPS = 64e-5
L2_EPS = 1e-12
LOGW_SCALE = 0.6065306597126334
X_HEADS = 4

LANES = 128
BF16_ROWS = 16
SCAN_CHUNK = 64
SCAN_HEADS = 32
SCAN_PASSES = 1
NORM_ROWS = 128
INV_BASE = 8
VMEM_LIMIT = 60 * 1024 * 1024

F32 = jnp.float32
BF16 = jnp.bfloat16


def _params(*sem, flags=None):
    return pltpu.CompilerParams(dimension_semantics=sem, vmem_limit_bytes=VMEM_LIMIT, flags=flags)


def _pick(n, candidates):
    for c in candidates:
        if n % c == 0:
            return c
    raise ValueError(f"no tile for {n} in {candidates}")


def _dot(a, b):
    return jnp.dot(a, b, preferred_element_type=F32)


def _split(x):
    hi = x.astype(BF16)
    lo = (x - hi.astype(F32)).astype(BF16)
    return hi, lo


NN = (((1,), (0,)), ((), ()))
NT = (((1,), (1,)), ((), ()))
TN = (((0,), (0,)), ((), ()))


def _rmsnorm_body(*refs, n_in):
    g_ref, o_ref = refs[n_in], refs[n_in + 1]
    x = refs[0][...]
    for r in refs[1:n_in]:
        x = x + r[...]
    ms = jnp.mean(x * x, axis=-1, keepdims=True)
    o_ref[...] = (x * lax.rsqrt(ms + RMS_EPS) * g_ref[...]).astype(o_ref.dtype)


def _rmsnorm(xs, g, out_dtype):
    m, d = xs[0].shape
    tm = _pick(m, (256, 128, 64, 32, 16, 8))
    row = pl.BlockSpec((tm, d), lambda i: (i, 0))
    return pl.pallas_call(
        functools.partial(_rmsnorm_body, n_in=len(xs)),
        out_shape=jax.ShapeDtypeStruct((m, d), out_dtype),
        grid=(m // tm,),
        in_specs=[row] * len(xs) + [pl.BlockSpec((1, d), lambda i: (0, 0))],
        out_specs=row,
        compiler_params=_params("parallel"),
        name="rmsnorm",
    )(*xs, g.reshape(1, d))


def _mm_body(*refs, n_lhs, has_res):
    a_refs, w_refs = refs[:n_lhs], refs[n_lhs:2 * n_lhs]
    o_ref = refs[-1]
    acc = _dot(a_refs[0][...], w_refs[0][...])
    for a, w in zip(a_refs[1:], w_refs[1:]):
        acc = acc + _dot(a[...], w[...])
    if has_res:
        acc = acc + refs[2 * n_lhs][...]
    o_ref[...] = acc.astype(o_ref.dtype)


def _matmul(lhs, w, out_dtype, residual=None, col0=0, n=None):
    m = lhs[0].shape[0]
    n = w.shape[1] if n is None else n
    tm = _pick(m, (1024, 512, 256, 128, 64, 32, 16, 8))
    tn = next(c for c in (1024, 768, 512, 256, 128) if n % c == 0 and col0 % c == 0)
    cb = col0 // tn
    in_specs = [pl.BlockSpec((tm, a.shape[1]), lambda i, j: (i, 0)) for a in lhs]
    in_specs += [pl.BlockSpec((a.shape[1], tn), lambda i, j, rb=rb: (rb, cb + j)) for rb, a in enumerate(lhs)]
    assert all(a.shape[1] == lhs[0].shape[1] for a in lhs) and len(lhs) * lhs[0].shape[1] == w.shape[0]
    args = list(lhs) + [w] * len(lhs)
    if residual is not None:
        in_specs.append(pl.BlockSpec((tm, tn), lambda i, j: (i, j)))
        args.append(residual)
    return pl.pallas_call(
        functools.partial(_mm_body, n_lhs=len(lhs), has_res=residual is not None),
        out_shape=jax.ShapeDtypeStruct((m, n), out_dtype),
        grid=(m // tm, n // tn),
        in_specs=in_specs,
        out_specs=pl.BlockSpec((tm, tn), lambda i, j: (i, j)),
        compiler_params=_params("parallel", "parallel"),
        name="matmul",
    )(*args)


def _norm_mm_body(x_ref, g_ref, w_ref, o_ref, h_ref):
    @pl.when(pl.program_id(1) == 0)
    def _():
        def norm_rows(c, carry):
            rs = pl.ds(pl.multiple_of(c * NORM_ROWS, NORM_ROWS), NORM_ROWS)
            h_ref[rs, :] = _rms(x_ref[rs, :], g_ref[...]).astype(BF16)
            return carry

        lax.fori_loop(0, x_ref.shape[0] // NORM_ROWS, norm_rows, 0)

    o_ref[...] = _dot(h_ref[...], w_ref[...]).astype(o_ref.dtype)


def _norm_matmul(x, g, w, out_dtype):
    m, k = x.shape
    n = w.shape[1]
    tm = _pick(m, (512, 256, 128))
    tn = _pick(n, (1024, 512, 256, 128))
    return pl.pallas_call(
        _norm_mm_body,
        out_shape=jax.ShapeDtypeStruct((m, n), out_dtype),
        grid=(m // tm, n // tn),
        in_specs=[pl.BlockSpec((tm, k), lambda i, j: (i, 0)),
                  pl.BlockSpec((1, k), lambda i, j: (0, 0)),
                  pl.BlockSpec((k, tn), lambda i, j: (0, j))],
        out_specs=pl.BlockSpec((tm, tn), lambda i, j: (i, j)),
        scratch_shapes=[pltpu.VMEM((tm, k), BF16)],
        compiler_params=_params("parallel", "arbitrary"),
        name="norm_matmul",
    )(x, g.reshape(1, k), w)


def _gmlp_body(zu_ref, zv_ref, g_ref, b_ref, ws_ref, bs_ref, o_ref, *, chunk):
    u = jax.nn.gelu(zu_ref[...])
    v = jax.nn.gelu(zv_ref[...])
    mu = jnp.mean(v, axis=-1, keepdims=True)
    var = jnp.mean(jnp.square(v - mu), axis=-1, keepdims=True)
    vn = ((v - mu) * lax.rsqrt(var + LN_EPS) * g_ref[...] + b_ref[...]).astype(BF16)
    rows, width = u.shape
    heads = ws_ref.shape[0]
    hd = width // heads
    bs = bs_ref[...]
    for c in range(rows // chunk):
        rs = slice(c * chunk, (c + 1) * chunk)
        mixed = jnp.concatenate(
            [_dot(ws_ref[h], vn[rs, h * hd:(h + 1) * hd]) for h in range(heads)], axis=1)
        o_ref[rs, :] = (u[rs, :] * (mixed + bs)).astype(o_ref.dtype)


def _gmlp(za, ln_g, ln_b, w_s, b_s):
    m, two_a = za.shape
    mix_a = two_a // 2
    heads, chunk, _ = w_s.shape
    rows = _pick(m, (2 * chunk, chunk))
    bs_full = jnp.repeat(b_s.T, mix_a // heads, axis=1)
    vec = pl.BlockSpec((1, mix_a), lambda i: (0, 0))
    return pl.pallas_call(
        functools.partial(_gmlp_body, chunk=chunk),
        out_shape=jax.ShapeDtypeStruct((m, mix_a), BF16),
        grid=(m // rows,),
        in_specs=[pl.BlockSpec((rows, mix_a), lambda i: (i, 0)),
                  pl.BlockSpec((rows, mix_a), lambda i: (i, 1)),
                  vec, vec,
                  pl.BlockSpec((heads, chunk, chunk), lambda i: (0, 0, 0)),
                  pl.BlockSpec((chunk, mix_a), lambda i: (0, 0))],
        out_specs=pl.BlockSpec((rows, mix_a), lambda i: (i, 0)),
        compiler_params=_params("parallel"),
        name="gmlp",
    )(za, za, ln_g.reshape(1, mix_a), ln_b.reshape(1, mix_a), w_s.astype(BF16), bs_full)


def _headsum(x, j_ref):
    tm, w = x.shape
    nb = w // LANES
    xs = jnp.concatenate([x[:, i * LANES:(i + 1) * LANES] for i in range(nb)], axis=0)
    hi, lo = _split(xs)
    j = j_ref[...]
    s = _dot(hi, j) + _dot(lo, j)
    return jnp.concatenate([s[i * tm:(i + 1) * tm] for i in range(nb)], axis=1)


def _tshift(z, zp, zn, mu, first, last):
    tm = z.shape[0]
    rows = lax.broadcasted_iota(jnp.int32, (8, 1), 0)
    prev_row = jnp.where(first, 0.0, zp[7:8, :])
    next_row = jnp.where(last, 0.0, zn[0:1, :])
    prev = pltpu.roll(z, 1, 0)
    nxt = pltpu.roll(z, tm - 1, 0)
    prev = jnp.concatenate([jnp.where(rows == 0, prev_row, prev[:8]), prev[8:]], axis=0)
    nxt = jnp.concatenate([nxt[:tm - 8], jnp.where(rows == 7, next_row, nxt[tm - 8:])], axis=0)
    return z * (1.0 - mu) + (0.5 * mu) * (prev + nxt)


def _prep_body(zr, zrp, zrn, zk, zkp, zkn, zv, zvp, zvn, zl, zlp, zln,
               mur, muk, muv, mul, w0_ref, wd_ref, a0_ref, wi_ref, wg_ref,
               kk_ref, ka_ref, rk_ref, j_ref,
               r_out, v_out, kk_out, g_out, bonus_out, lw_out, kd_out, a_out,
               *, lora_w, lora_a):
    i = pl.program_id(1)
    first = i == 0
    last = i == pl.num_programs(1) - 1
    r = _tshift(zr[0], zrp[0], zrn[0], mur[...], first, last)
    k = _tshift(zk[0], zkp[0], zkn[0], muk[...], first, last)
    v = _tshift(zv[0], zvp[0], zvn[0], muv[...], first, last)
    lo = _tshift(zl[0], zlp[0], zln[0], mul[...], first, last)
    xw = lo[:, :lora_w]
    xa = lo[:, lora_w:lora_w + lora_a]
    xg = lo[:, lora_w + lora_a:]

    g_out[0] = _dot(jax.nn.sigmoid(xg).astype(BF16), wg_ref[...]).astype(g_out.dtype)
    kkr = k * kk_ref[...]
    ss = _headsum(kkr * kkr, j_ref)
    kk = kkr * lax.rsqrt(jnp.maximum(ss, L2_EPS))
    tw = jnp.tanh(xw).astype(BF16)
    xab = xa.astype(BF16)
    rk = None
    for d in range(2):
        lw_out[d, 0] = -LOGW_SCALE * jax.nn.sigmoid(w0_ref[d:d + 1, :] + _dot(tw, wd_ref[d]))
        a = jax.nn.sigmoid(a0_ref[d:d + 1, :] + _dot(xab, wi_ref[d]))
        kd = k * (1.0 + (a - 1.0) * ka_ref[...])
        a_out[d, 0] = a.astype(a_out.dtype)
        kd_out[d, 0] = kd.astype(kd_out.dtype)
        s = _headsum(r * kd * rk_ref[...], j_ref)
        rk = s if rk is None else rk + s
    r_out[0] = r.astype(r_out.dtype)
    v_out[0] = v.astype(v_out.dtype)
    kk_out[0] = kk.astype(kk_out.dtype)
    bonus_out[0] = (rk * v).astype(bonus_out.dtype)


def _rwkv_prep(zrkv, zl, mu, w0, w_up_decay, a0, w_up_iclr, w_up_gate, k_k, k_a, r_k, jmat):
    b, t, three_b = zrkv.shape
    mix_b = three_b // 3
    nl = zl.shape[-1]
    lora_w, lora_a = w_up_decay.shape[1], w_up_iclr.shape[1]
    tm = _pick(t, (128, 64, 32, 16, 8))
    hb = tm // 8
    nblk8 = t // 8

    def main(w, col):
        return pl.BlockSpec((1, tm, w), lambda bi, i: (bi, i, col))

    def prev(w, col):
        return pl.BlockSpec((1, 8, w), lambda bi, i: (bi, jnp.maximum(i * hb - 1, 0), col))

    def nxt(w, col):
        return pl.BlockSpec((1, 8, w), lambda bi, i: (bi, jnp.minimum((i + 1) * hb, nblk8 - 1), col))

    def const(shape):
        return pl.BlockSpec(shape, lambda bi, i: (0,) * len(shape))

    in_specs, args = [], []
    for col in range(3):
        in_specs += [main(mix_b, col), prev(mix_b, col), nxt(mix_b, col)]
        args += [zrkv] * 3
    in_specs += [main(nl, 0), prev(nl, 0), nxt(nl, 0)]
    args += [zl] * 3
    mu2 = mu.reshape(1, -1)
    in_specs += [pl.BlockSpec((1, mix_b), lambda bi, i, c=c: (0, c)) for c in range(3)]
    args += [mu2[:, :three_b]] * 3
    in_specs.append(const((1, nl)))
    args.append(mu2[:, three_b:])
    consts = [w0, w_up_decay.astype(BF16), a0, w_up_iclr.astype(BF16), w_up_gate.astype(BF16),
              k_k.reshape(1, mix_b), k_a.reshape(1, mix_b), r_k.reshape(1, mix_b), jmat]
    in_specs += [const(c.shape) for c in consts]
    args += consts

    shared = jax.ShapeDtypeStruct((b, t, mix_b), BF16)
    perdir = lambda dt: jax.ShapeDtypeStruct((2, b, t, mix_b), dt)
    o_shared = pl.BlockSpec((1, tm, mix_b), lambda bi, i: (bi, i, 0))
    o_perdir = pl.BlockSpec((2, 1, tm, mix_b), lambda bi, i: (0, bi, i, 0))
    return pl.pallas_call(
        functools.partial(_prep_body, lora_w=lora_w, lora_a=lora_a),
        out_shape=[shared] * 5 + [perdir(F32), perdir(BF16), perdir(BF16)],
        grid=(b, t // tm),
        in_specs=in_specs,
        out_specs=[o_shared] * 5 + [o_perdir] * 3,
        compiler_params=_params("parallel", "parallel"),
        name="rwkv_prep",
    )(*args)


def _mm(a, b, dims, passes):
    dg = lambda x, y: lax.dot_general(x, y, dims, preferred_element_type=F32)
    if passes == 1:
        return dg(a.astype(BF16), b.astype(BF16))
    ah, al = _split(a)
    bh, bl = _split(b)
    return dg(ah, bh) + (dg(ah, bl) + dg(al, bh))


def _halves(x, size):
    blocks = x.shape[0] // (2 * size)
    lo = jnp.concatenate([x[2 * j * size:(2 * j + 1) * size] for j in range(blocks)], axis=0)
    hi = jnp.concatenate([x[(2 * j + 1) * size:(2 * j + 2) * size] for j in range(blocks)], axis=0)
    return lo, hi


def _interleave(lo, hi, size):
    pieces = []
    for j in range(lo.shape[0] // size):
        pieces += [lo[j * size:(j + 1) * size], hi[j * size:(j + 1) * size]]
    return jnp.concatenate(pieces, axis=0)


def _tri_inverse(ls, limit, reverse, passes):
    rows = ls[0].shape[0]
    row = lax.broadcasted_iota(jnp.int32, (rows, rows), 0)
    col = lax.broadcasted_iota(jnp.int32, (rows, rows), 1)
    eye = (row == col).astype(F32)
    sh = INV_BASE.bit_length() - 1
    base = (row >> sh) == (col >> sh)
    ps = [jnp.where(base, l, 0.0) for l in ls]
    ts = [eye + p for p in ps]
    size = 2
    while size < INV_BASE:
        ps = [_mm(p, p, NN, passes) for p in ps]
        ts = [_mm(t, eye + p, NN, passes) for t, p in zip(ts, ps)]
        size *= 2
    size = INV_BASE
    hrow = lax.broadcasted_iota(jnp.int32, (rows // 2, rows), 0)
    hcol = lax.broadcasted_iota(jnp.int32, (rows // 2, rows), 1)
    zero = jnp.zeros((rows // 2, rows), F32)
    while size < limit:
        sh = size.bit_length() - 1
        partner = ((hcol >> (sh + 1)) == (hrow >> sh)) & (((hcol >> sh) & 1) == (1 if reverse else 0))
        act = 0 if reverse else 1
        l_act = [_halves(l, size)[act] for l in ls]
        t_halves = [_halves(t, size) for t in ts]
        xs = [_mm(jnp.where(partner, la, 0.0), t, NN, passes) for la, t in zip(l_act, ts)]
        xs = [_interleave(x, zero, size) if reverse else _interleave(zero, x, size) for x in xs]
        upd = [th[act] + _mm(th[act], x, NN, passes) for th, x in zip(t_halves, xs)]
        ts = [_interleave(u, th[1], size) if reverse else _interleave(th[0], u, size)
              for th, u in zip(t_halves, upd)]
        size *= 2
    return ts


def _scan_body(r_ref, v_ref, kk_ref, lw_ref, kd_ref, a_ref, y_ref, s_ref, *, n, reverse, passes):
    c = pl.program_id(2)
    cs = r_ref.shape[1]

    @pl.when(c == 0)
    def _():
        s_ref[...] = jnp.zeros_like(s_ref)

    sgn = -1 if reverse else 1
    tiles = r_ref.shape[2] // LANES
    hp = LANES // n
    rows = hp * cs

    lw = lw_ref[0, 0]
    rc = lax.broadcasted_iota(jnp.int32, (cs, cs), 0)
    cc = lax.broadcasted_iota(jnp.int32, (cs, cs), 1)
    tri = ((rc - cc) * sgn >= 0).astype(BF16)
    l_hi = lw.astype(BF16)
    l_mid = (lw - l_hi.astype(F32)).astype(BF16)
    l_lo = (lw - l_hi.astype(F32) - l_mid.astype(F32)).astype(BF16)
    cum = _dot(tri, l_hi) + (_dot(tri, l_mid) + _dot(tri, l_lo))
    tot = cum[0:1, :] if reverse else cum[cs - 1:cs, :]
    e_tot = jnp.exp(tot)

    row = lax.broadcasted_iota(jnp.int32, (rows, rows), 0)
    col = lax.broadcasted_iota(jnp.int32, (rows, rows), 1)
    order = ((row & (cs - 1)) - (col & (cs - 1))) * sgn
    strict = order > 0
    incl = order >= 0
    eye_l = (lax.broadcasted_iota(jnp.int32, (LANES, LANES), 0)
             == lax.broadcasted_iota(jnp.int32, (LANES, LANES), 1))
    lane_head = lax.broadcasted_iota(jnp.int32, (cs, LANES), 1) // n
    head_masks = [lane_head == h for h in range(hp)]
    zero_t = jnp.zeros((rows, LANES), BF16)

    def pack(x):
        return jnp.concatenate([jnp.where(mk, x, jnp.zeros_like(x)) for mk in head_masks], axis=0)

    tl = range(tiles)
    mm = functools.partial(_mm, passes=passes)
    cat = jnp.concatenate
    at, rt, bh, kh, vv, aa = [], [], [], [], [], []
    for p in tl:
        sl = slice(p * LANES, (p + 1) * LANES)
        cum_p, lw_p, tot_p = cum[:, sl], lw[:, sl], tot[:, sl]
        kk_p = kk_ref[0, :, sl].astype(F32)
        kd_p = kd_ref[0, 0, :, sl].astype(F32)
        b_p = kk_p * a_ref[0, 0, :, sl].astype(F32)
        e_neg = jnp.exp(-cum_p)
        e_hat = jnp.exp(tot_p - cum_p)
        at.append(pack(-kk_p * jnp.exp(cum_p - lw_p)))
        rt.append(pack(r_ref[0, :, sl].astype(F32) * jnp.exp(cum_p)))
        bh.append(pack((b_p * e_hat).astype(BF16)))
        kh.append(pack((kd_p * e_hat).astype(BF16)))
        vv.append(pack(v_ref[0, :, sl].astype(BF16)))
        bk = cat([pack((b_p * e_neg).astype(BF16)), pack((kd_p * e_neg).astype(BF16))], axis=0)
        aa.append(mm(cat([at[p], rt[p]], axis=0), bk, NT))
    aab = [jnp.where(strict, x[:rows, :rows], 0.0) for x in aa]
    aak = [jnp.where(strict, x[:rows, rows:], 0.0) for x in aa]
    aq = [cat([jnp.where(incl, x[rows:, :rows], 0.0), jnp.where(incl, x[rows:, rows:], 0.0)], axis=1)
          for x in aa]
    tinv = _tri_inverse(aab, cs, reverse, passes)
    akv = [mm(aak[p], vv[p], NN) for p in tl]
    wu = [mm(tinv[p], cat([at[p], akv[p]], axis=1), NN) for p in tl]
    rhs = [cat([wu[p].astype(BF16), cat([zero_t, vv[p]], axis=1)], axis=0) for p in tl]
    qy = [mm(aq[p], rhs[p], NN) for p in tl]
    mn = [mm(cat([bh[p], kh[p]], axis=0), rhs[p], TN) for p in tl]
    lhs = []
    for p in tl:
        m_mat = jnp.where(eye_l, e_tot[:, p * LANES:(p + 1) * LANES], 0.0) + mn[p][:, :LANES]
        lhs.append(cat([m_mat, rt[p] + qy[p][:, :LANES]], axis=0))
    ms = [mm(lhs[p], s_ref[p], NN) for p in tl]
    outs = []
    for p in tl:
        s_ref[p] = ms[p][:LANES] + mn[p][:, LANES:]
        ybd = ms[p][LANES:] + qy[p][:, LANES:]
        y = ybd[:cs]
        for h in range(1, hp):
            y = y + ybd[h * cs:(h + 1) * cs]
        outs.append(y)
    y_ref[0] = cat(outs, axis=1)


def _wkv_scan(r, v, kk, lw, kd, a, n, reverse):
    b, t, mix_b = r.shape
    heads = mix_b // n
    hg = SCAN_HEADS if heads % SCAN_HEADS == 0 else heads
    cs = SCAN_CHUNK
    nc = t // cs
    w = hg * n
    di = 1 if reverse else 0
    tchunk = (lambda ci: nc - 1 - ci) if reverse else (lambda ci: ci)
    shared = pl.BlockSpec((1, cs, w), lambda bi, hi, ci: (bi, tchunk(ci), hi))
    perdir = pl.BlockSpec((1, 1, cs, w), lambda bi, hi, ci: (di, bi, tchunk(ci), hi))
    return pl.pallas_call(
        functools.partial(_scan_body, n=n, reverse=reverse, passes=SCAN_PASSES),
        out_shape=jax.ShapeDtypeStruct((b, t, mix_b), F32),
        grid=(b, heads // hg, nc),
        in_specs=[shared] * 3 + [perdir] * 3,
        out_specs=shared,
        scratch_shapes=[pltpu.VMEM((w // LANES, LANES, LANES), F32)],
        compiler_params=_params("parallel", "parallel", "arbitrary"),
        name="wkv_scan",
    )(r, v, kk, lw, kd, a)


def _post_body(yf_ref, yb_ref, bonus_ref, g_ref, gg_ref, gb_ref, j_ref, o_ref, *, n):
    y = yf_ref[...] + yb_ref[...]
    mean = _headsum(y, j_ref) * (1.0 / n)
    yc = y - mean
    var = _headsum(yc * yc, j_ref) * (1.0 / n)
    yn = yc * lax.rsqrt(var + GN_EPS) * gg_ref[...] + gb_ref[...]
    o_ref[...] = ((yn + bonus_ref[...]) * g_ref[...]).astype(o_ref.dtype)


def _rwkv_post(y_fwd, y_bwd, bonus, g, gn_g, gn_b, jmat, n):
    m, mix_b = y_fwd.shape
    tm = _pick(m, (256, 128, 64, 32, 16, 8))
    row = pl.BlockSpec((tm, mix_b), lambda i: (i, 0))
    vec = pl.BlockSpec((1, mix_b), lambda i: (0, 0))
    return pl.pallas_call(
        functools.partial(_post_body, n=n),
        out_shape=jax.ShapeDtypeStruct((m, mix_b), BF16),
        grid=(m // tm,),
        in_specs=[row, row, row, row, vec, vec, pl.BlockSpec(jmat.shape, lambda i: (0, 0))],
        out_specs=row,
        compiler_params=_params("parallel"),
        name="rwkv_post",
    )(y_fwd, y_bwd, bonus, g, gn_g.reshape(1, mix_b), gn_b.reshape(1, mix_b), jmat)


def _attn_body(q_ref, kv_ref, o_ref, *, heads):
    d = q_ref.shape[-1]
    hd = d // heads
    scale = hd ** -0.5
    for h in range(heads):
        q = q_ref[0, :, h * hd:(h + 1) * hd]
        k = kv_ref[0, :, h * hd:(h + 1) * hd]
        v = kv_ref[0, :, d + h * hd:d + (h + 1) * hd]
        s = lax.dot_general(q, k, NT, preferred_element_type=F32) * scale
        e = jnp.exp(s - jnp.max(s, axis=-1, keepdims=True))
        p = e / jnp.sum(e, axis=-1, keepdims=True)
        o_ref[0, :, h * hd:(h + 1) * hd] = _dot(p.astype(BF16), v).astype(o_ref.dtype)


def _attention(q, kv):
    b, t, d = q.shape
    n_mem = kv.shape[1]
    tq = _pick(t, (1024, 512, 256, 128, 64, 32, 16))
    return pl.pallas_call(
        functools.partial(_attn_body, heads=X_HEADS),
        out_shape=jax.ShapeDtypeStruct((b, t, d), BF16),
        grid=(b, t // tq),
        in_specs=[pl.BlockSpec((1, tq, d), lambda bi, i: (bi, i, 0)),
                  pl.BlockSpec((1, n_mem, 2 * d), lambda bi, i: (bi, 0, 0))],
        out_specs=pl.BlockSpec((1, tq, d), lambda bi, i: (bi, i, 0)),
        compiler_params=_params("parallel", "parallel"),
        name="cross_attention",
    )(q, kv)


def _rms(x, g):
    return x * lax.rsqrt(jnp.mean(x * x, axis=-1, keepdims=True) + RMS_EPS) * g


def _ffn_body(x_ref, xp_ref, xn_ref, gin_ref, gout_ref, wu_ref, cw_ref, cb_ref, wd_ref, o_ref, hx_ref):
    i = pl.program_id(1)
    f = pl.program_id(2)
    tm = x_ref.shape[1]
    halo = BF16_ROWS

    @pl.when(f == 0)
    def _():
        g = gin_ref[...]
        zero = jnp.zeros((halo, x_ref.shape[2]), BF16)
        hx_ref[0:halo, :] = jnp.where(i == 0, zero, _rms(xp_ref[0], g).astype(BF16))
        hx_ref[halo + tm:, :] = jnp.where(i == pl.num_programs(1) - 1, zero, _rms(xn_ref[0], g).astype(BF16))

        def norm_rows(c, carry):
            r0 = pl.multiple_of(c * NORM_ROWS, NORM_ROWS)
            hx_ref[pl.ds(halo + r0, NORM_ROWS), :] = _rms(x_ref[0, pl.ds(r0, NORM_ROWS), :], g).astype(BF16)
            o_ref[0, pl.ds(r0, NORM_ROWS), :] = jnp.zeros((NORM_ROWS, x_ref.shape[2]), F32)
            return carry

        lax.fori_loop(0, tm // NORM_ROWS, norm_rows, 0)

    rows = tm + 2 * halo
    tf = wd_ref.shape[0]
    z = _dot(hx_ref[...], wu_ref[...])
    prev = pltpu.roll(z, 1, 0)[halo:halo + tm]
    nxt = pltpu.roll(z, rows - 1, 0)[halo:halo + tm]
    cw = cw_ref[...]
    zc = cw[0:1] * prev + cw[1:2] * z[halo:halo + tm] + cw[2:3] * nxt + cb_ref[...]
    act = (jax.nn.silu(zc[:, :tf]) * zc[:, tf:]).astype(BF16)
    o_ref[0] += _dot(act, wd_ref[...])

    @pl.when(f == pl.num_programs(2) - 1)
    def _():
        g = gout_ref[...]

        def out_rows(c, carry):
            rs = pl.ds(pl.multiple_of(c * NORM_ROWS, NORM_ROWS), NORM_ROWS)
            o_ref[0, rs, :] = _rms(x_ref[0, rs, :] + o_ref[0, rs, :], g)
            return carry

        lax.fori_loop(0, tm // NORM_ROWS, out_rows, 0)


def _ffn_tile(d_ff):
    return _pick(d_ff, (256, 128))


def _pair_tiles(a, tf):
    lead, two_f = a.shape
    nf = two_f // (2 * tf)
    return a.reshape(lead, 2, nf, tf).swapaxes(1, 2).reshape(lead, two_f)


def _conv_ffn(x, g_in, g_out, w_up, conv_w, conv_b, w_down):
    b, t, d = x.shape
    d_ff = w_down.shape[0]
    tm = _pick(t, (1024, 512, 256, 128, 64, 32))
    tf = _ffn_tile(d_ff)
    nf = d_ff // tf
    hb = tm // BF16_ROWS
    nblk = t // BF16_ROWS
    once = pl.Buffered(1)
    return pl.pallas_call(
        _ffn_body,
        out_shape=jax.ShapeDtypeStruct((b, t, d), F32),
        grid=(b, t // tm, nf),
        in_specs=[
            pl.BlockSpec((1, tm, d), lambda bi, i, f: (bi, i, 0), pipeline_mode=once),
            pl.BlockSpec((1, BF16_ROWS, d), lambda bi, i, f: (bi, jnp.maximum(i * hb - 1, 0), 0)),
            pl.BlockSpec((1, BF16_ROWS, d), lambda bi, i, f: (bi, jnp.minimum((i + 1) * hb, nblk - 1), 0)),
            pl.BlockSpec((1, d), lambda bi, i, f: (0, 0)),
            pl.BlockSpec((1, d), lambda bi, i, f: (0, 0)),
            pl.BlockSpec((d, 2 * tf), lambda bi, i, f: (0, f)),
            pl.BlockSpec((3, 2 * tf), lambda bi, i, f: (0, f)),
            pl.BlockSpec((1, 2 * tf), lambda bi, i, f: (0, f)),
            pl.BlockSpec((tf, d), lambda bi, i, f: (f, 0)),
        ],
        out_specs=pl.BlockSpec((1, tm, d), lambda bi, i, f: (bi, i, 0), pipeline_mode=once),
        scratch_shapes=[pltpu.VMEM((tm + 2 * BF16_ROWS, d), BF16)],
        compiler_params=_params("parallel", "parallel", "arbitrary"),
        name="conv_ffn",
    )(x, x, x, g_in.reshape(1, d), g_out.reshape(1, d), w_up, conv_w, conv_b, w_down)


def _trunk(x, mem, p):
    b, t, d = x.shape
    m = b * t
    n_mem = mem.shape[1]
    mix_a = p["ln_v_g"].shape[-1]
    mix_b = p["k_k"].shape[-1]
    n = p["r_k"].shape[-1]
    x2d = x.reshape(m, d)

    h1 = _rmsnorm([x2d], p["norm_mix"], BF16)
    za = _matmul([h1], p["w_in"], F32, col0=0, n=2 * mix_a)
    zrkv = _matmul([h1], p["w_in"], F32, col0=2 * mix_a, n=3 * mix_b)
    zl = _matmul([h1], p["w_in"], F32, col0=2 * mix_a + 3 * mix_b, n=p["w_in"].shape[1] - 2 * mix_a - 3 * mix_b)

    ya = _gmlp(za, p["ln_v_g"], p["ln_v_b"], p["w_s"], p["b_s"])

    r, v, kk, g, bonus, lw, kd, a = _rwkv_prep(
        zrkv.reshape(b, t, 3 * mix_b), zl.reshape(b, t, -1), p["mu_shift"], p["w0"], p["w_up_decay"],
        p["a0"], p["w_up_iclr"], p["w_up_gate"], p["k_k"], p["k_a"], p["r_k"], p["jmat"])
    y_fwd = _wkv_scan(r, v, kk, lw, kd, a, n, reverse=False)
    y_bwd = _wkv_scan(r, v, kk, lw, kd, a, n, reverse=True)
    yb = _rwkv_post(y_fwd.reshape(m, mix_b), y_bwd.reshape(m, mix_b), bonus.reshape(m, mix_b),
                    g.reshape(m, mix_b), p["gn_g"], p["gn_b"], p["jmat"], n)

    x1 = _matmul([ya, yb], p["w_out"], F32, residual=x2d)

    q = _norm_matmul(x1, p["norm_x"], p["w_q"], BF16)
    kv = _norm_matmul(mem.reshape(b * n_mem, d), p["norm_mem"], p["w_kv"], BF16)
    o = _attention(q.reshape(b, t, d), kv.reshape(b, n_mem, 2 * d))
    x2 = _matmul([o.reshape(m, d)], p["w_o"], F32, residual=x1)

    return _conv_ffn(x2.reshape(b, t, d), p["norm_ffn"], p["norm_out"], p["w_ffn_up"], p["conv_w"],
                     p["conv_b"], p["w_ffn_down"])


def kernel(x_prompt, x_sample, mem_prompt, mem_sample, norm_mix, w_in, mu_shift, ln_v_g, ln_v_b, w_s, b_s, w0, w_up_decay, a0, w_up_iclr, w_up_gate, k_k, k_a, r_k, gn_g, gn_b, w_out, norm_x, norm_mem, w_q, w_kv, w_o, norm_ffn, w_ffn_up, conv_w, conv_b, w_ffn_down, norm_out):
    depth = w_in.shape[0]
    n = r_k.shape[-1]
    tf = _ffn_tile(w_ffn_down.shape[1])
    head_id = jnp.arange(LANES) // n
    jmat = (head_id[:, None] == head_id[None, :]).astype(BF16)

    layers = []
    for l in range(depth):
        layers.append(dict(
            norm_mix=norm_mix[l], w_in=w_in[l].astype(BF16), mu_shift=mu_shift[l], ln_v_g=ln_v_g[l], ln_v_b=ln_v_b[l], w_s=w_s[l], b_s=b_s[l],
            w0=w0[l], w_up_decay=w_up_decay[l], a0=a0[l], w_up_iclr=w_up_iclr[l], w_up_gate=w_up_gate[l],
            k_k=k_k[l], k_a=k_a[l], r_k=r_k[l], gn_g=gn_g[l], gn_b=gn_b[l],
            w_out=w_out[l].astype(BF16), norm_x=norm_x[l], norm_mem=norm_mem[l],
            w_q=w_q[l].astype(BF16), w_kv=w_kv[l].astype(BF16), w_o=w_o[l].astype(BF16),
            norm_ffn=norm_ffn[l], w_ffn_up=_pair_tiles(w_ffn_up[l].astype(BF16), tf),
            conv_w=_pair_tiles(conv_w[l], tf), conv_b=_pair_tiles(conv_b[l].reshape(1, -1), tf),
            w_ffn_down=w_ffn_down[l].astype(BF16), jmat=jmat))

    (p,) = layers
    p = dict(p, norm_out=norm_out)
    return _trunk(x_prompt, mem_prompt, p), _trunk(x_sample, mem_sample, p)
```

```python
import functools

import jax
import jax.numpy as jnp
from jax import lax
from jax.experimental import pallas as pl
from jax.experimental.pallas import tpu as pltpu

RMS_EPS = 1e-6
LN_EPS = 1e-5
GN_EPS = 64e-5
L2_EPS = 1e-12
LOGW_SCALE = 0.6065306597126334
X_HEADS = 4

LANES = 128
BF16_ROWS = 16
SCAN_CHUNK = 64
SCAN_HEADS = 32
NORM_ROWS = 128
INV_BASE = 8
VMEM_LIMIT = 60 * 1024 * 1024

F32 = jnp.float32
BF16 = jnp.bfloat16


def _params(*sem, flags=None):
    return pltpu.CompilerParams(dimension_semantics=sem, vmem_limit_bytes=VMEM_LIMIT, flags=flags)


def _pick(n, candidates):
    for c in candidates:
        if n % c == 0:
            return c
    raise ValueError(f"no tile for {n} in {candidates}")


def _dot(a, b):
    return jnp.dot(a, b, preferred_element_type=F32)


def _split(x):
    hi = x.astype(BF16)
    lo = (x - hi.astype(F32)).astype(BF16)
    return hi, lo


NN = (((1,), (0,)), ((), ()))
NT = (((1,), (1,)), ((), ()))
TN = (((0,), (0,)), ((), ()))


def _rmsnorm_body(*refs, n_in):
    g_ref, o_ref = refs[n_in], refs[n_in + 1]
    x = refs[0][...]
    for r in refs[1:n_in]:
        x = x + r[...]
    ms = jnp.mean(x * x, axis=-1, keepdims=True)
    o_ref[...] = (x * lax.rsqrt(ms + RMS_EPS) * g_ref[...]).astype(o_ref.dtype)


def _rmsnorm(xs, g, out_dtype):
    m, d = xs[0].shape
    tm = _pick(m, (256, 128, 64, 32, 16, 8))
    row = pl.BlockSpec((tm, d), lambda i: (i, 0))
    return pl.pallas_call(
        functools.partial(_rmsnorm_body, n_in=len(xs)),
        out_shape=jax.ShapeDtypeStruct((m, d), out_dtype),
        grid=(m // tm,),
        in_specs=[row] * len(xs) + [pl.BlockSpec((1, d), lambda i: (0, 0))],
        out_specs=row,
        compiler_params=_params("parallel"),
        name="rmsnorm",
    )(*xs, g.reshape(1, d))


def _mm_body(*refs, n_lhs, has_res):
    a_refs, w_refs = refs[:n_lhs], refs[n_lhs:2 * n_lhs]
    o_ref = refs[-1]
    acc = _dot(a_refs[0][...], w_refs[0][...])
    for a, w in zip(a_refs[1:], w_refs[1:]):
        acc = acc + _dot(a[...], w[...])
    if has_res:
        acc = acc + refs[2 * n_lhs][...]
    o_ref[...] = acc.astype(o_ref.dtype)


def _matmul(lhs, w, out_dtype, residual=None, col0=0, n=None):
    m = lhs[0].shape[0]
    n = w.shape[1] if n is None else n
    tm = _pick(m, (1024, 512, 256, 128, 64, 32, 16, 8))
    tn = next(c for c in (1024, 768, 512, 256, 128) if n % c == 0 and col0 % c == 0)
    cb = col0 // tn
    in_specs = [pl.BlockSpec((tm, a.shape[1]), lambda i, j: (i, 0)) for a in lhs]
    in_specs += [pl.BlockSpec((a.shape[1], tn), lambda i, j, rb=rb: (rb, cb + j)) for rb, a in enumerate(lhs)]
    assert all(a.shape[1] == lhs[0].shape[1] for a in lhs) and len(lhs) * lhs[0].shape[1] == w.shape[0]
    args = list(lhs) + [w] * len(lhs)
    if residual is not None:
        in_specs.append(pl.BlockSpec((tm, tn), lambda i, j: (i, j)))
        args.append(residual)
    return pl.pallas_call(
        functools.partial(_mm_body, n_lhs=len(lhs), has_res=residual is not None),
        out_shape=jax.ShapeDtypeStruct((m, n), out_dtype),
        grid=(m // tm, n // tn),
        in_specs=in_specs,
        out_specs=pl.BlockSpec((tm, tn), lambda i, j: (i, j)),
        compiler_params=_params("parallel", "parallel"),
        name="matmul",
    )(*args)


def _gmlp_body(zu_ref, zv_ref, g_ref, b_ref, ws_ref, bs_ref, o_ref, *, chunk):
    u = jax.nn.gelu(zu_ref[...])
    v = jax.nn.gelu(zv_ref[...])
    mu = jnp.mean(v, axis=-1, keepdims=True)
    var = jnp.mean(jnp.square(v - mu), axis=-1, keepdims=True)
    vn = ((v - mu) * lax.rsqrt(var + LN_EPS) * g_ref[...] + b_ref[...]).astype(BF16)
    rows, width = u.shape
    heads = ws_ref.shape[0]
    hd = width // heads
    bs = bs_ref[...]
    for c in range(rows // chunk):
        rs = slice(c * chunk, (c + 1) * chunk)
        mixed = jnp.concatenate(
            [_dot(ws_ref[h], vn[rs, h * hd:(h + 1) * hd]) for h in range(heads)], axis=1)
        o_ref[rs, :] = (u[rs, :] * (mixed + bs)).astype(o_ref.dtype)


def _gmlp(za, ln_g, ln_b, w_s, b_s):
    m, two_a = za.shape
    mix_a = two_a // 2
    heads, chunk, _ = w_s.shape
    rows = _pick(m, (2 * chunk, chunk))
    bs_full = jnp.repeat(b_s.T, mix_a // heads, axis=1)
    vec = pl.BlockSpec((1, mix_a), lambda i: (0, 0))
    return pl.pallas_call(
        functools.partial(_gmlp_body, chunk=chunk),
        out_shape=jax.ShapeDtypeStruct((m, mix_a), BF16),
        grid=(m // rows,),
        in_specs=[pl.BlockSpec((rows, mix_a), lambda i: (i, 0)),
                  pl.BlockSpec((rows, mix_a), lambda i: (i, 1)),
                  vec, vec,
                  pl.BlockSpec((heads, chunk, chunk), lambda i: (0, 0, 0)),
                  pl.BlockSpec((chunk, mix_a), lambda i: (0, 0))],
        out_specs=pl.BlockSpec((rows, mix_a), lambda i: (i, 0)),
        compiler_params=_params("parallel"),
        name="gmlp",
    )(za, za, ln_g.reshape(1, mix_a), ln_b.reshape(1, mix_a), w_s.astype(BF16), bs_full)


def _headsum(x, j_ref):
    tm, w = x.shape
    nb = w // LANES
    xs = jnp.concatenate([x[:, i * LANES:(i + 1) * LANES] for i in range(nb)], axis=0)
    hi, lo = _split(xs)
    j = j_ref[...]
    s = _dot(hi, j) + _dot(lo, j)
    return jnp.concatenate([s[i * tm:(i + 1) * tm] for i in range(nb)], axis=1)


def _tshift(z, zp, zn, mu, first, last):
    tm = z.shape[0]
    rows = lax.broadcasted_iota(jnp.int32, (8, 1), 0)
    prev_row = jnp.where(first, 0.0, zp[7:8, :])
    next_row = jnp.where(last, 0.0, zn[0:1, :])
    prev = pltpu.roll(z, 1, 0)
    nxt = pltpu.roll(z, tm - 1, 0)
    prev = jnp.concatenate([jnp.where(rows == 0, prev_row, prev[:8]), prev[8:]], axis=0)
    nxt = jnp.concatenate([nxt[:tm - 8], jnp.where(rows == 7, next_row, nxt[tm - 8:])], axis=0)
    return z * (1.0 - mu) + (0.5 * mu) * (prev + nxt)


def _prep_body(zr, zrp, zrn, zk, zkp, zkn, zv, zvp, zvn, zl, zlp, zln,
               mur, muk, muv, mul, w0_ref, wd_ref, a0_ref, wi_ref, wg_ref,
               kk_ref, ka_ref, rk_ref, j_ref,
               r_out, v_out, kk_out, g_out, bonus_out, lw_out, kd_out, a_out,
               *, lora_w, lora_a):
    i = pl.program_id(1)
    first = i == 0
    last = i == pl.num_programs(1) - 1
    r = _tshift(zr[0], zrp[0], zrn[0], mur[...], first, last)
    k = _tshift(zk[0], zkp[0], zkn[0], muk[...], first, last)
    v = _tshift(zv[0], zvp[0], zvn[0], muv[...], first, last)
    lo = _tshift(zl[0], zlp[0], zln[0], mul[...], first, last)
    xw = lo[:, :lora_w]
    xa = lo[:, lora_w:lora_w + lora_a]
    xg = lo[:, lora_w + lora_a:]

    g_out[0] = _dot(jax.nn.sigmoid(xg).astype(BF16), wg_ref[...]).astype(g_out.dtype)
    kkr = k * kk_ref[...]
    ss = _headsum(kkr * kkr, j_ref)
    kk = kkr * lax.rsqrt(jnp.maximum(ss, L2_EPS))
    tw = jnp.tanh(xw).astype(BF16)
    xab = xa.astype(BF16)
    rk = None
    for d in range(2):
        lw_out[d, 0] = -LOGW_SCALE * jax.nn.sigmoid(w0_ref[d:d + 1, :] + _dot(tw, wd_ref[d]))
        a = jax.nn.sigmoid(a0_ref[d:d + 1, :] + _dot(xab, wi_ref[d]))
        kd = k * (1.0 + (a - 1.0) * ka_ref[...])
        a_out[d, 0] = a.astype(a_out.dtype)
        kd_out[d, 0] = kd.astype(kd_out.dtype)
        s = _headsum(r * kd * rk_ref[...], j_ref)
        rk = s if rk is None else rk + s
    r_out[0] = r.astype(r_out.dtype)
    v_out[0] = v.astype(v_out.dtype)
    kk_out[0] = kk.astype(kk_out.dtype)
    bonus_out[0] = (rk * v).astype(bonus_out.dtype)


def _rwkv_prep(zrkv, zl, mu, w0, w_up_decay, a0, w_up_iclr, w_up_gate, k_k, k_a, r_k, jmat):
    b, t, three_b = zrkv.shape
    mix_b = three_b // 3
    nl = zl.shape[-1]
    lora_w, lora_a = w_up_decay.shape[1], w_up_iclr.shape[1]
    tm = _pick(t, (128, 64, 32, 16, 8))
    hb = tm // 8
    nblk8 = t // 8

    def main(w, col):
        return pl.BlockSpec((1, tm, w), lambda bi, i: (bi, i, col))

    def prev(w, col):
        return pl.BlockSpec((1, 8, w), lambda bi, i: (bi, jnp.maximum(i * hb - 1, 0), col))

    def nxt(w, col):
        return pl.BlockSpec((1, 8, w), lambda bi, i: (bi, jnp.minimum((i + 1) * hb, nblk8 - 1), col))

    def const(shape):
        return pl.BlockSpec(shape, lambda bi, i: (0,) * len(shape))

    in_specs, args = [], []
    for col in range(3):
        in_specs += [main(mix_b, col), prev(mix_b, col), nxt(mix_b, col)]
        args += [zrkv] * 3
    in_specs += [main(nl, 0), prev(nl, 0), nxt(nl, 0)]
    args += [zl] * 3
    mu2 = mu.reshape(1, -1)
    in_specs += [pl.BlockSpec((1, mix_b), lambda bi, i, c=c: (0, c)) for c in range(3)]
    args += [mu2[:, :three_b]] * 3
    in_specs.append(const((1, nl)))
    args.append(mu2[:, three_b:])
    consts = [w0, w_up_decay.astype(BF16), a0, w_up_iclr.astype(BF16), w_up_gate.astype(BF16),
              k_k.reshape(1, mix_b), k_a.reshape(1, mix_b), r_k.reshape(1, mix_b), jmat]
    in_specs += [const(c.shape) for c in consts]
    args += consts

    shared = jax.ShapeDtypeStruct((b, t, mix_b), BF16)
    perdir = lambda dt: jax.ShapeDtypeStruct((2, b, t, mix_b), dt)
    o_shared = pl.BlockSpec((1, tm, mix_b), lambda bi, i: (bi, i, 0))
    o_perdir = pl.BlockSpec((2, 1, tm, mix_b), lambda bi, i: (0, bi, i, 0))
    return pl.pallas_call(
        functools.partial(_prep_body, lora_w=lora_w, lora_a=lora_a),
        out_shape=[shared] * 5 + [perdir(F32), perdir(BF16), perdir(BF16)],
        grid=(b, t // tm),
        in_specs=in_specs,
        out_specs=[o_shared] * 5 + [o_perdir] * 3,
        compiler_params=_params("parallel", "parallel"),
        name="rwkv_prep",
    )(*args)


def _mm(a, b, dims):
    return lax.dot_general(a.astype(BF16), b.astype(BF16), dims, preferred_element_type=F32)


def _halves(x, size):
    blocks = x.shape[0] // (2 * size)
    lo = jnp.concatenate([x[2 * j * size:(2 * j + 1) * size] for j in range(blocks)], axis=0)
    hi = jnp.concatenate([x[(2 * j + 1) * size:(2 * j + 2) * size] for j in range(blocks)], axis=0)
    return lo, hi


def _interleave(lo, hi, size):
    pieces = []
    for j in range(lo.shape[0] // size):
        pieces += [lo[j * size:(j + 1) * size], hi[j * size:(j + 1) * size]]
    return jnp.concatenate(pieces, axis=0)


def _tri_inverse(ls, limit, reverse):
    rows = ls[0].shape[0]
    row = lax.broadcasted_iota(jnp.int32, (rows, rows), 0)
    col = lax.broadcasted_iota(jnp.int32, (rows, rows), 1)
    eye = (row == col).astype(F32)
    sh = INV_BASE.bit_length() - 1
    base = (row >> sh) == (col >> sh)
    ps = [jnp.where(base, l, 0.0) for l in ls]
    ts = [eye + p for p in ps]
    size = 2
    while size < INV_BASE:
        ps = [_mm(p, p, NN) for p in ps]
        ts = [_mm(t, eye + p, NN) for t, p in zip(ts, ps)]
        size *= 2
    size = INV_BASE
    hrow = lax.broadcasted_iota(jnp.int32, (rows // 2, rows), 0)
    hcol = lax.broadcasted_iota(jnp.int32, (rows // 2, rows), 1)
    zero = jnp.zeros((rows // 2, rows), F32)
    while size < limit:
        sh = size.bit_length() - 1
        partner = ((hcol >> (sh + 1)) == (hrow >> sh)) & (((hcol >> sh) & 1) == (1 if reverse else 0))
        act = 0 if reverse else 1
        l_act = [_halves(l, size)[act] for l in ls]
        t_halves = [_halves(t, size) for t in ts]
        xs = [_mm(jnp.where(partner, la, 0.0), t, NN) for la, t in zip(l_act, ts)]
        xs = [_interleave(x, zero, size) if reverse else _interleave(zero, x, size) for x in xs]
        upd = [th[act] + _mm(th[act], x, NN) for th, x in zip(t_halves, xs)]
        ts = [_interleave(u, th[1], size) if reverse else _interleave(th[0], u, size)
              for th, u in zip(t_halves, upd)]
        size *= 2
    return ts


def _scan_body(r_ref, v_ref, kk_ref, lw_ref, kd_ref, a_ref, y_ref, s_ref, *, n, reverse):
    c = pl.program_id(2)
    cs = r_ref.shape[1]

    @pl.when(c == 0)
    def _():
        s_ref[...] = jnp.zeros_like(s_ref)

    sgn = -1 if reverse else 1
    tiles = r_ref.shape[2] // LANES
    hp = LANES // n
    rows = hp * cs

    lw = lw_ref[0, 0]
    rc = lax.broadcasted_iota(jnp.int32, (cs, cs), 0)
    cc = lax.broadcasted_iota(jnp.int32, (cs, cs), 1)
    tri = ((rc - cc) * sgn >= 0).astype(BF16)
    l_hi = lw.astype(BF16)
    l_mid = (lw - l_hi.astype(F32)).astype(BF16)
    l_lo = (lw - l_hi.astype(F32) - l_mid.astype(F32)).astype(BF16)
    cum = _dot(tri, l_hi) + (_dot(tri, l_mid) + _dot(tri, l_lo))
    tot = cum[0:1, :] if reverse else cum[cs - 1:cs, :]
    e_tot = jnp.exp(tot)

    row = lax.broadcasted_iota(jnp.int32, (rows, rows), 0)
    col = lax.broadcasted_iota(jnp.int32, (rows, rows), 1)
    order = ((row & (cs - 1)) - (col & (cs - 1))) * sgn
    strict = order > 0
    incl = order >= 0
    eye_l = (lax.broadcasted_iota(jnp.int32, (LANES, LANES), 0)
             == lax.broadcasted_iota(jnp.int32, (LANES, LANES), 1))
    lane_head = lax.broadcasted_iota(jnp.int32, (cs, LANES), 1) // n
    head_masks = [lane_head == h for h in range(hp)]
    zero_t = jnp.zeros((rows, LANES), BF16)

    def pack(x):
        return jnp.concatenate([jnp.where(mk, x, jnp.zeros_like(x)) for mk in head_masks], axis=0)

    tl = range(tiles)
    cat = jnp.concatenate
    at, rt, bh, kh, vv, aa = [], [], [], [], [], []
    for p in tl:
        sl = slice(p * LANES, (p + 1) * LANES)
        cum_p, lw_p, tot_p = cum[:, sl], lw[:, sl], tot[:, sl]
        kk_p = kk_ref[0, :, sl].astype(F32)
        kd_p = kd_ref[0, 0, :, sl].astype(F32)
        b_p = kk_p * a_ref[0, 0, :, sl].astype(F32)
        e_neg = jnp.exp(-cum_p)
        e_hat = jnp.exp(tot_p - cum_p)
        at.append(pack(-kk_p * jnp.exp(cum_p - lw_p)))
        rt.append(pack(r_ref[0, :, sl].astype(F32) * jnp.exp(cum_p)))
        bh.append(pack((b_p * e_hat).astype(BF16)))
        kh.append(pack((kd_p * e_hat).astype(BF16)))
        vv.append(pack(v_ref[0, :, sl].astype(BF16)))
        bk = cat([pack((b_p * e_neg).astype(BF16)), pack((kd_p * e_neg).astype(BF16))], axis=0)
        aa.append(_mm(cat([at[p], rt[p]], axis=0), bk, NT))
    aab = [jnp.where(strict, x[:rows, :rows], 0.0) for x in aa]
    aak = [jnp.where(strict, x[:rows, rows:], 0.0) for x in aa]
    aq = [cat([jnp.where(incl, x[rows:, :rows], 0.0), jnp.where(incl, x[rows:, rows:], 0.0)], axis=1)
          for x in aa]
    tinv = _tri_inverse(aab, cs, reverse)
    akv = [_mm(aak[p], vv[p], NN) for p in tl]
    wu = [_mm(tinv[p], cat([at[p], akv[p]], axis=1), NN) for p in tl]
    rhs = [cat([wu[p].astype(BF16), cat([zero_t, vv[p]], axis=1)], axis=0) for p in tl]
    qy = [_mm(aq[p], rhs[p], NN) for p in tl]
    mn = [_mm(cat([bh[p], kh[p]], axis=0), rhs[p], TN) for p in tl]
    lhs = []
    for p in tl:
        m_mat = jnp.where(eye_l, e_tot[:, p * LANES:(p + 1) * LANES], 0.0) + mn[p][:, :LANES]
        lhs.append(cat([m_mat, rt[p] + qy[p][:, :LANES]], axis=0))
    ms = [_mm(lhs[p], s_ref[p], NN) for p in tl]
    outs = []
    for p in tl:
        s_ref[p] = ms[p][:LANES] + mn[p][:, LANES:]
        ybd = ms[p][LANES:] + qy[p][:, LANES:]
        y = ybd[:cs]
        for h in range(1, hp):
            y = y + ybd[h * cs:(h + 1) * cs]
        outs.append(y)
    y_ref[0] = cat(outs, axis=1)


def _wkv_scan(r, v, kk, lw, kd, a, n, reverse):
    b, t, mix_b = r.shape
    heads = mix_b // n
    hg = SCAN_HEADS if heads % SCAN_HEADS == 0 else heads
    cs = SCAN_CHUNK
    nc = t // cs
    w = hg * n
    di = 1 if reverse else 0
    tchunk = (lambda ci: nc - 1 - ci) if reverse else (lambda ci: ci)
    shared = pl.BlockSpec((1, cs, w), lambda bi, hi, ci: (bi, tchunk(ci), hi))
    perdir = pl.BlockSpec((1, 1, cs, w), lambda bi, hi, ci: (di, bi, tchunk(ci), hi))
    return pl.pallas_call(
        functools.partial(_scan_body, n=n, reverse=reverse),
        out_shape=jax.ShapeDtypeStruct((b, t, mix_b), F32),
        grid=(b, heads // hg, nc),
        in_specs=[shared] * 3 + [perdir] * 3,
        out_specs=shared,
        scratch_shapes=[pltpu.VMEM((w // LANES, LANES, LANES), F32)],
        compiler_params=_params("parallel", "parallel", "arbitrary"),
        name="wkv_scan",
    )(r, v, kk, lw, kd, a)


def _post_body(yf_ref, yb_ref, bonus_ref, g_ref, gg_ref, gb_ref, j_ref, o_ref, *, n):
    y = yf_ref[...] + yb_ref[...]
    mean = _headsum(y, j_ref) * (1.0 / n)
    yc = y - mean
    var = _headsum(yc * yc, j_ref) * (1.0 / n)
    yn = yc * lax.rsqrt(var + GN_EPS) * gg_ref[...] + gb_ref[...]
    o_ref[...] = ((yn + bonus_ref[...]) * g_ref[...]).astype(o_ref.dtype)


def _rwkv_post(y_fwd, y_bwd, bonus, g, gn_g, gn_b, jmat, n):
    m, mix_b = y_fwd.shape
    tm = _pick(m, (256, 128, 64, 32, 16, 8))
    row = pl.BlockSpec((tm, mix_b), lambda i: (i, 0))
    vec = pl.BlockSpec((1, mix_b), lambda i: (0, 0))
    return pl.pallas_call(
        functools.partial(_post_body, n=n),
        out_shape=jax.ShapeDtypeStruct((m, mix_b), BF16),
        grid=(m // tm,),
        in_specs=[row, row, row, row, vec, vec, pl.BlockSpec(jmat.shape, lambda i: (0, 0))],
        out_specs=row,
        compiler_params=_params("parallel"),
        name="rwkv_post",
    )(y_fwd, y_bwd, bonus, g, gn_g.reshape(1, mix_b), gn_b.reshape(1, mix_b), jmat)


def _attn_body(q_ref, kv_ref, o_ref, *, heads):
    d = q_ref.shape[-1]
    hd = d // heads
    scale = hd ** -0.5
    for h in range(heads):
        q = q_ref[0, :, h * hd:(h + 1) * hd]
        k = kv_ref[0, :, h * hd:(h + 1) * hd]
        v = kv_ref[0, :, d + h * hd:d + (h + 1) * hd]
        s = lax.dot_general(q, k, NT, preferred_element_type=F32) * scale
        e = jnp.exp(s - jnp.max(s, axis=-1, keepdims=True))
        p = e / jnp.sum(e, axis=-1, keepdims=True)
        o_ref[0, :, h * hd:(h + 1) * hd] = _dot(p.astype(BF16), v).astype(o_ref.dtype)


def _attention(q, kv):
    b, t, d = q.shape
    n_mem = kv.shape[1]
    tq = _pick(t, (1024, 512, 256, 128, 64, 32, 16))
    return pl.pallas_call(
        functools.partial(_attn_body, heads=X_HEADS),
        out_shape=jax.ShapeDtypeStruct((b, t, d), BF16),
        grid=(b, t // tq),
        in_specs=[pl.BlockSpec((1, tq, d), lambda bi, i: (bi, i, 0)),
                  pl.BlockSpec((1, n_mem, 2 * d), lambda bi, i: (bi, 0, 0))],
        out_specs=pl.BlockSpec((1, tq, d), lambda bi, i: (bi, i, 0)),
        compiler_params=_params("parallel", "parallel"),
        name="cross_attention",
    )(q, kv)


def _rms(x, g):
    return x * lax.rsqrt(jnp.mean(x * x, axis=-1, keepdims=True) + RMS_EPS) * g


def _ffn_body(x_ref, xp_ref, xn_ref, gin_ref, gout_ref, wg_ref, wv_ref, cwg_ref, cwv_ref, cbg_ref,
              cbv_ref, wd_ref, o_ref, hx_ref):
    i = pl.program_id(1)
    f = pl.program_id(2)
    tm = x_ref.shape[1]
    halo = BF16_ROWS

    @pl.when(f == 0)
    def _():
        g = gin_ref[...]
        zero = jnp.zeros((halo, x_ref.shape[2]), BF16)
        hx_ref[0:halo, :] = jnp.where(i == 0, zero, _rms(xp_ref[0], g).astype(BF16))
        hx_ref[halo + tm:, :] = jnp.where(i == pl.num_programs(1) - 1, zero, _rms(xn_ref[0], g).astype(BF16))

        def norm_rows(c, carry):
            r0 = pl.multiple_of(c * NORM_ROWS, NORM_ROWS)
            hx_ref[pl.ds(halo + r0, NORM_ROWS), :] = _rms(x_ref[0, pl.ds(r0, NORM_ROWS), :], g).astype(BF16)
            o_ref[0, pl.ds(r0, NORM_ROWS), :] = jnp.zeros((NORM_ROWS, x_ref.shape[2]), F32)
            return carry

        lax.fori_loop(0, tm // NORM_ROWS, norm_rows, 0)

    hx = hx_ref[...]
    rows = tm + 2 * halo

    def conv(w_ref, cw_ref, cb_ref):
        z = _dot(hx, w_ref[...])
        prev = pltpu.roll(z, 1, 0)[halo:halo + tm]
        nxt = pltpu.roll(z, rows - 1, 0)[halo:halo + tm]
        cw = cw_ref[...]
        return cw[0:1] * prev + cw[1:2] * z[halo:halo + tm] + cw[2:3] * nxt + cb_ref[...]

    gate = conv(wg_ref, cwg_ref, cbg_ref)
    val = conv(wv_ref, cwv_ref, cbv_ref)
    act = (jax.nn.silu(gate) * val).astype(BF16)
    o_ref[0] += _dot(act, wd_ref[...])

    @pl.when(f == pl.num_programs(2) - 1)
    def _():
        g = gout_ref[...]

        def out_rows(c, carry):
            rs = pl.ds(pl.multiple_of(c * NORM_ROWS, NORM_ROWS), NORM_ROWS)
            o_ref[0, rs, :] = _rms(x_ref[0, rs, :] + o_ref[0, rs, :], g)
            return carry

        lax.fori_loop(0, tm // NORM_ROWS, out_rows, 0)


def _conv_ffn(x, g_in, g_out, w_up, conv_w, conv_b, w_down):
    b, t, d = x.shape
    d_ff = w_down.shape[0]
    tm = _pick(t, (1024, 512, 256, 128))
    tf = _pick(d_ff, (256, 128))
    nf = d_ff // tf
    hb = tm // BF16_ROWS
    nblk = t // BF16_ROWS
    cb = conv_b.reshape(1, 2 * d_ff)
    once = pl.Buffered(1)
    return pl.pallas_call(
        _ffn_body,
        out_shape=jax.ShapeDtypeStruct((b, t, d), F32),
        grid=(b, t // tm, nf),
        in_specs=[
            pl.BlockSpec((1, tm, d), lambda bi, i, f: (bi, i, 0), pipeline_mode=once),
            pl.BlockSpec((1, BF16_ROWS, d), lambda bi, i, f: (bi, jnp.maximum(i * hb - 1, 0), 0)),
            pl.BlockSpec((1, BF16_ROWS, d), lambda bi, i, f: (bi, jnp.minimum((i + 1) * hb, nblk - 1), 0)),
            pl.BlockSpec((1, d), lambda bi, i, f: (0, 0)),
            pl.BlockSpec((1, d), lambda bi, i, f: (0, 0)),
            pl.BlockSpec((d, tf), lambda bi, i, f: (0, f)),
            pl.BlockSpec((d, tf), lambda bi, i, f: (0, nf + f)),
            pl.BlockSpec((3, tf), lambda bi, i, f: (0, f)),
            pl.BlockSpec((3, tf), lambda bi, i, f: (0, nf + f)),
            pl.BlockSpec((1, tf), lambda bi, i, f: (0, f)),
            pl.BlockSpec((1, tf), lambda bi, i, f: (0, nf + f)),
            pl.BlockSpec((tf, d), lambda bi, i, f: (f, 0)),
        ],
        out_specs=pl.BlockSpec((1, tm, d), lambda bi, i, f: (bi, i, 0), pipeline_mode=once),
        scratch_shapes=[pltpu.VMEM((tm + 2 * BF16_ROWS, d), BF16)],
        compiler_params=_params("parallel", "parallel", "arbitrary"),
        name="conv_ffn",
    )(x, x, x, g_in.reshape(1, d), g_out.reshape(1, d), w_up, w_up, conv_w, conv_w, cb, cb, w_down)


def _trunk(x, mem, p):
    b, t, d = x.shape
    m = b * t
    n_mem = mem.shape[1]
    mix_a = p["ln_v_g"].shape[-1]
    mix_b = p["k_k"].shape[-1]
    n = p["r_k"].shape[-1]
    x2d = x.reshape(m, d)

    h1 = _rmsnorm([x2d], p["norm_mix"], BF16)
    za = _matmul([h1], p["w_in"], F32, col0=0, n=2 * mix_a)
    zrkv = _matmul([h1], p["w_in"], F32, col0=2 * mix_a, n=3 * mix_b)
    zl = _matmul([h1], p["w_in"], F32, col0=2 * mix_a + 3 * mix_b, n=p["w_in"].shape[1] - 2 * mix_a - 3 * mix_b)

    ya = _gmlp(za, p["ln_v_g"], p["ln_v_b"], p["w_s"], p["b_s"])

    r, v, kk, g, bonus, lw, kd, a = _rwkv_prep(
        zrkv.reshape(b, t, 3 * mix_b), zl.reshape(b, t, -1), p["mu_shift"], p["w0"], p["w_up_decay"],
        p["a0"], p["w_up_iclr"], p["w_up_gate"], p["k_k"], p["k_a"], p["r_k"], p["jmat"])
    y_fwd = _wkv_scan(r, v, kk, lw, kd, a, n, reverse=False)
    y_bwd = _wkv_scan(r, v, kk, lw, kd, a, n, reverse=True)
    yb = _rwkv_post(y_fwd.reshape(m, mix_b), y_bwd.reshape(m, mix_b), bonus.reshape(m, mix_b),
                    g.reshape(m, mix_b), p["gn_g"], p["gn_b"], p["jmat"], n)

    x1 = _matmul([ya, yb], p["w_out"], F32, residual=x2d)

    hq = _rmsnorm([x1], p["norm_x"], BF16)
    q = _matmul([hq], p["w_q"], BF16)
    memn = _rmsnorm([mem.reshape(b * n_mem, d)], p["norm_mem"], BF16)
    kv = _matmul([memn], p["w_kv"], BF16)
    o = _attention(q.reshape(b, t, d), kv.reshape(b, n_mem, 2 * d))
    x2 = _matmul([o.reshape(m, d)], p["w_o"], F32, residual=x1)

    return _conv_ffn(x2.reshape(b, t, d), p["norm_ffn"], p["norm_out"], p["w_ffn_up"], p["conv_w"],
                     p["conv_b"], p["w_ffn_down"])


def kernel(x_prompt, x_sample, mem_prompt, mem_sample, norm_mix, w_in, mu_shift, ln_v_g, ln_v_b, w_s, b_s, w0, w_up_decay, a0, w_up_iclr, w_up_gate, k_k, k_a, r_k, gn_g, gn_b, w_out, norm_x, norm_mem, w_q, w_kv, w_o, norm_ffn, w_ffn_up, conv_w, conv_b, w_ffn_down, norm_out):
    depth = w_in.shape[0]
    n = r_k.shape[-1]
    head_id = jnp.arange(LANES) // n
    jmat = (head_id[:, None] == head_id[None, :]).astype(BF16)

    layers = []
    for l in range(depth):
        layers.append(dict(
            norm_mix=norm_mix[l], w_in=w_in[l].astype(BF16), mu_shift=mu_shift[l],
            ln_v_g=ln_v_g[l], ln_v_b=ln_v_b[l], w_s=w_s[l], b_s=b_s[l],
            w0=w0[l], w_up_decay=w_up_decay[l], a0=a0[l], w_up_iclr=w_up_iclr[l], w_up_gate=w_up_gate[l],
            k_k=k_k[l], k_a=k_a[l], r_k=r_k[l], gn_g=gn_g[l], gn_b=gn_b[l],
            w_out=w_out[l].astype(BF16), norm_x=norm_x[l], norm_mem=norm_mem[l],
            w_q=w_q[l].astype(BF16), w_kv=w_kv[l].astype(BF16), w_o=w_o[l].astype(BF16),
            norm_ffn=norm_ffn[l], w_ffn_up=w_ffn_up[l].astype(BF16), conv_w=conv_w[l], conv_b=conv_b[l],
            w_ffn_down=w_ffn_down[l].astype(BF16), jmat=jmat))

    (p,) = layers
    p = dict(p, norm_out=norm_out)
    return _trunk(x_prompt, mem_prompt, p), _trunk(x_sample, mem_sample, p)
```

```python
import functools

import jax
import jax.numpy as jnp
from jax import lax
from jax.experimental import pallas as pl
from jax.experimental.pallas import tpu as pltpu

RMS_EPS = 1e-6
LN_EPS = 1e-5
GN_EPS = 64e-5
L2_EPS = 1e-12
LOGW_SCALE = 0.6065306597126334
X_HEADS = 4

LANES = 128
BF16_ROWS = 16
SCAN_CHUNK = 64
SCAN_HEADS = 32
NORM_ROWS = 128
INV_BASE = 8
VMEM_LIMIT = 60 * 1024 * 1024

F32 = jnp.float32
BF16 = jnp.bfloat16


def _params(*sem, flags=None):
    return pltpu.CompilerParams(dimension_semantics=sem, vmem_limit_bytes=VMEM_LIMIT, flags=flags)


def _pick(n, candidates):
    for c in candidates:
        if n % c == 0:
            return c
    raise ValueError(f"no tile for {n} in {candidates}")


def _dot(a, b):
    return jnp.dot(a, b, preferred_element_type=F32)


NN = (((1,), (0,)), ((), ()))
NT = (((1,), (1,)), ((), ()))
TN = (((0,), (0,)), ((), ()))


def _rmsnorm_body(*refs, n_in):
    g_ref, o_ref = refs[n_in], refs[n_in + 1]
    x = refs[0][...]
    for r in refs[1:n_in]:
        x = x + r[...]
    ms = jnp.mean(x * x, axis=-1, keepdims=True)
    o_ref[...] = (x * lax.rsqrt(ms + RMS_EPS) * g_ref[...]).astype(o_ref.dtype)


def _rmsnorm(xs, g, out_dtype):
    m, d = xs[0].shape
    tm = _pick(m, (256, 128, 64, 32, 16, 8))
    row = pl.BlockSpec((tm, d), lambda i: (i, 0))
    return pl.pallas_call(
        functools.partial(_rmsnorm_body, n_in=len(xs)),
        out_shape=jax.ShapeDtypeStruct((m, d), out_dtype),
        grid=(m // tm,),
        in_specs=[row] * len(xs) + [pl.BlockSpec((1, d), lambda i: (0, 0))],
        out_specs=row,
        compiler_params=_params("parallel"),
        name="rmsnorm",
    )(*xs, g.reshape(1, d))


def _mm_body(*refs, n_lhs, has_res):
    a_refs, w_refs = refs[:n_lhs], refs[n_lhs:2 * n_lhs]
    o_ref = refs[-1]
    acc = _dot(a_refs[0][...], w_refs[0][...])
    for a, w in zip(a_refs[1:], w_refs[1:]):
        acc = acc + _dot(a[...], w[...])
    if has_res:
        acc = acc + refs[2 * n_lhs][...]
    o_ref[...] = acc.astype(o_ref.dtype)


def _matmul(lhs, w, out_dtype, residual=None, col0=0, n=None):
    m = lhs[0].shape[0]
    n = w.shape[1] if n is None else n
    tm = _pick(m, (1024, 512, 256, 128, 64, 32, 16, 8))
    tn = _pick(n, (1024, 768, 512, 256, 128))
    if col0 % tn:
        w, col0 = w[:, col0:col0 + n], 0
    cb = col0 // tn
    in_specs = [pl.BlockSpec((tm, a.shape[1]), lambda i, j: (i, 0)) for a in lhs]
    in_specs += [pl.BlockSpec((a.shape[1], tn), lambda i, j, rb=rb: (rb, cb + j)) for rb, a in enumerate(lhs)]
    assert all(a.shape[1] == lhs[0].shape[1] for a in lhs) and len(lhs) * lhs[0].shape[1] == w.shape[0]
    args = list(lhs) + [w] * len(lhs)
    if residual is not None:
        in_specs.append(pl.BlockSpec((tm, tn), lambda i, j: (i, j)))
        args.append(residual)
    return pl.pallas_call(
        functools.partial(_mm_body, n_lhs=len(lhs), has_res=residual is not None),
        out_shape=jax.ShapeDtypeStruct((m, n), out_dtype),
        grid=(m // tm, n // tn),
        in_specs=in_specs,
        out_specs=pl.BlockSpec((tm, tn), lambda i, j: (i, j)),
        compiler_params=_params("parallel", "parallel"),
        name="matmul",
    )(*args)


def _gmlp_body(zu_ref, zv_ref, g_ref, b_ref, ws_ref, bs_ref, o_ref, *, chunk):
    u = jax.nn.gelu(zu_ref[...])
    v = jax.nn.gelu(zv_ref[...])
    mu = jnp.mean(v, axis=-1, keepdims=True)
    var = jnp.mean(jnp.square(v - mu), axis=-1, keepdims=True)
    vn = ((v - mu) * lax.rsqrt(var + LN_EPS) * g_ref[...] + b_ref[...]).astype(BF16)
    rows, width = u.shape
    heads = ws_ref.shape[0]
    hd = width // heads
    bs = bs_ref[...]
    for c in range(rows // chunk):
        rs = slice(c * chunk, (c + 1) * chunk)
        mixed = jnp.concatenate(
            [_dot(ws_ref[h], vn[rs, h * hd:(h + 1) * hd]) for h in range(heads)], axis=1)
        o_ref[rs, :] = (u[rs, :] * (mixed + bs)).astype(o_ref.dtype)


def _gmlp(za, ln_g, ln_b, w_s, b_s):
    m, two_a = za.shape
    mix_a = two_a // 2
    heads, chunk, _ = w_s.shape
    rows = _pick(m, (2 * chunk, chunk))
    bs_full = jnp.repeat(b_s.T, mix_a // heads, axis=1)
    vec = pl.BlockSpec((1, mix_a), lambda i: (0, 0))
    return pl.pallas_call(
        functools.partial(_gmlp_body, chunk=chunk),
        out_shape=jax.ShapeDtypeStruct((m, mix_a), BF16),
        grid=(m // rows,),
        in_specs=[pl.BlockSpec((rows, mix_a), lambda i: (i, 0)),
                  pl.BlockSpec((rows, mix_a), lambda i: (i, 1)),
                  vec, vec,
                  pl.BlockSpec((heads, chunk, chunk), lambda i: (0, 0, 0)),
                  pl.BlockSpec((chunk, mix_a), lambda i: (0, 0))],
        out_specs=pl.BlockSpec((rows, mix_a), lambda i: (i, 0)),
        compiler_params=_params("parallel"),
        name="gmlp",
    )(za, za, ln_g.reshape(1, mix_a), ln_b.reshape(1, mix_a), w_s.astype(BF16), bs_full)


def _headsum(x, j_ref):
    tm, w = x.shape
    nb = w // LANES
    xs = jnp.concatenate([x[:, i * LANES:(i + 1) * LANES] for i in range(nb)], axis=0)
    s = _dot(xs.astype(BF16), j_ref[...])
    return jnp.concatenate([s[i * tm:(i + 1) * tm] for i in range(nb)], axis=1)


def _tshift(z, zp, zn, mu, first, last):
    tm = z.shape[0]
    rows = lax.broadcasted_iota(jnp.int32, (8, 1), 0)
    prev_row = jnp.where(first, 0.0, zp[7:8, :])
    next_row = jnp.where(last, 0.0, zn[0:1, :])
    prev = pltpu.roll(z, 1, 0)
    nxt = pltpu.roll(z, tm - 1, 0)
    prev = jnp.concatenate([jnp.where(rows == 0, prev_row, prev[:8]), prev[8:]], axis=0)
    nxt = jnp.concatenate([nxt[:tm - 8], jnp.where(rows == 7, next_row, nxt[tm - 8:])], axis=0)
    return z * (1.0 - mu) + (0.5 * mu) * (prev + nxt)


def _prep_body(zr, zrp, zrn, zk, zkp, zkn, zv, zvp, zvn, zl, zlp, zln,
               mur, muk, muv, mul, w0_ref, wd_ref, a0_ref, wi_ref, wg_ref,
               kk_ref, ka_ref, rk_ref, j_ref,
               r_out, v_out, kk_out, g_out, bonus_out, lw_out, kd_out, a_out,
               *, lora_w, lora_a):
    i = pl.program_id(1)
    first = i == 0
    last = i == pl.num_programs(1) - 1
    r = _tshift(zr[0], zrp[0], zrn[0], mur[...], first, last)
    k = _tshift(zk[0], zkp[0], zkn[0], muk[...], first, last)
    v = _tshift(zv[0], zvp[0], zvn[0], muv[...], first, last)
    lo = _tshift(zl[0], zlp[0], zln[0], mul[...], first, last)
    xw = lo[:, :lora_w]
    xa = lo[:, lora_w:lora_w + lora_a]
    xg = lo[:, lora_w + lora_a:]

    g_out[0] = _dot(jax.nn.sigmoid(xg).astype(BF16), wg_ref[...]).astype(g_out.dtype)
    kkr = k * kk_ref[...]
    ss = _headsum(kkr * kkr, j_ref)
    kk = kkr * lax.rsqrt(jnp.maximum(ss, L2_EPS))
    tw = jnp.tanh(xw).astype(BF16)
    xab = xa.astype(BF16)
    rk = None
    for d in range(2):
        lw_out[d, 0] = -LOGW_SCALE * jax.nn.sigmoid(w0_ref[d:d + 1, :] + _dot(tw, wd_ref[d]))
        a = jax.nn.sigmoid(a0_ref[d:d + 1, :] + _dot(xab, wi_ref[d]))
        kd = k * (1.0 + (a - 1.0) * ka_ref[...])
        a_out[d, 0] = a.astype(a_out.dtype)
        kd_out[d, 0] = kd.astype(kd_out.dtype)
        s = _headsum(r * kd * rk_ref[...], j_ref)
        rk = s if rk is None else rk + s
    r_out[0] = r.astype(r_out.dtype)
    v_out[0] = v.astype(v_out.dtype)
    kk_out[0] = kk.astype(kk_out.dtype)
    bonus_out[0] = (rk * v).astype(bonus_out.dtype)


def _rwkv_prep(zrkv, zl, mu, w0, w_up_decay, a0, w_up_iclr, w_up_gate, k_k, k_a, r_k, jmat):
    b, t, three_b = zrkv.shape
    mix_b = three_b // 3
    nl = zl.shape[-1]
    lora_w, lora_a = w_up_decay.shape[1], w_up_iclr.shape[1]
    tm = _pick(t, (128, 64, 32, 16, 8))
    hb = tm // 8
    nblk8 = t // 8

    def main(w, col):
        return pl.BlockSpec((1, tm, w), lambda bi, i: (bi, i, col))

    def prev(w, col):
        return pl.BlockSpec((1, 8, w), lambda bi, i: (bi, jnp.maximum(i * hb - 1, 0), col))

    def nxt(w, col):
        return pl.BlockSpec((1, 8, w), lambda bi, i: (bi, jnp.minimum((i + 1) * hb, nblk8 - 1), col))

    def const(shape):
        return pl.BlockSpec(shape, lambda bi, i: (0,) * len(shape))

    in_specs, args = [], []
    for col in range(3):
        in_specs += [main(mix_b, col), prev(mix_b, col), nxt(mix_b, col)]
        args += [zrkv] * 3
    in_specs += [main(nl, 0), prev(nl, 0), nxt(nl, 0)]
    args += [zl] * 3
    mu2 = mu.reshape(1, -1)
    in_specs += [pl.BlockSpec((1, mix_b), lambda bi, i, c=c: (0, c)) for c in range(3)]
    args += [mu2[:, :three_b]] * 3
    in_specs.append(const((1, nl)))
    args.append(mu2[:, three_b:])
    consts = [w0, w_up_decay.astype(BF16), a0, w_up_iclr.astype(BF16), w_up_gate.astype(BF16),
              k_k.reshape(1, mix_b), k_a.reshape(1, mix_b), r_k.reshape(1, mix_b), jmat]
    in_specs += [const(c.shape) for c in consts]
    args += consts

    shared = jax.ShapeDtypeStruct((b, t, mix_b), BF16)
    perdir = lambda dt: jax.ShapeDtypeStruct((2, b, t, mix_b), dt)
    o_shared = pl.BlockSpec((1, tm, mix_b), lambda bi, i: (bi, i, 0))
    o_perdir = pl.BlockSpec((2, 1, tm, mix_b), lambda bi, i: (0, bi, i, 0))
    return pl.pallas_call(
        functools.partial(_prep_body, lora_w=lora_w, lora_a=lora_a),
        out_shape=[shared] * 5 + [perdir(F32), perdir(BF16), perdir(BF16)],
        grid=(b, t // tm),
        in_specs=in_specs,
        out_specs=[o_shared] * 5 + [o_perdir] * 3,
        compiler_params=_params("parallel", "parallel"),
        name="rwkv_prep",
    )(*args)


def _mm(a, b, dims):
    return lax.dot_general(a.astype(BF16), b.astype(BF16), dims, preferred_element_type=F32)


def _halves(x, size):
    blocks = x.shape[0] // (2 * size)
    lo = jnp.concatenate([x[2 * j * size:(2 * j + 1) * size] for j in range(blocks)], axis=0)
    hi = jnp.concatenate([x[(2 * j + 1) * size:(2 * j + 2) * size] for j in range(blocks)], axis=0)
    return lo, hi


def _interleave(lo, hi, size):
    pieces = []
    for j in range(lo.shape[0] // size):
        pieces += [lo[j * size:(j + 1) * size], hi[j * size:(j + 1) * size]]
    return jnp.concatenate(pieces, axis=0)


def _tri_inverse(ls, limit, reverse):
    rows = ls[0].shape[0]
    row = lax.broadcasted_iota(jnp.int32, (rows, rows), 0)
    col = lax.broadcasted_iota(jnp.int32, (rows, rows), 1)
    eye = (row == col).astype(F32)
    sh = INV_BASE.bit_length() - 1
    base = (row >> sh) == (col >> sh)
    ps = [jnp.where(base, l, 0.0) for l in ls]
    ts = [jnp.where(row == col, 1.0, p) for p in ps]
    size = 2
    while size < INV_BASE:
        ps = [_mm(p, p, NN) for p in ps]
        ts = [_mm(t, eye + p, NN) for t, p in zip(ts, ps)]
        size *= 2
    size = INV_BASE
    hrow = lax.broadcasted_iota(jnp.int32, (rows // 2, rows), 0)
    hcol = lax.broadcasted_iota(jnp.int32, (rows // 2, rows), 1)
    zero = jnp.zeros((rows // 2, rows), F32)
    while size < limit:
        sh = size.bit_length() - 1
        partner = ((hcol >> (sh + 1)) == (hrow >> sh)) & (((hcol >> sh) & 1) == (1 if reverse else 0))
        act = 0 if reverse else 1
        l_act = [_halves(l, size)[act] for l in ls]
        t_halves = [_halves(t, size) for t in ts]
        xs = [_mm(jnp.where(partner, la, 0.0), t, NN) for la, t in zip(l_act, ts)]
        xs = [_interleave(x, zero, size) if reverse else _interleave(zero, x, size) for x in xs]
        upd = [th[act] + _mm(th[act], x, NN) for th, x in zip(t_halves, xs)]
        ts = [_interleave(u, th[1], size) if reverse else _interleave(th[0], u, size)
              for th, u in zip(t_halves, upd)]
        size *= 2
    return ts


def _scan_body(r_ref, v_ref, kk_ref, lw_ref, kd_ref, a_ref, y_ref, s_ref, *, n, reverse):
    c = pl.program_id(2)
    cs = r_ref.shape[1]

    @pl.when(c == 0)
    def _():
        s_ref[...] = jnp.zeros_like(s_ref)

    sgn = -1 if reverse else 1
    tiles = r_ref.shape[2] // LANES
    hp = LANES // n
    rows = hp * cs

    lw = lw_ref[0, 0]
    rc = lax.broadcasted_iota(jnp.int32, (cs, cs), 0)
    cc = lax.broadcasted_iota(jnp.int32, (cs, cs), 1)
    tri = ((rc - cc) * sgn >= 0).astype(BF16)
    l_hi = lw.astype(BF16)
    l_mid = (lw - l_hi.astype(F32)).astype(BF16)
    l_lo = (lw - l_hi.astype(F32) - l_mid.astype(F32)).astype(BF16)
    cum = _dot(tri, l_hi) + (_dot(tri, l_mid) + _dot(tri, l_lo))
    tot = cum[0:1, :] if reverse else cum[cs - 1:cs, :]
    e_tot = jnp.exp(tot)

    row = lax.broadcasted_iota(jnp.int32, (rows, rows), 0)
    col = lax.broadcasted_iota(jnp.int32, (rows, rows), 1)
    order = ((row & (cs - 1)) - (col & (cs - 1))) * sgn
    strict = order > 0
    incl = order >= 0
    eye_l = (lax.broadcasted_iota(jnp.int32, (LANES, LANES), 0)
             == lax.broadcasted_iota(jnp.int32, (LANES, LANES), 1))
    lane_head = lax.broadcasted_iota(jnp.int32, (cs, LANES), 1) // n
    head_masks = [lane_head == h for h in range(hp)]
    zero_t = jnp.zeros((rows, LANES), BF16)

    def pack(x):
        return jnp.concatenate([jnp.where(mk, x, jnp.zeros_like(x)) for mk in head_masks], axis=0)

    tl = range(tiles)
    cat = jnp.concatenate
    at, rt, bh, kh, vv, aa = [], [], [], [], [], []
    for p in tl:
        sl = slice(p * LANES, (p + 1) * LANES)
        cum_p, lw_p, tot_p = cum[:, sl], lw[:, sl], tot[:, sl]
        kk_p = kk_ref[0, :, sl].astype(F32)
        kd_p = kd_ref[0, 0, :, sl].astype(F32)
        b_p = kk_p * a_ref[0, 0, :, sl].astype(F32)
        e_neg = jnp.exp(-cum_p)
        e_hat = jnp.exp(tot_p - cum_p)
        at.append(pack(-kk_p * jnp.exp(cum_p - lw_p)))
        rt.append(pack(r_ref[0, :, sl].astype(F32) * jnp.exp(cum_p)))
        bh.append(pack((b_p * e_hat).astype(BF16)))
        kh.append(pack((kd_p * e_hat).astype(BF16)))
        vv.append(pack(v_ref[0, :, sl].astype(BF16)))
        bk = cat([pack((b_p * e_neg).astype(BF16)), pack((kd_p * e_neg).astype(BF16))], axis=0)
        aa.append(_mm(cat([at[p], rt[p]], axis=0), bk, NT).astype(BF16))
    aab = [jnp.where(strict, x[:rows, :rows], 0.0) for x in aa]
    aak = [jnp.where(strict, x[:rows, rows:], 0.0) for x in aa]
    aq = [cat([jnp.where(incl, x[rows:, :rows], 0.0), jnp.where(incl, x[rows:, rows:], 0.0)], axis=1)
          for x in aa]
    tinv = _tri_inverse(aab, cs, reverse)
    akv = [_mm(aak[p], vv[p], NN) for p in tl]
    wu = [_mm(tinv[p], cat([at[p], akv[p]], axis=1), NN) for p in tl]
    rhs = [cat([wu[p].astype(BF16), cat([zero_t, vv[p]], axis=1)], axis=0) for p in tl]
    qy = [_mm(aq[p], rhs[p], NN) for p in tl]
    mn = [_mm(cat([bh[p], kh[p]], axis=0), rhs[p], TN) for p in tl]
    lhs = []
    for p in tl:
        m_mat = jnp.where(eye_l, e_tot[:, p * LANES:(p + 1) * LANES], 0.0) + mn[p][:, :LANES]
        lhs.append(cat([m_mat, rt[p] + qy[p][:, :LANES]], axis=0))
    ms = [_mm(lhs[p], s_ref[p], NN) for p in tl]
    outs = []
    for p in tl:
        s_ref[p] = ms[p][:LANES] + mn[p][:, LANES:]
        ybd = ms[p][LANES:] + qy[p][:, LANES:]
        y = ybd[:cs]
        for h in range(1, hp):
            y = y + ybd[h * cs:(h + 1) * cs]
        outs.append(y)
    y_ref[0] = cat(outs, axis=1)


def _wkv_scan(r, v, kk, lw, kd, a, n, reverse):
    b, t, mix_b = r.shape
    heads = mix_b // n
    hg = SCAN_HEADS if heads % SCAN_HEADS == 0 else heads
    cs = SCAN_CHUNK
    nc = t // cs
    w = hg * n
    di = 1 if reverse else 0
    tchunk = (lambda ci: nc - 1 - ci) if reverse else (lambda ci: ci)
    shared = pl.BlockSpec((1, cs, w), lambda bi, hi, ci: (bi, tchunk(ci), hi))
    perdir = pl.BlockSpec((1, 1, cs, w), lambda bi, hi, ci: (di, bi, tchunk(ci), hi))
    return pl.pallas_call(
        functools.partial(_scan_body, n=n, reverse=reverse),
        out_shape=jax.ShapeDtypeStruct((b, t, mix_b), F32),
        grid=(b, heads // hg, nc),
        in_specs=[shared] * 3 + [perdir] * 3,
        out_specs=shared,
        scratch_shapes=[pltpu.VMEM((w // LANES, LANES, LANES), F32)],
        compiler_params=_params("parallel", "parallel", "arbitrary"),
        name="wkv_scan",
    )(r, v, kk, lw, kd, a)


def _post_body(yf_ref, yb_ref, bonus_ref, g_ref, gg_ref, gb_ref, j_ref, o_ref, *, n):
    y = yf_ref[...] + yb_ref[...]
    mean = _headsum(y, j_ref) * (1.0 / n)
    yc = y - mean
    var = _headsum(yc * yc, j_ref) * (1.0 / n)
    yn = yc * lax.rsqrt(var + GN_EPS) * gg_ref[...] + gb_ref[...]
    o_ref[...] = ((yn + bonus_ref[...]) * g_ref[...]).astype(o_ref.dtype)


def _rwkv_post(y_fwd, y_bwd, bonus, g, gn_g, gn_b, jmat, n):
    m, mix_b = y_fwd.shape
    tm = _pick(m, (256, 128, 64, 32, 16, 8))
    row = pl.BlockSpec((tm, mix_b), lambda i: (i, 0))
    vec = pl.BlockSpec((1, mix_b), lambda i: (0, 0))
    return pl.pallas_call(
        functools.partial(_post_body, n=n),
        out_shape=jax.ShapeDtypeStruct((m, mix_b), BF16),
        grid=(m // tm,),
        in_specs=[row, row, row, row, vec, vec, pl.BlockSpec(jmat.shape, lambda i: (0, 0))],
        out_specs=row,
        compiler_params=_params("parallel"),
        name="rwkv_post",
    )(y_fwd, y_bwd, bonus, g, gn_g.reshape(1, mix_b), gn_b.reshape(1, mix_b), jmat)


def _attn_body(q_ref, kv_ref, o_ref, *, heads):
    d = q_ref.shape[-1]
    hd = d // heads
    scale = hd ** -0.5
    for h in range(heads):
        q = q_ref[0, :, h * hd:(h + 1) * hd]
        k = kv_ref[0, :, h * hd:(h + 1) * hd]
        v = kv_ref[0, :, d + h * hd:d + (h + 1) * hd]
        s = lax.dot_general(q, k, NT, preferred_element_type=F32) * scale
        e = jnp.exp(s - jnp.max(s, axis=-1, keepdims=True))
        p = e / jnp.sum(e, axis=-1, keepdims=True)
        o_ref[0, :, h * hd:(h + 1) * hd] = _dot(p.astype(BF16), v).astype(o_ref.dtype)


def _attention(q, kv):
    b, t, d = q.shape
    n_mem = kv.shape[1]
    tq = _pick(t, (1024, 512, 256, 128, 64, 32, 16))
    return pl.pallas_call(
        functools.partial(_attn_body, heads=X_HEADS),
        out_shape=jax.ShapeDtypeStruct((b, t, d), BF16),
        grid=(b, t // tq),
        in_specs=[pl.BlockSpec((1, tq, d), lambda bi, i: (bi, i, 0)),
                  pl.BlockSpec((1, n_mem, 2 * d), lambda bi, i: (bi, 0, 0))],
        out_specs=pl.BlockSpec((1, tq, d), lambda bi, i: (bi, i, 0)),
        compiler_params=_params("parallel", "parallel"),
        name="cross_attention",
    )(q, kv)


def _rms(x, g):
    return x * lax.rsqrt(jnp.mean(x * x, axis=-1, keepdims=True) + RMS_EPS) * g


def _ffn_body(x_ref, xp_ref, xn_ref, gin_ref, gout_ref, wg_ref, wv_ref, cwg_ref, cwv_ref, cbg_ref,
              cbv_ref, wd_ref, o_ref, hx_ref):
    i = pl.program_id(1)
    f = pl.program_id(2)
    tm = x_ref.shape[1]
    halo = BF16_ROWS

    @pl.when(f == 0)
    def _():
        g = gin_ref[...]
        zero = jnp.zeros((halo, x_ref.shape[2]), BF16)
        hx_ref[0:halo, :] = jnp.where(i == 0, zero, _rms(xp_ref[0], g).astype(BF16))
        hx_ref[halo + tm:, :] = jnp.where(i == pl.num_programs(1) - 1, zero, _rms(xn_ref[0], g).astype(BF16))

        def norm_rows(c, carry):
            r0 = pl.multiple_of(c * NORM_ROWS, NORM_ROWS)
            hx_ref[pl.ds(halo + r0, NORM_ROWS), :] = _rms(x_ref[0, pl.ds(r0, NORM_ROWS), :], g).astype(BF16)
            o_ref[0, pl.ds(r0, NORM_ROWS), :] = jnp.zeros((NORM_ROWS, x_ref.shape[2]), F32)
            return carry

        lax.fori_loop(0, tm // NORM_ROWS, norm_rows, 0)

    hx = hx_ref[...]
    rows = tm + 2 * halo

    def conv(w_ref, cw_ref, cb_ref):
        z = _dot(hx, w_ref[...])
        prev = pltpu.roll(z, 1, 0)[halo:halo + tm]
        nxt = pltpu.roll(z, rows - 1, 0)[halo:halo + tm]
        cw = cw_ref[...]
        return cw[0:1] * prev + cw[1:2] * z[halo:halo + tm] + cw[2:3] * nxt + cb_ref[...]

    gate = conv(wg_ref, cwg_ref, cbg_ref)
    val = conv(wv_ref, cwv_ref, cbv_ref)
    act = (jax.nn.silu(gate) * val).astype(BF16)
    o_ref[0] += _dot(act, wd_ref[...])

    @pl.when(f == pl.num_programs(2) - 1)
    def _():
        g = gout_ref[...]

        def out_rows(c, carry):
            rs = pl.ds(pl.multiple_of(c * NORM_ROWS, NORM_ROWS), NORM_ROWS)
            o_ref[0, rs, :] = _rms(x_ref[0, rs, :] + o_ref[0, rs, :], g)
            return carry

        lax.fori_loop(0, tm // NORM_ROWS, out_rows, 0)


def _conv_ffn(x, g_in, g_out, w_up, conv_w, conv_b, w_down):
    b, t, d = x.shape
    d_ff = w_down.shape[0]
    tm = _pick(t, (1024, 512, 256, 128))
    tf = _pick(d_ff, (256, 128))
    nf = d_ff // tf
    hb = tm // BF16_ROWS
    nblk = t // BF16_ROWS
    cb = conv_b.reshape(1, 2 * d_ff)
    once = pl.Buffered(1)
    return pl.pallas_call(
        _ffn_body,
        out_shape=jax.ShapeDtypeStruct((b, t, d), F32),
        grid=(b, t // tm, nf),
        in_specs=[
            pl.BlockSpec((1, tm, d), lambda bi, i, f: (bi, i, 0), pipeline_mode=once),
            pl.BlockSpec((1, BF16_ROWS, d), lambda bi, i, f: (bi, jnp.maximum(i * hb - 1, 0), 0)),
            pl.BlockSpec((1, BF16_ROWS, d), lambda bi, i, f: (bi, jnp.minimum((i + 1) * hb, nblk - 1), 0)),
            pl.BlockSpec((1, d), lambda bi, i, f: (0, 0)),
            pl.BlockSpec((1, d), lambda bi, i, f: (0, 0)),
            pl.BlockSpec((d, tf), lambda bi, i, f: (0, f)),
            pl.BlockSpec((d, tf), lambda bi, i, f: (0, nf + f)),
            pl.BlockSpec((3, tf), lambda bi, i, f: (0, f)),
            pl.BlockSpec((3, tf), lambda bi, i, f: (0, nf + f)),
            pl.BlockSpec((1, tf), lambda bi, i, f: (0, f)),
            pl.BlockSpec((1, tf), lambda bi, i, f: (0, nf + f)),
            pl.BlockSpec((tf, d), lambda bi, i, f: (f, 0)),
        ],
        out_specs=pl.BlockSpec((1, tm, d), lambda bi, i, f: (bi, i, 0), pipeline_mode=once),
        scratch_shapes=[pltpu.VMEM((tm + 2 * BF16_ROWS, d), BF16)],
        compiler_params=_params("parallel", "parallel", "arbitrary"),
        name="conv_ffn",
    )(x, x, x, g_in.reshape(1, d), g_out.reshape(1, d), w_up, w_up, conv_w, conv_w, cb, cb, w_down)


def _trunk(x, mem, p):
    b, t, d = x.shape
    m = b * t
    n_mem = mem.shape[1]
    mix_a = p["ln_v_g"].shape[-1]
    mix_b = p["k_k"].shape[-1]
    n = p["r_k"].shape[-1]
    x2d = x.reshape(m, d)

    h1 = _rmsnorm([x2d], p["norm_mix"], BF16)
    za = _matmul([h1], p["w_in"], F32, col0=0, n=2 * mix_a)
    zrkv = _matmul([h1], p["w_in"], F32, col0=2 * mix_a, n=3 * mix_b)
    zl = _matmul([h1], p["w_in"], F32, col0=2 * mix_a + 3 * mix_b, n=p["w_in"].shape[1] - 2 * mix_a - 3 * mix_b)

    ya = _gmlp(za, p["ln_v_g"], p["ln_v_b"], p["w_s"], p["b_s"])

    r, v, kk, g, bonus, lw, kd, a = _rwkv_prep(
        zrkv.reshape(b, t, 3 * mix_b), zl.reshape(b, t, -1), p["mu_shift"], p["w0"], p["w_up_decay"],
        p["a0"], p["w_up_iclr"], p["w_up_gate"], p["k_k"], p["k_a"], p["r_k"], p["jmat"])
    y_fwd = _wkv_scan(r, v, kk, lw, kd, a, n, reverse=False)
    y_bwd = _wkv_scan(r, v, kk, lw, kd, a, n, reverse=True)
    yb = _rwkv_post(y_fwd.reshape(m, mix_b), y_bwd.reshape(m, mix_b), bonus.reshape(m, mix_b),
                    g.reshape(m, mix_b), p["gn_g"], p["gn_b"], p["jmat"], n)

    x1 = _matmul([ya, yb], p["w_out"], F32, residual=x2d)

    hq = _rmsnorm([x1], p["norm_x"], BF16)
    q = _matmul([hq], p["w_q"], BF16)
    memn = _rmsnorm([mem.reshape(b * n_mem, d)], p["norm_mem"], BF16)
    kv = _matmul([memn], p["w_kv"], BF16)
    o = _attention(q.reshape(b, t, d), kv.reshape(b, n_mem, 2 * d))
    x2 = _matmul([o.reshape(m, d)], p["w_o"], F32, residual=x1)

    return _conv_ffn(x2.reshape(b, t, d), p["norm_ffn"], p["norm_out"], p["w_ffn_up"], p["conv_w"],
                     p["conv_b"], p["w_ffn_down"])


def kernel(x_prompt, x_sample, mem_prompt, mem_sample, norm_mix, w_in, mu_shift, ln_v_g, ln_v_b, w_s, b_s, w0, w_up_decay, a0, w_up_iclr, w_up_gate, k_k, k_a, r_k, gn_g, gn_b, w_out, norm_x, norm_mem, w_q, w_kv, w_o, norm_ffn, w_ffn_up, conv_w, conv_b, w_ffn_down, norm_out):
    depth = w_in.shape[0]
    n = r_k.shape[-1]
    head_id = jnp.arange(LANES) // n
    jmat = (head_id[:, None] == head_id[None, :]).astype(BF16)

    layers = []
    for l in range(depth):
        layers.append(dict(
            norm_mix=norm_mix[l], w_in=w_in[l].astype(BF16), mu_shift=mu_shift[l],
            ln_v_g=ln_v_g[l], ln_v_b=ln_v_b[l], w_s=w_s[l], b_s=b_s[l],
            w0=w0[l], w_up_decay=w_up_decay[l], a0=a0[l], w_up_iclr=w_up_iclr[l], w_up_gate=w_up_gate[l],
            k_k=k_k[l], k_a=k_a[l], r_k=r_k[l], gn_g=gn_g[l], gn_b=gn_b[l],
            w_out=w_out[l].astype(BF16), norm_x=norm_x[l], norm_mem=norm_mem[l],
            w_q=w_q[l].astype(BF16), w_kv=w_kv[l].astype(BF16), w_o=w_o[l].astype(BF16),
            norm_ffn=norm_ffn[l], w_ffn_up=w_ffn_up[l].astype(BF16), conv_w=conv_w[l], conv_b=conv_b[l],
            w_ffn_down=w_ffn_down[l].astype(BF16), jmat=jmat))

    (p,) = layers
    p = dict(p, norm_out=norm_out)
    return _trunk(x_prompt, mem_prompt, p), _trunk(x_sample, mem_sample, p)
```

```python
import functools

import jax
import jax.numpy as jnp
from jax import lax
from jax.experimental import pallas as pl
from jax.experimental.pallas import tpu as pltpu

RMS_EPS = 1e-6
LN_EPS = 1e-5
GN_EPS = 64e-5
L2_EPS = 1e-12
LOGW_SCALE = 0.6065306597126334
X_HEADS = 4

LANES = 128
BF16_ROWS = 16
SCAN_CHUNK = 64
SCAN_HEADS = 32
NORM_ROWS = 128
INV_BASE = 8
VMEM_LIMIT = 60 * 1024 * 1024

F32 = jnp.float32
BF16 = jnp.bfloat16


def _params(*sem, flags=None):
    return pltpu.CompilerParams(dimension_semantics=sem, vmem_limit_bytes=VMEM_LIMIT, flags=flags)


def _pick(n, candidates):
    for c in candidates:
        if n % c == 0:
            return c
    raise ValueError(f"no tile for {n} in {candidates}")


def _dot(a, b):
    return jnp.dot(a, b, preferred_element_type=F32)


NN = (((1,), (0,)), ((), ()))
NT = (((1,), (1,)), ((), ()))
TN = (((0,), (0,)), ((), ()))


def _rmsnorm_body(*refs, n_in):
    g_ref, o_ref = refs[n_in], refs[n_in + 1]
    x = refs[0][...]
    for r in refs[1:n_in]:
        x = x + r[...]
    ms = jnp.mean(x * x, axis=-1, keepdims=True)
    o_ref[...] = (x * lax.rsqrt(ms + RMS_EPS) * g_ref[...]).astype(o_ref.dtype)


def _rmsnorm(xs, g, out_dtype):
    m, d = xs[0].shape
    tm = _pick(m, (256, 128, 64, 32, 16, 8))
    row = pl.BlockSpec((tm, d), lambda i: (i, 0))
    return pl.pallas_call(
        functools.partial(_rmsnorm_body, n_in=len(xs)),
        out_shape=jax.ShapeDtypeStruct((m, d), out_dtype),
        grid=(m // tm,),
        in_specs=[row] * len(xs) + [pl.BlockSpec((1, d), lambda i: (0, 0))],
        out_specs=row,
        compiler_params=_params("parallel"),
        name="rmsnorm",
    )(*xs, g.reshape(1, d))


def _mm_body(*refs, n_lhs, has_res):
    a_refs, w_refs = refs[:n_lhs], refs[n_lhs:2 * n_lhs]
    o_ref = refs[-1]
    acc = _dot(a_refs[0][...], w_refs[0][...])
    for a, w in zip(a_refs[1:], w_refs[1:]):
        acc = acc + _dot(a[...], w[...])
    if has_res:
        acc = acc + refs[2 * n_lhs][...]
    o_ref[...] = acc.astype(o_ref.dtype)


def _matmul(lhs, w, out_dtype, residual=None, col0=0, n=None):
    m = lhs[0].shape[0]
    n = w.shape[1] if n is None else n
    tm = _pick(m, (1024, 512, 256, 128, 64, 32, 16, 8))
    tn = _pick(n, (1024, 768, 512, 256, 128))
    if col0 % tn:
        w, col0 = w[:, col0:col0 + n], 0
    cb = col0 // tn
    in_specs = [pl.BlockSpec((tm, a.shape[1]), lambda i, j: (i, 0)) for a in lhs]
    in_specs += [pl.BlockSpec((a.shape[1], tn), lambda i, j, rb=rb: (rb, cb + j)) for rb, a in enumerate(lhs)]
    assert all(a.shape[1] == lhs[0].shape[1] for a in lhs) and len(lhs) * lhs[0].shape[1] == w.shape[0]
    args = list(lhs) + [w] * len(lhs)
    if residual is not None:
        in_specs.append(pl.BlockSpec((tm, tn), lambda i, j: (i, j)))
        args.append(residual)
    return pl.pallas_call(
        functools.partial(_mm_body, n_lhs=len(lhs), has_res=residual is not None),
        out_shape=jax.ShapeDtypeStruct((m, n), out_dtype),
        grid=(m // tm, n // tn),
        in_specs=in_specs,
        out_specs=pl.BlockSpec((tm, tn), lambda i, j: (i, j)),
        compiler_params=_params("parallel", "parallel"),
        name="matmul",
    )(*args)


def _gmlp_body(zu_ref, zv_ref, g_ref, b_ref, ws_ref, bs_ref, o_ref, *, chunk):
    u = jax.nn.gelu(zu_ref[...])
    v = jax.nn.gelu(zv_ref[...])
    mu = jnp.mean(v, axis=-1, keepdims=True)
    var = jnp.mean(jnp.square(v - mu), axis=-1, keepdims=True)
    vn = ((v - mu) * lax.rsqrt(var + LN_EPS) * g_ref[...] + b_ref[...]).astype(BF16)
    rows, width = u.shape
    heads = ws_ref.shape[0]
    hd = width // heads
    bs = bs_ref[...]
    for c in range(rows // chunk):
        rs = slice(c * chunk, (c + 1) * chunk)
        mixed = jnp.concatenate(
            [_dot(ws_ref[h], vn[rs, h * hd:(h + 1) * hd]) for h in range(heads)], axis=1)
        o_ref[rs, :] = (u[rs, :] * (mixed + bs)).astype(o_ref.dtype)


def _gmlp(za, ln_g, ln_b, w_s, b_s):
    m, two_a = za.shape
    mix_a = two_a // 2
    heads, chunk, _ = w_s.shape
    rows = _pick(m, (2 * chunk, chunk))
    bs_full = jnp.repeat(b_s.T, mix_a // heads, axis=1)
    vec = pl.BlockSpec((1, mix_a), lambda i: (0, 0))
    return pl.pallas_call(
        functools.partial(_gmlp_body, chunk=chunk),
        out_shape=jax.ShapeDtypeStruct((m, mix_a), BF16),
        grid=(m // rows,),
        in_specs=[pl.BlockSpec((rows, mix_a), lambda i: (i, 0)),
                  pl.BlockSpec((rows, mix_a), lambda i: (i, 1)),
                  vec, vec,
                  pl.BlockSpec((heads, chunk, chunk), lambda i: (0, 0, 0)),
                  pl.BlockSpec((chunk, mix_a), lambda i: (0, 0))],
        out_specs=pl.BlockSpec((rows, mix_a), lambda i: (i, 0)),
        compiler_params=_params("parallel"),
        name="gmlp",
    )(za, za, ln_g.reshape(1, mix_a), ln_b.reshape(1, mix_a), w_s.astype(BF16), bs_full)


def _headsum(x, j_ref):
    tm, w = x.shape
    nb = w // LANES
    xs = jnp.concatenate([x[:, i * LANES:(i + 1) * LANES] for i in range(nb)], axis=0)
    s = _dot(xs.astype(BF16), j_ref[...])
    return jnp.concatenate([s[i * tm:(i + 1) * tm] for i in range(nb)], axis=1)


def _tshift(z, zp, zn, mu, first, last):
    tm = z.shape[0]
    rows = lax.broadcasted_iota(jnp.int32, (8, 1), 0)
    prev_row = jnp.where(first, 0.0, zp[7:8, :])
    next_row = jnp.where(last, 0.0, zn[0:1, :])
    prev = pltpu.roll(z, 1, 0)
    nxt = pltpu.roll(z, tm - 1, 0)
    prev = jnp.concatenate([jnp.where(rows == 0, prev_row, prev[:8]), prev[8:]], axis=0)
    nxt = jnp.concatenate([nxt[:tm - 8], jnp.where(rows == 7, next_row, nxt[tm - 8:])], axis=0)
    return z * (1.0 - mu) + (0.5 * mu) * (prev + nxt)


def _prep_body(zr, zrp, zrn, zk, zkp, zkn, zv, zvp, zvn, zl, zlp, zln,
               mur, muk, muv, mul, w0_ref, wd_ref, a0_ref, wi_ref, wg_ref,
               kk_ref, ka_ref, rk_ref, j_ref,
               r_out, v_out, kk_out, g_out, bonus_out, lw_out, kd_out, a_out,
               *, lora_w, lora_a):
    i = pl.program_id(1)
    first = i == 0
    last = i == pl.num_programs(1) - 1
    r = _tshift(zr[0], zrp[0], zrn[0], mur[...], first, last)
    k = _tshift(zk[0], zkp[0], zkn[0], muk[...], first, last)
    v = _tshift(zv[0], zvp[0], zvn[0], muv[...], first, last)
    lo = _tshift(zl[0], zlp[0], zln[0], mul[...], first, last)
    xw = lo[:, :lora_w]
    xa = lo[:, lora_w:lora_w + lora_a]
    xg = lo[:, lora_w + lora_a:]

    g_out[0] = _dot(jax.nn.sigmoid(xg).astype(BF16), wg_ref[...]).astype(g_out.dtype)
    kkr = k * kk_ref[...]
    ss = _headsum(kkr * kkr, j_ref)
    kk = kkr * lax.rsqrt(jnp.maximum(ss, L2_EPS))
    tw = jnp.tanh(xw).astype(BF16)
    xab = xa.astype(BF16)
    rk = None
    for d in range(2):
        lw_out[d, 0] = -LOGW_SCALE * jax.nn.sigmoid(w0_ref[d:d + 1, :] + _dot(tw, wd_ref[d]))
        a = jax.nn.sigmoid(a0_ref[d:d + 1, :] + _dot(xab, wi_ref[d]))
        kd = k * (1.0 + (a - 1.0) * ka_ref[...])
        a_out[d, 0] = a.astype(a_out.dtype)
        kd_out[d, 0] = kd.astype(kd_out.dtype)
        s = _headsum(r * kd * rk_ref[...], j_ref)
        rk = s if rk is None else rk + s
    r_out[0] = r.astype(r_out.dtype)
    v_out[0] = v.astype(v_out.dtype)
    kk_out[0] = kk.astype(kk_out.dtype)
    bonus_out[0] = (rk * v).astype(bonus_out.dtype)


def _rwkv_prep(zrkv, zl, mu, w0, w_up_decay, a0, w_up_iclr, w_up_gate, k_k, k_a, r_k, jmat):
    b, t, three_b = zrkv.shape
    mix_b = three_b // 3
    nl = zl.shape[-1]
    lora_w, lora_a = w_up_decay.shape[1], w_up_iclr.shape[1]
    tm = _pick(t, (128, 64, 32, 16, 8))
    hb = tm // 8
    nblk8 = t // 8

    def main(w, col):
        return pl.BlockSpec((1, tm, w), lambda bi, i: (bi, i, col))

    def prev(w, col):
        return pl.BlockSpec((1, 8, w), lambda bi, i: (bi, jnp.maximum(i * hb - 1, 0), col))

    def nxt(w, col):
        return pl.BlockSpec((1, 8, w), lambda bi, i: (bi, jnp.minimum((i + 1) * hb, nblk8 - 1), col))

    def const(shape):
        return pl.BlockSpec(shape, lambda bi, i: (0,) * len(shape))

    in_specs, args = [], []
    for col in range(3):
        in_specs += [main(mix_b, col), prev(mix_b, col), nxt(mix_b, col)]
        args += [zrkv] * 3
    in_specs += [main(nl, 0), prev(nl, 0), nxt(nl, 0)]
    args += [zl] * 3
    mu2 = mu.reshape(1, -1)
    in_specs += [pl.BlockSpec((1, mix_b), lambda bi, i, c=c: (0, c)) for c in range(3)]
    args += [mu2[:, :three_b]] * 3
    in_specs.append(const((1, nl)))
    args.append(mu2[:, three_b:])
    consts = [w0, w_up_decay.astype(BF16), a0, w_up_iclr.astype(BF16), w_up_gate.astype(BF16),
              k_k.reshape(1, mix_b), k_a.reshape(1, mix_b), r_k.reshape(1, mix_b), jmat]
    in_specs += [const(c.shape) for c in consts]
    args += consts

    shared = jax.ShapeDtypeStruct((b, t, mix_b), BF16)
    perdir = lambda dt: jax.ShapeDtypeStruct((2, b, t, mix_b), dt)
    o_shared = pl.BlockSpec((1, tm, mix_b), lambda bi, i: (bi, i, 0))
    o_perdir = pl.BlockSpec((2, 1, tm, mix_b), lambda bi, i: (0, bi, i, 0))
    return pl.pallas_call(
        functools.partial(_prep_body, lora_w=lora_w, lora_a=lora_a),
        out_shape=[shared] * 5 + [perdir(F32), perdir(BF16), perdir(BF16)],
        grid=(b, t // tm),
        in_specs=in_specs,
        out_specs=[o_shared] * 5 + [o_perdir] * 3,
        compiler_params=_params("parallel", "parallel"),
        name="rwkv_prep",
    )(*args)


def _mm(a, b, dims):
    return lax.dot_general(a.astype(BF16), b.astype(BF16), dims, preferred_element_type=F32)


def _halves(x, size):
    blocks = x.shape[0] // (2 * size)
    lo = jnp.concatenate([x[2 * j * size:(2 * j + 1) * size] for j in range(blocks)], axis=0)
    hi = jnp.concatenate([x[(2 * j + 1) * size:(2 * j + 2) * size] for j in range(blocks)], axis=0)
    return lo, hi


def _interleave(lo, hi, size):
    pieces = []
    for j in range(lo.shape[0] // size):
        pieces += [lo[j * size:(j + 1) * size], hi[j * size:(j + 1) * size]]
    return jnp.concatenate(pieces, axis=0)


def _tri_inverse(ls, limit, reverse):
    rows = ls[0].shape[0]
    row = lax.broadcasted_iota(jnp.int32, (rows, rows), 0)
    col = lax.broadcasted_iota(jnp.int32, (rows, rows), 1)
    eye = (row == col).astype(F32)
    sh = INV_BASE.bit_length() - 1
    base = (row >> sh) == (col >> sh)
    ps = [jnp.where(base, l, 0.0) for l in ls]
    ts = [jnp.where(row == col, 1.0, p) for p in ps]
    size = 2
    while size < INV_BASE:
        ps = [_mm(p, p, NN) for p in ps]
        ts = [_mm(t, eye + p, NN) for t, p in zip(ts, ps)]
        size *= 2
    size = INV_BASE
    hrow = lax.broadcasted_iota(jnp.int32, (rows // 2, rows), 0)
    hcol = lax.broadcasted_iota(jnp.int32, (rows // 2, rows), 1)
    zero = jnp.zeros((rows // 2, rows), F32)
    while size < limit:
        sh = size.bit_length() - 1
        partner = ((hcol >> (sh + 1)) == (hrow >> sh)) & (((hcol >> sh) & 1) == (1 if reverse else 0))
        act = 0 if reverse else 1
        l_act = [_halves(l, size)[act] for l in ls]
        t_halves = [_halves(t, size) for t in ts]
        xs = [_mm(jnp.where(partner, la, 0.0), t, NN) for la, t in zip(l_act, ts)]
        xs = [_interleave(x, zero, size) if reverse else _interleave(zero, x, size) for x in xs]
        upd = [th[act] + _mm(th[act], x, NN) for th, x in zip(t_halves, xs)]
        ts = [_interleave(u, th[1], size) if reverse else _interleave(th[0], u, size)
              for th, u in zip(t_halves, upd)]
        size *= 2
    return ts


def _scan_body(r_ref, v_ref, kk_ref, lw_ref, kd_ref, a_ref, y_ref, s_ref, *, n, reverse):
    c = pl.program_id(2)
    cs = r_ref.shape[1]

    @pl.when(c == 0)
    def _():
        s_ref[...] = jnp.zeros_like(s_ref)

    sgn = -1 if reverse else 1
    tiles = r_ref.shape[2] // LANES
    hp = LANES // n
    rows = hp * cs

    lw = lw_ref[0, 0]
    rc = lax.broadcasted_iota(jnp.int32, (cs, cs), 0)
    cc = lax.broadcasted_iota(jnp.int32, (cs, cs), 1)
    tri = ((rc - cc) * sgn >= 0).astype(BF16)
    l_hi = lw.astype(BF16)
    l_mid = (lw - l_hi.astype(F32)).astype(BF16)
    l_lo = (lw - l_hi.astype(F32) - l_mid.astype(F32)).astype(BF16)
    cum = _dot(tri, l_hi) + (_dot(tri, l_mid) + _dot(tri, l_lo))
    tot = cum[0:1, :] if reverse else cum[cs - 1:cs, :]
    e_tot = jnp.exp(tot)

    row = lax.broadcasted_iota(jnp.int32, (rows, rows), 0)
    col = lax.broadcasted_iota(jnp.int32, (rows, rows), 1)
    order = ((row & (cs - 1)) - (col & (cs - 1))) * sgn
    strict = order > 0
    incl = order >= 0
    eye_l = (lax.broadcasted_iota(jnp.int32, (LANES, LANES), 0)
             == lax.broadcasted_iota(jnp.int32, (LANES, LANES), 1))
    lane_head = lax.broadcasted_iota(jnp.int32, (cs, LANES), 1) // n
    head_masks = [lane_head == h for h in range(hp)]
    zero_t = jnp.zeros((rows, LANES), BF16)

    def pack(x):
        return jnp.concatenate([jnp.where(mk, x, jnp.zeros_like(x)) for mk in head_masks], axis=0)

    tl = range(tiles)
    cat = jnp.concatenate
    at, rt, bh, kh, vv, aa = [], [], [], [], [], []
    for p in tl:
        sl = slice(p * LANES, (p + 1) * LANES)
        cum_p, lw_p, tot_p = cum[:, sl], lw[:, sl], tot[:, sl]
        kk_p = kk_ref[0, :, sl].astype(F32)
        kd_p = kd_ref[0, 0, :, sl].astype(F32)
        b_p = kk_p * a_ref[0, 0, :, sl].astype(F32)
        e_neg = jnp.exp(-cum_p)
        e_hat = jnp.exp(tot_p - cum_p)
        at.append(pack(-kk_p * jnp.exp(cum_p - lw_p)))
        rt.append(pack(r_ref[0, :, sl].astype(F32) * jnp.exp(cum_p)))
        bh.append(pack((b_p * e_hat).astype(BF16)))
        kh.append(pack((kd_p * e_hat).astype(BF16)))
        vv.append(pack(v_ref[0, :, sl].astype(BF16)))
        bk = cat([pack((b_p * e_neg).astype(BF16)), pack((kd_p * e_neg).astype(BF16))], axis=0)
        aa.append(_mm(cat([at[p], rt[p]], axis=0), bk, NT).astype(BF16))
    aab = [jnp.where(strict, x[:rows, :rows], 0.0) for x in aa]
    aak = [jnp.where(strict, x[:rows, rows:], 0.0) for x in aa]
    aq = [cat([jnp.where(incl, x[rows:, :rows], 0.0), jnp.where(incl, x[rows:, rows:], 0.0)], axis=1)
          for x in aa]
    tinv = _tri_inverse(aab, cs, reverse)
    akv = [_mm(aak[p], vv[p], NN) for p in tl]
    wu = [_mm(tinv[p], cat([at[p], akv[p]], axis=1), NN) for p in tl]
    rhs = [cat([wu[p].astype(BF16), cat([zero_t, vv[p]], axis=1)], axis=0) for p in tl]
    qy = [_mm(aq[p], rhs[p], NN) for p in tl]
    mn = [_mm(cat([bh[p], kh[p]], axis=0), rhs[p], TN) for p in tl]
    lhs = []
    for p in tl:
        m_mat = jnp.where(eye_l, e_tot[:, p * LANES:(p + 1) * LANES], 0.0) + mn[p][:, :LANES]
        lhs.append(cat([m_mat, rt[p] + qy[p][:, :LANES]], axis=0))
    ms = [_mm(lhs[p], s_ref[p], NN) for p in tl]
    outs = []
    for p in tl:
        s_ref[p] = ms[p][:LANES] + mn[p][:, LANES:]
        ybd = ms[p][LANES:] + qy[p][:, LANES:]
        y = ybd[:cs]
        for h in range(1, hp):
            y = y + ybd[h * cs:(h + 1) * cs]
        outs.append(y)
    y_ref[0] = cat(outs, axis=1)


def _wkv_scan(r, v, kk, lw, kd, a, n, reverse):
    b, t, mix_b = r.shape
    heads = mix_b // n
    hg = SCAN_HEADS if heads % SCAN_HEADS == 0 else heads
    cs = SCAN_CHUNK
    nc = t // cs
    w = hg * n
    di = 1 if reverse else 0
    tchunk = (lambda ci: nc - 1 - ci) if reverse else (lambda ci: ci)
    shared = pl.BlockSpec((1, cs, w), lambda bi, hi, ci: (bi, tchunk(ci), hi))
    perdir = pl.BlockSpec((1, 1, cs, w), lambda bi, hi, ci: (di, bi, tchunk(ci), hi))
    return pl.pallas_call(
        functools.partial(_scan_body, n=n, reverse=reverse),
        out_shape=jax.ShapeDtypeStruct((b, t, mix_b), F32),
        grid=(b, heads // hg, nc),
        in_specs=[shared] * 3 + [perdir] * 3,
        out_specs=shared,
        scratch_shapes=[pltpu.VMEM((w // LANES, LANES, LANES), F32)],
        compiler_params=_params("parallel", "parallel", "arbitrary"),
        name="wkv_scan",
    )(r, v, kk, lw, kd, a)


def _post_body(yf_ref, yb_ref, bonus_ref, g_ref, gg_ref, gb_ref, j_ref, o_ref, *, n):
    y = yf_ref[...] + yb_ref[...]
    mean = _headsum(y, j_ref) * (1.0 / n)
    yc = y - mean
    var = _headsum(yc * yc, j_ref) * (1.0 / n)
    yn = yc * lax.rsqrt(var + GN_EPS) * gg_ref[...] + gb_ref[...]
    o_ref[...] = ((yn + bonus_ref[...]) * g_ref[...]).astype(o_ref.dtype)


def _rwkv_post(y_fwd, y_bwd, bonus, g, gn_g, gn_b, jmat, n):
    m, mix_b = y_fwd.shape
    tm = _pick(m, (256, 128, 64, 32, 16, 8))
    row = pl.BlockSpec((tm, mix_b), lambda i: (i, 0))
    vec = pl.BlockSpec((1, mix_b), lambda i: (0, 0))
    return pl.pallas_call(
        functools.partial(_post_body, n=n),
        out_shape=jax.ShapeDtypeStruct((m, mix_b), BF16),
        grid=(m // tm,),
        in_specs=[row, row, row, row, vec, vec, pl.BlockSpec(jmat.shape, lambda i: (0, 0))],
        out_specs=row,
        compiler_params=_params("parallel"),
        name="rwkv_post",
    )(y_fwd, y_bwd, bonus, g, gn_g.reshape(1, mix_b), gn_b.reshape(1, mix_b), jmat)


def _attn_body(q_ref, kv_ref, o_ref, *, heads):
    d = q_ref.shape[-1]
    hd = d // heads
    scale = hd ** -0.5
    for h in range(heads):
        q = q_ref[0, :, h * hd:(h + 1) * hd]
        k = kv_ref[0, :, h * hd:(h + 1) * hd]
        v = kv_ref[0, :, d + h * hd:d + (h + 1) * hd]
        s = lax.dot_general(q, k, NT, preferred_element_type=F32) * scale
        e = jnp.exp(s - jnp.max(s, axis=-1, keepdims=True))
        p = e / jnp.sum(e, axis=-1, keepdims=True)
        o_ref[0, :, h * hd:(h + 1) * hd] = _dot(p.astype(BF16), v).astype(o_ref.dtype)


def _attention(q, kv):
    b, t, d = q.shape
    n_mem = kv.shape[1]
    tq = _pick(t, (1024, 512, 256, 128, 64, 32, 16))
    return pl.pallas_call(
        functools.partial(_attn_body, heads=X_HEADS),
        out_shape=jax.ShapeDtypeStruct((b, t, d), BF16),
        grid=(b, t // tq),
        in_specs=[pl.BlockSpec((1, tq, d), lambda bi, i: (bi, i, 0)),
                  pl.BlockSpec((1, n_mem, 2 * d), lambda bi, i: (bi, 0, 0))],
        out_specs=pl.BlockSpec((1, tq, d), lambda bi, i: (bi, i, 0)),
        compiler_params=_params("parallel", "parallel"),
        name="cross_attention",
    )(q, kv)


def _rms(x, g):
    return x * lax.rsqrt(jnp.mean(x * x, axis=-1, keepdims=True) + RMS_EPS) * g


def _ffn_body(x_ref, xp_ref, xn_ref, gin_ref, gout_ref, wg_ref, wv_ref, cw_ref, cb_ref, wd_ref,
              o_ref, hx_ref):
    i = pl.program_id(1)
    f = pl.program_id(2)
    tm = x_ref.shape[1]
    halo = BF16_ROWS

    @pl.when(f == 0)
    def _():
        g = gin_ref[...]
        zero = jnp.zeros((halo, x_ref.shape[2]), BF16)
        hx_ref[0:halo, :] = jnp.where(i == 0, zero, _rms(xp_ref[0], g).astype(BF16))
        hx_ref[halo + tm:, :] = jnp.where(i == pl.num_programs(1) - 1, zero, _rms(xn_ref[0], g).astype(BF16))

        def norm_rows(c, carry):
            r0 = pl.multiple_of(c * NORM_ROWS, NORM_ROWS)
            hx_ref[pl.ds(halo + r0, NORM_ROWS), :] = _rms(x_ref[0, pl.ds(r0, NORM_ROWS), :], g).astype(BF16)
            o_ref[0, pl.ds(r0, NORM_ROWS), :] = jnp.zeros((NORM_ROWS, x_ref.shape[2]), F32)
            return carry

        lax.fori_loop(0, tm // NORM_ROWS, norm_rows, 0)

    hx = hx_ref[...]
    rows = tm + 2 * halo

    nf = pl.num_programs(2)

    def conv(w_ref, tile):
        z = _dot(hx, w_ref[...])
        prev = pltpu.roll(z, 1, 0)[halo:halo + tm]
        nxt = pltpu.roll(z, rows - 1, 0)[halo:halo + tm]
        cw = cw_ref[tile]
        return cw[0:1] * prev + cw[1:2] * z[halo:halo + tm] + cw[2:3] * nxt + cb_ref[tile]

    gate = conv(wg_ref, f)
    val = conv(wv_ref, nf + f)
    act = (jax.nn.silu(gate) * val).astype(BF16)
    o_ref[0] += _dot(act, wd_ref[...])

    @pl.when(f == pl.num_programs(2) - 1)
    def _():
        g = gout_ref[...]

        def out_rows(c, carry):
            rs = pl.ds(pl.multiple_of(c * NORM_ROWS, NORM_ROWS), NORM_ROWS)
            o_ref[0, rs, :] = _rms(x_ref[0, rs, :] + o_ref[0, rs, :], g)
            return carry

        lax.fori_loop(0, tm // NORM_ROWS, out_rows, 0)


def _conv_ffn(x, g_in, g_out, w_up, conv_w, conv_b, w_down):
    b, t, d = x.shape
    d_ff = w_down.shape[0]
    tm = _pick(t, (1024, 512, 256, 128))
    tf = _pick(d_ff, (256, 128))
    nf = d_ff // tf
    hb = tm // BF16_ROWS
    nblk = t // BF16_ROWS
    cw = conv_w.reshape(3, 2 * nf, tf).swapaxes(0, 1)
    cb = conv_b.reshape(2 * nf, 1, tf)
    once = pl.Buffered(1)
    return pl.pallas_call(
        _ffn_body,
        out_shape=jax.ShapeDtypeStruct((b, t, d), F32),
        grid=(b, t // tm, nf),
        in_specs=[
            pl.BlockSpec((1, tm, d), lambda bi, i, f: (bi, i, 0), pipeline_mode=once),
            pl.BlockSpec((1, BF16_ROWS, d), lambda bi, i, f: (bi, jnp.maximum(i * hb - 1, 0), 0)),
            pl.BlockSpec((1, BF16_ROWS, d), lambda bi, i, f: (bi, jnp.minimum((i + 1) * hb, nblk - 1), 0)),
            pl.BlockSpec((1, d), lambda bi, i, f: (0, 0)),
            pl.BlockSpec((1, d), lambda bi, i, f: (0, 0)),
            pl.BlockSpec((d, tf), lambda bi, i, f: (0, f)),
            pl.BlockSpec((d, tf), lambda bi, i, f: (0, nf + f)),
            pl.BlockSpec((2 * nf, 3, tf), lambda bi, i, f: (0, 0, 0), pipeline_mode=once),
            pl.BlockSpec((2 * nf, 1, tf), lambda bi, i, f: (0, 0, 0), pipeline_mode=once),
            pl.BlockSpec((tf, d), lambda bi, i, f: (f, 0)),
        ],
        out_specs=pl.BlockSpec((1, tm, d), lambda bi, i, f: (bi, i, 0), pipeline_mode=once),
        scratch_shapes=[pltpu.VMEM((tm + 2 * BF16_ROWS, d), BF16)],
        compiler_params=_params("parallel", "parallel", "arbitrary"),
        name="conv_ffn",
    )(x, x, x, g_in.reshape(1, d), g_out.reshape(1, d), w_up, w_up, cw, cb, w_down)


def _trunk(x, mem, p):
    b, t, d = x.shape
    m = b * t
    n_mem = mem.shape[1]
    mix_a = p["ln_v_g"].shape[-1]
    mix_b = p["k_k"].shape[-1]
    n = p["r_k"].shape[-1]
    x2d = x.reshape(m, d)

    h1 = _rmsnorm([x2d], p["norm_mix"], BF16)
    za = _matmul([h1], p["w_in"], F32, col0=0, n=2 * mix_a)
    zrkv = _matmul([h1], p["w_in"], F32, col0=2 * mix_a, n=3 * mix_b)
    zl = _matmul([h1], p["w_in"], F32, col0=2 * mix_a + 3 * mix_b, n=p["w_in"].shape[1] - 2 * mix_a - 3 * mix_b)

    ya = _gmlp(za, p["ln_v_g"], p["ln_v_b"], p["w_s"], p["b_s"])

    r, v, kk, g, bonus, lw, kd, a = _rwkv_prep(
        zrkv.reshape(b, t, 3 * mix_b), zl.reshape(b, t, -1), p["mu_shift"], p["w0"], p["w_up_decay"],
        p["a0"], p["w_up_iclr"], p["w_up_gate"], p["k_k"], p["k_a"], p["r_k"], p["jmat"])
    y_fwd = _wkv_scan(r, v, kk, lw, kd, a, n, reverse=False)
    y_bwd = _wkv_scan(r, v, kk, lw, kd, a, n, reverse=True)
    yb = _rwkv_post(y_fwd.reshape(m, mix_b), y_bwd.reshape(m, mix_b), bonus.reshape(m, mix_b),
                    g.reshape(m, mix_b), p["gn_g"], p["gn_b"], p["jmat"], n)

    x1 = _matmul([ya, yb], p["w_out"], F32, residual=x2d)

    hq = _rmsnorm([x1], p["norm_x"], BF16)
    q = _matmul([hq], p["w_q"], BF16)
    memn = _rmsnorm([mem.reshape(b * n_mem, d)], p["norm_mem"], BF16)
    kv = _matmul([memn], p["w_kv"], BF16)
    o = _attention(q.reshape(b, t, d), kv.reshape(b, n_mem, 2 * d))
    x2 = _matmul([o.reshape(m, d)], p["w_o"], F32, residual=x1)

    return _conv_ffn(x2.reshape(b, t, d), p["norm_ffn"], p["norm_out"], p["w_ffn_up"], p["conv_w"],
                     p["conv_b"], p["w_ffn_down"])


def kernel(x_prompt, x_sample, mem_prompt, mem_sample, norm_mix, w_in, mu_shift, ln_v_g, ln_v_b, w_s, b_s, w0, w_up_decay, a0, w_up_iclr, w_up_gate, k_k, k_a, r_k, gn_g, gn_b, w_out, norm_x, norm_mem, w_q, w_kv, w_o, norm_ffn, w_ffn_up, conv_w, conv_b, w_ffn_down, norm_out):
    depth = w_in.shape[0]
    n = r_k.shape[-1]
    head_id = jnp.arange(LANES) // n
    jmat = (head_id[:, None] == head_id[None, :]).astype(BF16)

    layers = []
    for l in range(depth):
        layers.append(dict(
            norm_mix=norm_mix[l], w_in=w_in[l].astype(BF16), mu_shift=mu_shift[l],
            ln_v_g=ln_v_g[l], ln_v_b=ln_v_b[l], w_s=w_s[l], b_s=b_s[l],
            w0=w0[l], w_up_decay=w_up_decay[l], a0=a0[l], w_up_iclr=w_up_iclr[l], w_up_gate=w_up_gate[l],
            k_k=k_k[l], k_a=k_a[l], r_k=r_k[l], gn_g=gn_g[l], gn_b=gn_b[l],
            w_out=w_out[l].astype(BF16), norm_x=norm_x[l], norm_mem=norm_mem[l],
            w_q=w_q[l].astype(BF16), w_kv=w_kv[l].astype(BF16), w_o=w_o[l].astype(BF16),
            norm_ffn=norm_ffn[l], w_ffn_up=w_ffn_up[l].astype(BF16), conv_w=conv_w[l], conv_b=conv_b[l],
            w_ffn_down=w_ffn_down[l].astype(BF16), jmat=jmat))

    (p,) = layers
    p = dict(p, norm_out=norm_out)
    return _trunk(x_prompt, mem_prompt, p), _trunk(x_sample, mem_sample, p)
```

```python
import functools

import jax
import jax.numpy as jnp
from jax import lax
from jax.experimental import pallas as pl
from jax.experimental.pallas import tpu as pltpu

RMS_EPS = 1e-6
LN_EPS = 1e-5
GN_EPS = 64e-5
L2_EPS = 1e-12
LOGW_SCALE = 0.6065306597126334
X_HEADS = 4

LANES = 128
BF16_ROWS = 16
SCAN_CHUNK = 64
SCAN_HEADS = 32
NORM_ROWS = 128
INV_BASE = 8
VMEM_LIMIT = 60 * 1024 * 1024

F32 = jnp.float32
BF16 = jnp.bfloat16


def _params(*sem, flags=None):
    return pltpu.CompilerParams(dimension_semantics=sem, vmem_limit_bytes=VMEM_LIMIT, flags=flags)


def _pick(n, candidates):
    for c in candidates:
        if n % c == 0:
            return c
    raise ValueError(f"no tile for {n} in {candidates}")


def _dot(a, b):
    return jnp.dot(a, b, preferred_element_type=F32)


NN = (((1,), (0,)), ((), ()))
NT = (((1,), (1,)), ((), ()))
TN = (((0,), (0,)), ((), ()))


def _rmsnorm_body(*refs, n_in):
    g_ref, o_ref = refs[n_in], refs[n_in + 1]
    x = refs[0][...]
    for r in refs[1:n_in]:
        x = x + r[...]
    ms = jnp.mean(x * x, axis=-1, keepdims=True)
    o_ref[...] = (x * lax.rsqrt(ms + RMS_EPS) * g_ref[...]).astype(o_ref.dtype)


def _rmsnorm(xs, g, out_dtype):
    m, d = xs[0].shape
    tm = _pick(m, (512, 256, 128, 64, 32, 16, 8))
    row = pl.BlockSpec((tm, d), lambda i: (i, 0))
    return pl.pallas_call(
        functools.partial(_rmsnorm_body, n_in=len(xs)),
        out_shape=jax.ShapeDtypeStruct((m, d), out_dtype),
        grid=(m // tm,),
        in_specs=[row] * len(xs) + [pl.BlockSpec((1, d), lambda i: (0, 0))],
        out_specs=row,
        compiler_params=_params("parallel"),
        name="rmsnorm",
    )(*xs, g.reshape(1, d))


def _mm_body(*refs, n_lhs, has_res):
    a_refs, w_refs = refs[:n_lhs], refs[n_lhs:2 * n_lhs]
    o_ref = refs[-1]
    acc = _dot(a_refs[0][...], w_refs[0][...])
    for a, w in zip(a_refs[1:], w_refs[1:]):
        acc = acc + _dot(a[...], w[...])
    if has_res:
        acc = acc + refs[2 * n_lhs][...]
    o_ref[...] = acc.astype(o_ref.dtype)


def _matmul(lhs, w, out_dtype, residual=None, col0=0, n=None):
    m = lhs[0].shape[0]
    n = w.shape[1] if n is None else n
    tm = _pick(m, (1024, 512, 256, 128, 64, 32, 16, 8))
    tn = _pick(n, (1024, 768, 512, 256, 128))
    if col0 % tn:
        w, col0 = w[:, col0:col0 + n], 0
    cb = col0 // tn
    in_specs = [pl.BlockSpec((tm, a.shape[1]), lambda i, j: (i, 0)) for a in lhs]
    in_specs += [pl.BlockSpec((a.shape[1], tn), lambda i, j, rb=rb: (rb, cb + j)) for rb, a in enumerate(lhs)]
    assert all(a.shape[1] == lhs[0].shape[1] for a in lhs) and len(lhs) * lhs[0].shape[1] == w.shape[0]
    args = list(lhs) + [w] * len(lhs)
    if residual is not None:
        in_specs.append(pl.BlockSpec((tm, tn), lambda i, j: (i, j)))
        args.append(residual)
    return pl.pallas_call(
        functools.partial(_mm_body, n_lhs=len(lhs), has_res=residual is not None),
        out_shape=jax.ShapeDtypeStruct((m, n), out_dtype),
        grid=(m // tm, n // tn),
        in_specs=in_specs,
        out_specs=pl.BlockSpec((tm, tn), lambda i, j: (i, j)),
        compiler_params=_params("parallel", "parallel"),
        name="matmul",
    )(*args)


def _gmlp_body(zu_ref, zv_ref, g_ref, b_ref, ws_ref, bs_ref, o_ref, *, chunk):
    u = jax.nn.gelu(zu_ref[...])
    v = jax.nn.gelu(zv_ref[...])
    mu = jnp.mean(v, axis=-1, keepdims=True)
    var = jnp.mean(jnp.square(v - mu), axis=-1, keepdims=True)
    vn = ((v - mu) * lax.rsqrt(var + LN_EPS) * g_ref[...] + b_ref[...]).astype(BF16)
    rows, width = u.shape
    heads = ws_ref.shape[0]
    hd = width // heads
    bs = bs_ref[...]
    for c in range(rows // chunk):
        rs = slice(c * chunk, (c + 1) * chunk)
        mixed = jnp.concatenate(
            [_dot(ws_ref[h], vn[rs, h * hd:(h + 1) * hd]) for h in range(heads)], axis=1)
        o_ref[rs, :] = (u[rs, :] * (mixed + bs)).astype(o_ref.dtype)


def _gmlp(za, ln_g, ln_b, w_s, b_s):
    m, two_a = za.shape
    mix_a = two_a // 2
    heads, chunk, _ = w_s.shape
    rows = _pick(m, (4 * chunk, 2 * chunk, chunk))
    bs_full = jnp.repeat(b_s.T, mix_a // heads, axis=1)
    vec = pl.BlockSpec((1, mix_a), lambda i: (0, 0))
    return pl.pallas_call(
        functools.partial(_gmlp_body, chunk=chunk),
        out_shape=jax.ShapeDtypeStruct((m, mix_a), BF16),
        grid=(m // rows,),
        in_specs=[pl.BlockSpec((rows, mix_a), lambda i: (i, 0)),
                  pl.BlockSpec((rows, mix_a), lambda i: (i, 1)),
                  vec, vec,
                  pl.BlockSpec((heads, chunk, chunk), lambda i: (0, 0, 0)),
                  pl.BlockSpec((chunk, mix_a), lambda i: (0, 0))],
        out_specs=pl.BlockSpec((rows, mix_a), lambda i: (i, 0)),
        compiler_params=_params("parallel"),
        name="gmlp",
    )(za, za, ln_g.reshape(1, mix_a), ln_b.reshape(1, mix_a), w_s.astype(BF16), bs_full)


def _headsum(x, j_ref):
    tm, w = x.shape
    nb = w // LANES
    xs = jnp.concatenate([x[:, i * LANES:(i + 1) * LANES] for i in range(nb)], axis=0)
    s = _dot(xs.astype(BF16), j_ref[...])
    return jnp.concatenate([s[i * tm:(i + 1) * tm] for i in range(nb)], axis=1)


def _tshift(z, zp, zn, mu, first, last):
    tm = z.shape[0]
    rows = lax.broadcasted_iota(jnp.int32, (8, 1), 0)
    prev_row = jnp.where(first, 0.0, zp[7:8, :])
    next_row = jnp.where(last, 0.0, zn[0:1, :])
    prev = pltpu.roll(z, 1, 0)
    nxt = pltpu.roll(z, tm - 1, 0)
    prev = jnp.concatenate([jnp.where(rows == 0, prev_row, prev[:8]), prev[8:]], axis=0)
    nxt = jnp.concatenate([nxt[:tm - 8], jnp.where(rows == 7, next_row, nxt[tm - 8:])], axis=0)
    return z * (1.0 - mu) + (0.5 * mu) * (prev + nxt)


def _prep_body(zr, zrp, zrn, zk, zkp, zkn, zv, zvp, zvn, zl, zlp, zln,
               mur, muk, muv, mul, w0_ref, wd_ref, a0_ref, wi_ref, wg_ref,
               kk_ref, ka_ref, rk_ref, j_ref,
               r_out, v_out, kk_out, g_out, bonus_out, lw_out, kd_out, a_out,
               *, lora_w, lora_a):
    i = pl.program_id(1)
    first = i == 0
    last = i == pl.num_programs(1) - 1
    r = _tshift(zr[0], zrp[0], zrn[0], mur[...], first, last)
    k = _tshift(zk[0], zkp[0], zkn[0], muk[...], first, last)
    v = _tshift(zv[0], zvp[0], zvn[0], muv[...], first, last)
    lo = _tshift(zl[0], zlp[0], zln[0], mul[...], first, last)
    xw = lo[:, :lora_w]
    xa = lo[:, lora_w:lora_w + lora_a]
    xg = lo[:, lora_w + lora_a:]

    g_out[0] = _dot(jax.nn.sigmoid(xg).astype(BF16), wg_ref[...]).astype(g_out.dtype)
    kkr = k * kk_ref[...]
    ss = _headsum(kkr * kkr, j_ref)
    kk = kkr * lax.rsqrt(jnp.maximum(ss, L2_EPS))
    tw = jnp.tanh(xw).astype(BF16)
    xab = xa.astype(BF16)
    rk = None
    for d in range(2):
        lw_out[d, 0] = -LOGW_SCALE * jax.nn.sigmoid(w0_ref[d:d + 1, :] + _dot(tw, wd_ref[d]))
        a = jax.nn.sigmoid(a0_ref[d:d + 1, :] + _dot(xab, wi_ref[d]))
        kd = k * (1.0 + (a - 1.0) * ka_ref[...])
        a_out[d, 0] = a.astype(a_out.dtype)
        kd_out[d, 0] = kd.astype(kd_out.dtype)
        s = _headsum(r * kd * rk_ref[...], j_ref)
        rk = s if rk is None else rk + s
    r_out[0] = r.astype(r_out.dtype)
    v_out[0] = v.astype(v_out.dtype)
    kk_out[0] = kk.astype(kk_out.dtype)
    bonus_out[0] = (rk * v).astype(bonus_out.dtype)


def _rwkv_prep(zrkv, zl, mu, w0, w_up_decay, a0, w_up_iclr, w_up_gate, k_k, k_a, r_k, jmat):
    b, t, three_b = zrkv.shape
    mix_b = three_b // 3
    nl = zl.shape[-1]
    lora_w, lora_a = w_up_decay.shape[1], w_up_iclr.shape[1]
    tm = _pick(t, (128, 64, 32, 16, 8))
    hb = tm // 8
    nblk8 = t // 8

    def main(w, col):
        return pl.BlockSpec((1, tm, w), lambda bi, i: (bi, i, col))

    def prev(w, col):
        return pl.BlockSpec((1, 8, w), lambda bi, i: (bi, jnp.maximum(i * hb - 1, 0), col))

    def nxt(w, col):
        return pl.BlockSpec((1, 8, w), lambda bi, i: (bi, jnp.minimum((i + 1) * hb, nblk8 - 1), col))

    def const(shape):
        return pl.BlockSpec(shape, lambda bi, i: (0,) * len(shape))

    in_specs, args = [], []
    for col in range(3):
        in_specs += [main(mix_b, col), prev(mix_b, col), nxt(mix_b, col)]
        args += [zrkv] * 3
    in_specs += [main(nl, 0), prev(nl, 0), nxt(nl, 0)]
    args += [zl] * 3
    mu2 = mu.reshape(1, -1)
    in_specs += [pl.BlockSpec((1, mix_b), lambda bi, i, c=c: (0, c)) for c in range(3)]
    args += [mu2[:, :three_b]] * 3
    in_specs.append(const((1, nl)))
    args.append(mu2[:, three_b:])
    consts = [w0, w_up_decay.astype(BF16), a0, w_up_iclr.astype(BF16), w_up_gate.astype(BF16),
              k_k.reshape(1, mix_b), k_a.reshape(1, mix_b), r_k.reshape(1, mix_b), jmat]
    in_specs += [const(c.shape) for c in consts]
    args += consts

    shared = jax.ShapeDtypeStruct((b, t, mix_b), BF16)
    perdir = lambda dt: jax.ShapeDtypeStruct((2, b, t, mix_b), dt)
    o_shared = pl.BlockSpec((1, tm, mix_b), lambda bi, i: (bi, i, 0))
    o_perdir = pl.BlockSpec((2, 1, tm, mix_b), lambda bi, i: (0, bi, i, 0))
    return pl.pallas_call(
        functools.partial(_prep_body, lora_w=lora_w, lora_a=lora_a),
        out_shape=[shared] * 5 + [perdir(F32), perdir(BF16), perdir(BF16)],
        grid=(b, t // tm),
        in_specs=in_specs,
        out_specs=[o_shared] * 5 + [o_perdir] * 3,
        compiler_params=_params("parallel", "parallel"),
        name="rwkv_prep",
    )(*args)


def _mm(a, b, dims):
    return lax.dot_general(a.astype(BF16), b.astype(BF16), dims, preferred_element_type=F32)


def _halves(x, size):
    blocks = x.shape[0] // (2 * size)
    lo = jnp.concatenate([x[2 * j * size:(2 * j + 1) * size] for j in range(blocks)], axis=0)
    hi = jnp.concatenate([x[(2 * j + 1) * size:(2 * j + 2) * size] for j in range(blocks)], axis=0)
    return lo, hi


def _interleave(lo, hi, size):
    pieces = []
    for j in range(lo.shape[0] // size):
        pieces += [lo[j * size:(j + 1) * size], hi[j * size:(j + 1) * size]]
    return jnp.concatenate(pieces, axis=0)


def _tri_inverse(ls, limit, reverse):
    rows = ls[0].shape[0]
    row = lax.broadcasted_iota(jnp.int32, (rows, rows), 0)
    col = lax.broadcasted_iota(jnp.int32, (rows, rows), 1)
    eye = (row == col).astype(F32)
    sh = INV_BASE.bit_length() - 1
    base = (row >> sh) == (col >> sh)
    ps = [jnp.where(base, l, 0.0) for l in ls]
    ts = [jnp.where(row == col, 1.0, p) for p in ps]
    size = 2
    while size < INV_BASE:
        ps = [_mm(p, p, NN) for p in ps]
        ts = [_mm(t, eye + p, NN) for t, p in zip(ts, ps)]
        size *= 2
    size = INV_BASE
    hrow = lax.broadcasted_iota(jnp.int32, (rows // 2, rows), 0)
    hcol = lax.broadcasted_iota(jnp.int32, (rows // 2, rows), 1)
    zero = jnp.zeros((rows // 2, rows), F32)
    while size < limit:
        sh = size.bit_length() - 1
        partner = ((hcol >> (sh + 1)) == (hrow >> sh)) & (((hcol >> sh) & 1) == (1 if reverse else 0))
        act = 0 if reverse else 1
        l_act = [_halves(l, size)[act] for l in ls]
        t_halves = [_halves(t, size) for t in ts]
        xs = [_mm(jnp.where(partner, la, 0.0), t, NN) for la, t in zip(l_act, ts)]
        xs = [_interleave(x, zero, size) if reverse else _interleave(zero, x, size) for x in xs]
        upd = [th[act] + _mm(th[act], x, NN) for th, x in zip(t_halves, xs)]
        ts = [_interleave(u, th[1], size) if reverse else _interleave(th[0], u, size)
              for th, u in zip(t_halves, upd)]
        size *= 2
    return ts


def _scan_body(r_ref, v_ref, kk_ref, lw_ref, kd_ref, a_ref, y_ref, s_ref, *, n, reverse):
    c = pl.program_id(2)
    cs = r_ref.shape[1]

    @pl.when(c == 0)
    def _():
        s_ref[...] = jnp.zeros_like(s_ref)

    sgn = -1 if reverse else 1
    tiles = r_ref.shape[2] // LANES
    hp = LANES // n
    rows = hp * cs

    lw = lw_ref[0, 0]
    rc = lax.broadcasted_iota(jnp.int32, (cs, cs), 0)
    cc = lax.broadcasted_iota(jnp.int32, (cs, cs), 1)
    tri = ((rc - cc) * sgn >= 0).astype(BF16)
    l_hi = lw.astype(BF16)
    l_mid = (lw - l_hi.astype(F32)).astype(BF16)
    l_lo = (lw - l_hi.astype(F32) - l_mid.astype(F32)).astype(BF16)
    cum = _dot(tri, l_hi) + (_dot(tri, l_mid) + _dot(tri, l_lo))
    tot = cum[0:1, :] if reverse else cum[cs - 1:cs, :]
    e_tot = jnp.exp(tot)

    row = lax.broadcasted_iota(jnp.int32, (rows, rows), 0)
    col = lax.broadcasted_iota(jnp.int32, (rows, rows), 1)
    order = ((row & (cs - 1)) - (col & (cs - 1))) * sgn
    strict = order > 0
    incl = order >= 0
    eye_l = (lax.broadcasted_iota(jnp.int32, (LANES, LANES), 0)
             == lax.broadcasted_iota(jnp.int32, (LANES, LANES), 1))
    lane_head = lax.broadcasted_iota(jnp.int32, (cs, LANES), 1) // n
    head_masks = [lane_head == h for h in range(hp)]
    zero_t = jnp.zeros((rows, LANES), BF16)

    def pack(x):
        return jnp.concatenate([jnp.where(mk, x, jnp.zeros_like(x)) for mk in head_masks], axis=0)

    tl = range(tiles)
    cat = jnp.concatenate
    at, rt, bh, kh, vv, aa = [], [], [], [], [], []
    for p in tl:
        sl = slice(p * LANES, (p + 1) * LANES)
        cum_p, lw_p, tot_p = cum[:, sl], lw[:, sl], tot[:, sl]
        kk_p = kk_ref[0, :, sl].astype(F32)
        kd_p = kd_ref[0, 0, :, sl].astype(F32)
        b_p = kk_p * a_ref[0, 0, :, sl].astype(F32)
        e_neg = jnp.exp(-cum_p)
        e_hat = jnp.exp(tot_p - cum_p)
        at.append(pack(-kk_p * jnp.exp(cum_p - lw_p)))
        rt.append(pack(r_ref[0, :, sl].astype(F32) * jnp.exp(cum_p)))
        bh.append(pack((b_p * e_hat).astype(BF16)))
        kh.append(pack((kd_p * e_hat).astype(BF16)))
        vv.append(pack(v_ref[0, :, sl].astype(BF16)))
        bk = cat([pack((b_p * e_neg).astype(BF16)), pack((kd_p * e_neg).astype(BF16))], axis=0)
        aa.append(_mm(cat([at[p], rt[p]], axis=0), bk, NT).astype(BF16))
    aab = [jnp.where(strict, x[:rows, :rows], 0.0) for x in aa]
    aak = [jnp.where(strict, x[:rows, rows:], 0.0) for x in aa]
    aq = [cat([jnp.where(incl, x[rows:, :rows], 0.0), jnp.where(incl, x[rows:, rows:], 0.0)], axis=1)
          for x in aa]
    tinv = _tri_inverse(aab, cs, reverse)
    akv = [_mm(aak[p], vv[p], NN) for p in tl]
    wu = [_mm(tinv[p], cat([at[p], akv[p]], axis=1), NN) for p in tl]
    rhs = [cat([wu[p].astype(BF16), cat([zero_t, vv[p]], axis=1)], axis=0) for p in tl]
    qy = [_mm(aq[p], rhs[p], NN) for p in tl]
    mn = [_mm(cat([bh[p], kh[p]], axis=0), rhs[p], TN) for p in tl]
    lhs = []
    for p in tl:
        m_mat = jnp.where(eye_l, e_tot[:, p * LANES:(p + 1) * LANES], 0.0) + mn[p][:, :LANES]
        lhs.append(cat([m_mat, rt[p] + qy[p][:, :LANES]], axis=0))
    ms = [_mm(lhs[p], s_ref[p], NN) for p in tl]
    outs = []
    for p in tl:
        s_ref[p] = ms[p][:LANES] + mn[p][:, LANES:]
        ybd = ms[p][LANES:] + qy[p][:, LANES:]
        y = ybd[:cs]
        for h in range(1, hp):
            y = y + ybd[h * cs:(h + 1) * cs]
        outs.append(y)
    y_ref[0] = cat(outs, axis=1)


def _wkv_scan(r, v, kk, lw, kd, a, n, reverse):
    b, t, mix_b = r.shape
    heads = mix_b // n
    hg = SCAN_HEADS if heads % SCAN_HEADS == 0 else heads
    cs = SCAN_CHUNK
    nc = t // cs
    w = hg * n
    di = 1 if reverse else 0
    tchunk = (lambda ci: nc - 1 - ci) if reverse else (lambda ci: ci)
    shared = pl.BlockSpec((1, cs, w), lambda bi, hi, ci: (bi, tchunk(ci), hi))
    perdir = pl.BlockSpec((1, 1, cs, w), lambda bi, hi, ci: (di, bi, tchunk(ci), hi))
    return pl.pallas_call(
        functools.partial(_scan_body, n=n, reverse=reverse),
        out_shape=jax.ShapeDtypeStruct((b, t, mix_b), F32),
        grid=(b, heads // hg, nc),
        in_specs=[shared] * 3 + [perdir] * 3,
        out_specs=shared,
        scratch_shapes=[pltpu.VMEM((w // LANES, LANES, LANES), F32)],
        compiler_params=_params("parallel", "parallel", "arbitrary"),
        name="wkv_scan",
    )(r, v, kk, lw, kd, a)


def _post_body(yf_ref, yb_ref, bonus_ref, g_ref, gg_ref, gb_ref, j_ref, o_ref, *, n):
    y = yf_ref[...] + yb_ref[...]
    mean = _headsum(y, j_ref) * (1.0 / n)
    yc = y - mean
    var = _headsum(yc * yc, j_ref) * (1.0 / n)
    yn = yc * lax.rsqrt(var + GN_EPS) * gg_ref[...] + gb_ref[...]
    o_ref[...] = ((yn + bonus_ref[...]) * g_ref[...]).astype(o_ref.dtype)


def _rwkv_post(y_fwd, y_bwd, bonus, g, gn_g, gn_b, jmat, n):
    m, mix_b = y_fwd.shape
    tm = _pick(m, (512, 256, 128, 64, 32, 16, 8))
    row = pl.BlockSpec((tm, mix_b), lambda i: (i, 0))
    vec = pl.BlockSpec((1, mix_b), lambda i: (0, 0))
    return pl.pallas_call(
        functools.partial(_post_body, n=n),
        out_shape=jax.ShapeDtypeStruct((m, mix_b), BF16),
        grid=(m // tm,),
        in_specs=[row, row, row, row, vec, vec, pl.BlockSpec(jmat.shape, lambda i: (0, 0))],
        out_specs=row,
        compiler_params=_params("parallel"),
        name="rwkv_post",
    )(y_fwd, y_bwd, bonus, g, gn_g.reshape(1, mix_b), gn_b.reshape(1, mix_b), jmat)


def _attn_body(q_ref, kv_ref, o_ref, *, heads):
    d = q_ref.shape[-1]
    hd = d // heads
    scale = hd ** -0.5
    for h in range(heads):
        q = q_ref[0, :, h * hd:(h + 1) * hd]
        k = kv_ref[0, :, h * hd:(h + 1) * hd]
        v = kv_ref[0, :, d + h * hd:d + (h + 1) * hd]
        s = lax.dot_general(q, k, NT, preferred_element_type=F32) * scale
        e = jnp.exp(s - jnp.max(s, axis=-1, keepdims=True))
        p = e / jnp.sum(e, axis=-1, keepdims=True)
        o_ref[0, :, h * hd:(h + 1) * hd] = _dot(p.astype(BF16), v).astype(o_ref.dtype)


def _attention(q, kv):
    b, t, d = q.shape
    n_mem = kv.shape[1]
    tq = _pick(t, (1024, 512, 256, 128, 64, 32, 16))
    return pl.pallas_call(
        functools.partial(_attn_body, heads=X_HEADS),
        out_shape=jax.ShapeDtypeStruct((b, t, d), BF16),
        grid=(b, t // tq),
        in_specs=[pl.BlockSpec((1, tq, d), lambda bi, i: (bi, i, 0)),
                  pl.BlockSpec((1, n_mem, 2 * d), lambda bi, i: (bi, 0, 0))],
        out_specs=pl.BlockSpec((1, tq, d), lambda bi, i: (bi, i, 0)),
        compiler_params=_params("parallel", "parallel"),
        name="cross_attention",
    )(q, kv)


def _rms(x, g):
    return x * lax.rsqrt(jnp.mean(x * x, axis=-1, keepdims=True) + RMS_EPS) * g


def _ffn_body(x_ref, xp_ref, xn_ref, gin_ref, gout_ref, wg_ref, wv_ref, cwg_ref, cwv_ref, cbg_ref,
              cbv_ref, wd_ref, o_ref, hx_ref):
    i = pl.program_id(1)
    f = pl.program_id(2)
    tm = x_ref.shape[1]
    halo = BF16_ROWS

    @pl.when(f == 0)
    def _():
        g = gin_ref[...]
        zero = jnp.zeros((halo, x_ref.shape[2]), BF16)
        hx_ref[0:halo, :] = jnp.where(i == 0, zero, _rms(xp_ref[0], g).astype(BF16))
        hx_ref[halo + tm:, :] = jnp.where(i == pl.num_programs(1) - 1, zero, _rms(xn_ref[0], g).astype(BF16))

        def norm_rows(c, carry):
            r0 = pl.multiple_of(c * NORM_ROWS, NORM_ROWS)
            hx_ref[pl.ds(halo + r0, NORM_ROWS), :] = _rms(x_ref[0, pl.ds(r0, NORM_ROWS), :], g).astype(BF16)
            o_ref[0, pl.ds(r0, NORM_ROWS), :] = jnp.zeros((NORM_ROWS, x_ref.shape[2]), F32)
            return carry

        lax.fori_loop(0, tm // NORM_ROWS, norm_rows, 0)

    hx = hx_ref[...]
    rows = tm + 2 * halo

    def conv(w_ref, cw_ref, cb_ref):
        z = _dot(hx, w_ref[...])
        prev = pltpu.roll(z, 1, 0)[halo:halo + tm]
        nxt = pltpu.roll(z, rows - 1, 0)[halo:halo + tm]
        cw = cw_ref[...]
        return cw[0:1] * prev + cw[1:2] * z[halo:halo + tm] + cw[2:3] * nxt + cb_ref[...]

    gate = conv(wg_ref, cwg_ref, cbg_ref)
    val = conv(wv_ref, cwv_ref, cbv_ref)
    act = (jax.nn.silu(gate) * val).astype(BF16)
    o_ref[0] += _dot(act, wd_ref[...])

    @pl.when(f == pl.num_programs(2) - 1)
    def _():
        g = gout_ref[...]

        def out_rows(c, carry):
            rs = pl.ds(pl.multiple_of(c * NORM_ROWS, NORM_ROWS), NORM_ROWS)
            o_ref[0, rs, :] = _rms(x_ref[0, rs, :] + o_ref[0, rs, :], g)
            return carry

        lax.fori_loop(0, tm // NORM_ROWS, out_rows, 0)


def _conv_ffn(x, g_in, g_out, w_up, conv_w, conv_b, w_down):
    b, t, d = x.shape
    d_ff = w_down.shape[0]
    tm = _pick(t, (1024, 512, 256, 128))
    tf = _pick(d_ff, (256, 128))
    nf = d_ff // tf
    hb = tm // BF16_ROWS
    nblk = t // BF16_ROWS
    cb = conv_b.reshape(1, 2 * d_ff)
    once = pl.Buffered(1)
    return pl.pallas_call(
        _ffn_body,
        out_shape=jax.ShapeDtypeStruct((b, t, d), F32),
        grid=(b, t // tm, nf),
        in_specs=[
            pl.BlockSpec((1, tm, d), lambda bi, i, f: (bi, i, 0), pipeline_mode=once),
            pl.BlockSpec((1, BF16_ROWS, d), lambda bi, i, f: (bi, jnp.maximum(i * hb - 1, 0), 0)),
            pl.BlockSpec((1, BF16_ROWS, d), lambda bi, i, f: (bi, jnp.minimum((i + 1) * hb, nblk - 1), 0)),
            pl.BlockSpec((1, d), lambda bi, i, f: (0, 0)),
            pl.BlockSpec((1, d), lambda bi, i, f: (0, 0)),
            pl.BlockSpec((d, tf), lambda bi, i, f: (0, f)),
            pl.BlockSpec((d, tf), lambda bi, i, f: (0, nf + f)),
            pl.BlockSpec((3, tf), lambda bi, i, f: (0, f)),
            pl.BlockSpec((3, tf), lambda bi, i, f: (0, nf + f)),
            pl.BlockSpec((1, tf), lambda bi, i, f: (0, f)),
            pl.BlockSpec((1, tf), lambda bi, i, f: (0, nf + f)),
            pl.BlockSpec((tf, d), lambda bi, i, f: (f, 0)),
        ],
        out_specs=pl.BlockSpec((1, tm, d), lambda bi, i, f: (bi, i, 0), pipeline_mode=once),
        scratch_shapes=[pltpu.VMEM((tm + 2 * BF16_ROWS, d), BF16)],
        compiler_params=_params("parallel", "parallel", "arbitrary"),
        name="conv_ffn",
    )(x, x, x, g_in.reshape(1, d), g_out.reshape(1, d), w_up, w_up, conv_w, conv_w, cb, cb, w_down)


def _trunk(x, mem, p):
    b, t, d = x.shape
    m = b * t
    n_mem = mem.shape[1]
    mix_a = p["ln_v_g"].shape[-1]
    mix_b = p["k_k"].shape[-1]
    n = p["r_k"].shape[-1]
    x2d = x.reshape(m, d)

    h1 = _rmsnorm([x2d], p["norm_mix"], BF16)
    za = _matmul([h1], p["w_in"], F32, col0=0, n=2 * mix_a)
    zrkv = _matmul([h1], p["w_in"], F32, col0=2 * mix_a, n=3 * mix_b)
    zl = _matmul([h1], p["w_in"], F32, col0=2 * mix_a + 3 * mix_b, n=p["w_in"].shape[1] - 2 * mix_a - 3 * mix_b)

    ya = _gmlp(za, p["ln_v_g"], p["ln_v_b"], p["w_s"], p["b_s"])

    r, v, kk, g, bonus, lw, kd, a = _rwkv_prep(
        zrkv.reshape(b, t, 3 * mix_b), zl.reshape(b, t, -1), p["mu_shift"], p["w0"], p["w_up_decay"],
        p["a0"], p["w_up_iclr"], p["w_up_gate"], p["k_k"], p["k_a"], p["r_k"], p["jmat"])
    y_fwd = _wkv_scan(r, v, kk, lw, kd, a, n, reverse=False)
    y_bwd = _wkv_scan(r, v, kk, lw, kd, a, n, reverse=True)
    yb = _rwkv_post(y_fwd.reshape(m, mix_b), y_bwd.reshape(m, mix_b), bonus.reshape(m, mix_b),
                    g.reshape(m, mix_b), p["gn_g"], p["gn_b"], p["jmat"], n)

    x1 = _matmul([ya, yb], p["w_out"], F32, residual=x2d)

    hq = _rmsnorm([x1], p["norm_x"], BF16)
    q = _matmul([hq], p["w_q"], BF16)
    memn = _rmsnorm([mem.reshape(b * n_mem, d)], p["norm_mem"], BF16)
    kv = _matmul([memn], p["w_kv"], BF16)
    o = _attention(q.reshape(b, t, d), kv.reshape(b, n_mem, 2 * d))
    x2 = _matmul([o.reshape(m, d)], p["w_o"], F32, residual=x1)

    return _conv_ffn(x2.reshape(b, t, d), p["norm_ffn"], p["norm_out"], p["w_ffn_up"], p["conv_w"],
                     p["conv_b"], p["w_ffn_down"])


def kernel(x_prompt, x_sample, mem_prompt, mem_sample, norm_mix, w_in, mu_shift, ln_v_g, ln_v_b, w_s, b_s, w0, w_up_decay, a0, w_up_iclr, w_up_gate, k_k, k_a, r_k, gn_g, gn_b, w_out, norm_x, norm_mem, w_q, w_kv, w_o, norm_ffn, w_ffn_up, conv_w, conv_b, w_ffn_down, norm_out):
    depth = w_in.shape[0]
    n = r_k.shape[-1]
    head_id = jnp.arange(LANES) // n
    jmat = (head_id[:, None] == head_id[None, :]).astype(BF16)

    layers = []
    for l in range(depth):
        layers.append(dict(
            norm_mix=norm_mix[l], w_in=w_in[l].astype(BF16), mu_shift=mu_shift[l],
            ln_v_g=ln_v_g[l], ln_v_b=ln_v_b[l], w_s=w_s[l], b_s=b_s[l],
            w0=w0[l], w_up_decay=w_up_decay[l], a0=a0[l], w_up_iclr=w_up_iclr[l], w_up_gate=w_up_gate[l],
            k_k=k_k[l], k_a=k_a[l], r_k=r_k[l], gn_g=gn_g[l], gn_b=gn_b[l],
            w_out=w_out[l].astype(BF16), norm_x=norm_x[l], norm_mem=norm_mem[l],
            w_q=w_q[l].astype(BF16), w_kv=w_kv[l].astype(BF16), w_o=w_o[l].astype(BF16),
            norm_ffn=norm_ffn[l], w_ffn_up=w_ffn_up[l].astype(BF16), conv_w=conv_w[l], conv_b=conv_b[l],
            w_ffn_down=w_ffn_down[l].astype(BF16), jmat=jmat))

    (p,) = layers
    p = dict(p, norm_out=norm_out)
    return _trunk(x_prompt, mem_prompt, p), _trunk(x_sample, mem_sample, p)
```

```python
import functools

import jax
import jax.numpy as jnp
from jax import lax
from jax.experimental import pallas as pl
from jax.experimental.pallas import tpu as pltpu

RMS_EPS = 1e-6
LN_EPS = 1e-5
GN_EPS = 64e-5
L2_EPS = 1e-12
LOGW_SCALE = 0.6065306597126334
X_HEADS = 4

LANES = 128
BF16_ROWS = 16
SCAN_CHUNK = 64
SCAN_HEADS = 32
NORM_ROWS = 128
INV_BASE = 8
VMEM_LIMIT = 60 * 1024 * 1024

F32 = jnp.float32
BF16 = jnp.bfloat16


def _params(*sem):
    return pltpu.CompilerParams(dimension_semantics=sem, vmem_limit_bytes=VMEM_LIMIT)


def _pick(n, candidates):
    for c in candidates:
        if n % c == 0:
            return c
    raise ValueError(f"no tile for {n} in {candidates}")


def _dot(a, b):
    return jnp.dot(a, b, preferred_element_type=F32)


NN = (((1,), (0,)), ((), ()))
NT = (((1,), (1,)), ((), ()))
TN = (((0,), (0,)), ((), ()))


def _rmsnorm_body(*refs, n_in):
    g_ref, o_ref = refs[n_in], refs[n_in + 1]
    x = refs[0][...]
    for r in refs[1:n_in]:
        x = x + r[...]
    ms = jnp.mean(x * x, axis=-1, keepdims=True)
    o_ref[...] = (x * lax.rsqrt(ms + RMS_EPS) * g_ref[...]).astype(o_ref.dtype)


def _rmsnorm(xs, g, out_dtype):
    m, d = xs[0].shape
    tm = _pick(m, (512, 256, 128, 64, 32, 16, 8))
    row = pl.BlockSpec((tm, d), lambda i: (i, 0))
    return pl.pallas_call(
        functools.partial(_rmsnorm_body, n_in=len(xs)),
        out_shape=jax.ShapeDtypeStruct((m, d), out_dtype),
        grid=(m // tm,),
        in_specs=[row] * len(xs) + [pl.BlockSpec((1, d), lambda i: (0, 0))],
        out_specs=row,
        compiler_params=_params("parallel"),
        name="rmsnorm",
    )(*xs, g.reshape(1, d))


def _mm_body(*refs, n_lhs, has_res):
    a_refs, w_refs = refs[:n_lhs], refs[n_lhs:2 * n_lhs]
    o_ref = refs[-1]
    acc = _dot(a_refs[0][...], w_refs[0][...])
    for a, w in zip(a_refs[1:], w_refs[1:]):
        acc = acc + _dot(a[...], w[...])
    if has_res:
        acc = acc + refs[2 * n_lhs][...]
    o_ref[...] = acc.astype(o_ref.dtype)


def _matmul(lhs, w, out_dtype, residual=None, col0=0, n=None):
    m = lhs[0].shape[0]
    n = w.shape[1] if n is None else n
    tm = _pick(m, (1024, 512, 256, 128, 64, 32, 16, 8))
    tn = _pick(n, (1024, 768, 512, 256, 128))
    if col0 % tn:
        w, col0 = w[:, col0:col0 + n], 0
    cb = col0 // tn
    in_specs = [pl.BlockSpec((tm, a.shape[1]), lambda i, j: (i, 0)) for a in lhs]
    in_specs += [pl.BlockSpec((a.shape[1], tn), lambda i, j, rb=rb: (rb, cb + j)) for rb, a in enumerate(lhs)]
    assert all(a.shape[1] == lhs[0].shape[1] for a in lhs) and len(lhs) * lhs[0].shape[1] == w.shape[0]
    args = list(lhs) + [w] * len(lhs)
    if residual is not None:
        in_specs.append(pl.BlockSpec((tm, tn), lambda i, j: (i, j)))
        args.append(residual)
    return pl.pallas_call(
        functools.partial(_mm_body, n_lhs=len(lhs), has_res=residual is not None),
        out_shape=jax.ShapeDtypeStruct((m, n), out_dtype),
        grid=(m // tm, n // tn),
        in_specs=in_specs,
        out_specs=pl.BlockSpec((tm, tn), lambda i, j: (i, j)),
        compiler_params=_params("parallel", "parallel"),
        name="matmul",
    )(*args)


def _gmlp_body(zu_ref, zv_ref, g_ref, b_ref, ws_ref, bs_ref, o_ref, *, chunk):
    u = jax.nn.gelu(zu_ref[...])
    v = jax.nn.gelu(zv_ref[...])
    mu = jnp.mean(v, axis=-1, keepdims=True)
    var = jnp.mean(jnp.square(v - mu), axis=-1, keepdims=True)
    vn = ((v - mu) * lax.rsqrt(var + LN_EPS) * g_ref[...] + b_ref[...]).astype(BF16)
    rows, width = u.shape
    heads = ws_ref.shape[0]
    hd = width // heads
    bs = bs_ref[...]
    for c in range(rows // chunk):
        rs = slice(c * chunk, (c + 1) * chunk)
        mixed = jnp.concatenate(
            [_dot(ws_ref[h], vn[rs, h * hd:(h + 1) * hd]) for h in range(heads)], axis=1)
        o_ref[rs, :] = (u[rs, :] * (mixed + bs)).astype(o_ref.dtype)


def _gmlp(za, ln_g, ln_b, w_s, b_s):
    m, two_a = za.shape
    mix_a = two_a // 2
    heads, chunk, _ = w_s.shape
    rows = _pick(m, (4 * chunk, 2 * chunk, chunk))
    bs_full = jnp.repeat(b_s.T, mix_a // heads, axis=1)
    vec = pl.BlockSpec((1, mix_a), lambda i: (0, 0))
    return pl.pallas_call(
        functools.partial(_gmlp_body, chunk=chunk),
        out_shape=jax.ShapeDtypeStruct((m, mix_a), BF16),
        grid=(m // rows,),
        in_specs=[pl.BlockSpec((rows, mix_a), lambda i: (i, 0)),
                  pl.BlockSpec((rows, mix_a), lambda i: (i, 1)),
                  vec, vec,
                  pl.BlockSpec((heads, chunk, chunk), lambda i: (0, 0, 0)),
                  pl.BlockSpec((chunk, mix_a), lambda i: (0, 0))],
        out_specs=pl.BlockSpec((rows, mix_a), lambda i: (i, 0)),
        compiler_params=_params("parallel"),
        name="gmlp",
    )(za, za, ln_g.reshape(1, mix_a), ln_b.reshape(1, mix_a), w_s.astype(BF16), bs_full)


def _headsum(x, j_ref):
    tm, w = x.shape
    nb = w // LANES
    xs = jnp.concatenate([x[:, i * LANES:(i + 1) * LANES] for i in range(nb)], axis=0)
    s = _dot(xs.astype(BF16), j_ref[...])
    return jnp.concatenate([s[i * tm:(i + 1) * tm] for i in range(nb)], axis=1)


def _tshift(z, zp, zn, mu, first, last):
    tm = z.shape[0]
    rows = lax.broadcasted_iota(jnp.int32, (8, 1), 0)
    prev_row = jnp.where(first, 0.0, zp[7:8, :])
    next_row = jnp.where(last, 0.0, zn[0:1, :])
    prev = pltpu.roll(z, 1, 0)
    nxt = pltpu.roll(z, tm - 1, 0)
    prev = jnp.concatenate([jnp.where(rows == 0, prev_row, prev[:8]), prev[8:]], axis=0)
    nxt = jnp.concatenate([nxt[:tm - 8], jnp.where(rows == 7, next_row, nxt[tm - 8:])], axis=0)
    return z * (1.0 - mu) + (0.5 * mu) * (prev + nxt)


def _prep_body(zm, zmp, zmn, zl, zlp, zln, mum, mul, w0_ref, wd_ref, a0_ref, wi_ref, wg_ref,
               kk_ref, ka_ref, rk_ref, j_ref,
               r_out, v_out, kk_out, g_out, bonus_out, lw_out, kd_out, a_out,
               *, lora_w, lora_a):
    i = pl.program_id(1)
    first = i == 0
    last = i == pl.num_programs(1) - 1
    mix_b = r_out.shape[-1]

    def group(c):
        sl = slice(c * mix_b, (c + 1) * mix_b)
        return _tshift(zm[0, :, sl], zmp[0, :, sl], zmn[0, :, sl], mum[:, sl], first, last)

    r, k, v = group(0), group(1), group(2)
    lo = _tshift(zl[0], zlp[0], zln[0], mul[...], first, last)
    xw = lo[:, :lora_w]
    xa = lo[:, lora_w:lora_w + lora_a]
    xg = lo[:, lora_w + lora_a:]

    g_out[0] = _dot(jax.nn.sigmoid(xg).astype(BF16), wg_ref[...]).astype(g_out.dtype)
    kkr = k * kk_ref[...]
    ss = _headsum(kkr * kkr, j_ref)
    kk = kkr * lax.rsqrt(jnp.maximum(ss, L2_EPS))
    tw = jnp.tanh(xw).astype(BF16)
    xab = xa.astype(BF16)
    rk = None
    for d in range(2):
        lw_out[d, 0] = -LOGW_SCALE * jax.nn.sigmoid(w0_ref[d:d + 1, :] + _dot(tw, wd_ref[d]))
        a = jax.nn.sigmoid(a0_ref[d:d + 1, :] + _dot(xab, wi_ref[d]))
        kd = k * (1.0 + (a - 1.0) * ka_ref[...])
        a_out[d, 0] = a.astype(a_out.dtype)
        kd_out[d, 0] = kd.astype(kd_out.dtype)
        s = _headsum(r * kd * rk_ref[...], j_ref)
        rk = s if rk is None else rk + s
    r_out[0] = r.astype(r_out.dtype)
    v_out[0] = v.astype(v_out.dtype)
    kk_out[0] = kk.astype(kk_out.dtype)
    bonus_out[0] = (rk * v).astype(bonus_out.dtype)


def _rwkv_prep(zrkv, zl, mu, w0, w_up_decay, a0, w_up_iclr, w_up_gate, k_k, k_a, r_k, jmat):
    b, t, three_b = zrkv.shape
    mix_b = three_b // 3
    nl = zl.shape[-1]
    lora_w, lora_a = w_up_decay.shape[1], w_up_iclr.shape[1]
    tm = _pick(t, (128, 64, 32, 16, 8))
    hb = tm // 8
    nblk8 = t // 8

    def main(w, col):
        return pl.BlockSpec((1, tm, w), lambda bi, i: (bi, i, col))

    def prev(w, col):
        return pl.BlockSpec((1, 8, w), lambda bi, i: (bi, jnp.maximum(i * hb - 1, 0), col))

    def nxt(w, col):
        return pl.BlockSpec((1, 8, w), lambda bi, i: (bi, jnp.minimum((i + 1) * hb, nblk8 - 1), col))

    def const(shape):
        return pl.BlockSpec(shape, lambda bi, i: (0,) * len(shape))

    in_specs = [main(three_b, 0), prev(three_b, 0), nxt(three_b, 0), main(nl, 0), prev(nl, 0), nxt(nl, 0)]
    args = [zrkv] * 3 + [zl] * 3
    mu2 = mu.reshape(1, -1)
    in_specs += [const((1, three_b)), const((1, nl))]
    args += [mu2[:, :three_b], mu2[:, three_b:]]
    consts = [w0, w_up_decay.astype(BF16), a0, w_up_iclr.astype(BF16), w_up_gate.astype(BF16),
              k_k.reshape(1, mix_b), k_a.reshape(1, mix_b), r_k.reshape(1, mix_b), jmat]
    in_specs += [const(c.shape) for c in consts]
    args += consts

    shared = jax.ShapeDtypeStruct((b, t, mix_b), BF16)
    perdir = lambda dt: jax.ShapeDtypeStruct((2, b, t, mix_b), dt)
    o_shared = pl.BlockSpec((1, tm, mix_b), lambda bi, i: (bi, i, 0))
    o_perdir = pl.BlockSpec((2, 1, tm, mix_b), lambda bi, i: (0, bi, i, 0))
    return pl.pallas_call(
        functools.partial(_prep_body, lora_w=lora_w, lora_a=lora_a),
        out_shape=[shared] * 5 + [perdir(F32), perdir(BF16), perdir(BF16)],
        grid=(b, t // tm),
        in_specs=in_specs,
        out_specs=[o_shared] * 5 + [o_perdir] * 3,
        compiler_params=_params("parallel", "parallel"),
        name="rwkv_prep",
    )(*args)


def _mm(a, b, dims):
    return lax.dot_general(a.astype(BF16), b.astype(BF16), dims, preferred_element_type=F32)


def _halves(x, size):
    blocks = x.shape[0] // (2 * size)
    lo = jnp.concatenate([x[2 * j * size:(2 * j + 1) * size] for j in range(blocks)], axis=0)
    hi = jnp.concatenate([x[(2 * j + 1) * size:(2 * j + 2) * size] for j in range(blocks)], axis=0)
    return lo, hi


def _interleave(lo, hi, size):
    pieces = []
    for j in range(lo.shape[0] // size):
        pieces += [lo[j * size:(j + 1) * size], hi[j * size:(j + 1) * size]]
    return jnp.concatenate(pieces, axis=0)


def _tri_inverse(ls, limit, reverse):
    rows = ls[0].shape[0]
    row = lax.broadcasted_iota(jnp.int32, (rows, rows), 0)
    col = lax.broadcasted_iota(jnp.int32, (rows, rows), 1)
    eye = (row == col).astype(F32)
    sh = INV_BASE.bit_length() - 1
    base = (row >> sh) == (col >> sh)
    ps = [jnp.where(base, l, 0.0) for l in ls]
    ts = [jnp.where(row == col, 1.0, p) for p in ps]
    size = 2
    while size < INV_BASE:
        ps = [_mm(p, p, NN) for p in ps]
        ts = [_mm(t, eye + p, NN) for t, p in zip(ts, ps)]
        size *= 2
    size = INV_BASE
    hrow = lax.broadcasted_iota(jnp.int32, (rows // 2, rows), 0)
    hcol = lax.broadcasted_iota(jnp.int32, (rows // 2, rows), 1)
    zero = jnp.zeros((rows // 2, rows), F32)
    while size < limit:
        sh = size.bit_length() - 1
        partner = ((hcol >> (sh + 1)) == (hrow >> sh)) & (((hcol >> sh) & 1) == (1 if reverse else 0))
        act = 0 if reverse else 1
        l_act = [_halves(l, size)[act] for l in ls]
        t_halves = [_halves(t, size) for t in ts]
        xs = [_mm(jnp.where(partner, la, 0.0), t, NN) for la, t in zip(l_act, ts)]
        xs = [_interleave(x, zero, size) if reverse else _interleave(zero, x, size) for x in xs]
        upd = [th[act] + _mm(th[act], x, NN) for th, x in zip(t_halves, xs)]
        ts = [_interleave(u, th[1], size) if reverse else _interleave(th[0], u, size)
              for th, u in zip(t_halves, upd)]
        size *= 2
    return ts


def _scan_body(r_ref, v_ref, kk_ref, lw_ref, kd_ref, a_ref, y_ref, s_ref, *, n, reverse):
    c = pl.program_id(2)
    cs = r_ref.shape[1]

    @pl.when(c == 0)
    def _():
        s_ref[...] = jnp.zeros_like(s_ref)

    sgn = -1 if reverse else 1
    tiles = r_ref.shape[2] // LANES
    hp = LANES // n
    rows = hp * cs

    lw = lw_ref[0, 0]
    rc = lax.broadcasted_iota(jnp.int32, (cs, cs), 0)
    cc = lax.broadcasted_iota(jnp.int32, (cs, cs), 1)
    tri = ((rc - cc) * sgn >= 0).astype(BF16)
    l_hi = lw.astype(BF16)
    l_mid = (lw - l_hi.astype(F32)).astype(BF16)
    l_lo = (lw - l_hi.astype(F32) - l_mid.astype(F32)).astype(BF16)
    cum = _dot(tri, l_hi) + (_dot(tri, l_mid) + _dot(tri, l_lo))
    tot = cum[0:1, :] if reverse else cum[cs - 1:cs, :]
    e_tot = jnp.exp(tot)

    row = lax.broadcasted_iota(jnp.int32, (rows, rows), 0)
    col = lax.broadcasted_iota(jnp.int32, (rows, rows), 1)
    order = ((row & (cs - 1)) - (col & (cs - 1))) * sgn
    strict = order > 0
    incl = order >= 0
    eye_l = (lax.broadcasted_iota(jnp.int32, (LANES, LANES), 0)
             == lax.broadcasted_iota(jnp.int32, (LANES, LANES), 1))
    lane_head = lax.broadcasted_iota(jnp.int32, (cs, LANES), 1) // n
    head_masks = [lane_head == h for h in range(hp)]
    zero_t = jnp.zeros((rows, LANES), BF16)

    def pack(x):
        return jnp.concatenate([jnp.where(mk, x, jnp.zeros_like(x)) for mk in head_masks], axis=0)

    tl = range(tiles)
    cat = jnp.concatenate
    at, rt, bh, kh, vv, aa = [], [], [], [], [], []
    for p in tl:
        sl = slice(p * LANES, (p + 1) * LANES)
        cum_p, lw_p, tot_p = cum[:, sl], lw[:, sl], tot[:, sl]
        kk_p = kk_ref[0, :, sl].astype(F32)
        kd_p = kd_ref[0, 0, :, sl].astype(F32)
        b_p = kk_p * a_ref[0, 0, :, sl].astype(F32)
        e_neg = jnp.exp(-cum_p)
        e_hat = jnp.exp(tot_p - cum_p)
        at.append(pack(-kk_p * jnp.exp(cum_p - lw_p)))
        rt.append(pack(r_ref[0, :, sl].astype(F32) * jnp.exp(cum_p)))
        bh.append(pack((b_p * e_hat).astype(BF16)))
        kh.append(pack((kd_p * e_hat).astype(BF16)))
        vv.append(pack(v_ref[0, :, sl].astype(BF16)))
        bk = cat([pack((b_p * e_neg).astype(BF16)), pack((kd_p * e_neg).astype(BF16))], axis=0)
        aa.append(_mm(cat([at[p], rt[p]], axis=0), bk, NT).astype(BF16))
    aab = [jnp.where(strict, x[:rows, :rows], 0.0) for x in aa]
    aak = [jnp.where(strict, x[:rows, rows:], 0.0) for x in aa]
    aq = [cat([jnp.where(incl, x[rows:, :rows], 0.0), jnp.where(incl, x[rows:, rows:], 0.0)], axis=1)
          for x in aa]
    tinv = _tri_inverse(aab, cs, reverse)
    akv = [_mm(aak[p], vv[p], NN) for p in tl]
    wu = [_mm(tinv[p], cat([at[p], akv[p]], axis=1), NN) for p in tl]
    rhs = [cat([wu[p].astype(BF16), cat([zero_t, vv[p]], axis=1)], axis=0) for p in tl]
    qy = [_mm(aq[p], rhs[p], NN) for p in tl]
    mn = [_mm(cat([bh[p], kh[p]], axis=0), rhs[p], TN) for p in tl]
    lhs = []
    for p in tl:
        m_mat = jnp.where(eye_l, e_tot[:, p * LANES:(p + 1) * LANES], 0.0) + mn[p][:, :LANES]
        lhs.append(cat([m_mat, rt[p] + qy[p][:, :LANES]], axis=0))
    ms = [_mm(lhs[p], s_ref[p], NN) for p in tl]
    outs = []
    for p in tl:
        s_ref[p] = ms[p][:LANES] + mn[p][:, LANES:]
        ybd = ms[p][LANES:] + qy[p][:, LANES:]
        y = ybd[:cs]
        for h in range(1, hp):
            y = y + ybd[h * cs:(h + 1) * cs]
        outs.append(y)
    y_ref[0] = cat(outs, axis=1)


def _wkv_scan(r, v, kk, lw, kd, a, n, reverse):
    b, t, mix_b = r.shape
    heads = mix_b // n
    hg = SCAN_HEADS if heads % SCAN_HEADS == 0 else heads
    cs = SCAN_CHUNK
    nc = t // cs
    w = hg * n
    di = 1 if reverse else 0
    tchunk = (lambda ci: nc - 1 - ci) if reverse else (lambda ci: ci)
    shared = pl.BlockSpec((1, cs, w), lambda bi, hi, ci: (bi, tchunk(ci), hi))
    perdir = pl.BlockSpec((1, 1, cs, w), lambda bi, hi, ci: (di, bi, tchunk(ci), hi))
    return pl.pallas_call(
        functools.partial(_scan_body, n=n, reverse=reverse),
        out_shape=jax.ShapeDtypeStruct((b, t, mix_b), F32),
        grid=(b, heads // hg, nc),
        in_specs=[shared] * 3 + [perdir] * 3,
        out_specs=shared,
        scratch_shapes=[pltpu.VMEM((w // LANES, LANES, LANES), F32)],
        compiler_params=_params("parallel", "parallel", "arbitrary"),
        name="wkv_scan",
    )(r, v, kk, lw, kd, a)


def _post_body(yf_ref, yb_ref, bonus_ref, g_ref, gg_ref, gb_ref, j_ref, o_ref, *, n):
    y = yf_ref[...] + yb_ref[...]
    mean = _headsum(y, j_ref) * (1.0 / n)
    yc = y - mean
    var = _headsum(yc * yc, j_ref) * (1.0 / n)
    yn = yc * lax.rsqrt(var + GN_EPS) * gg_ref[...] + gb_ref[...]
    o_ref[...] = ((yn + bonus_ref[...]) * g_ref[...]).astype(o_ref.dtype)


def _rwkv_post(y_fwd, y_bwd, bonus, g, gn_g, gn_b, jmat, n):
    m, mix_b = y_fwd.shape
    tm = _pick(m, (512, 256, 128, 64, 32, 16, 8))
    row = pl.BlockSpec((tm, mix_b), lambda i: (i, 0))
    vec = pl.BlockSpec((1, mix_b), lambda i: (0, 0))
    return pl.pallas_call(
        functools.partial(_post_body, n=n),
        out_shape=jax.ShapeDtypeStruct((m, mix_b), BF16),
        grid=(m // tm,),
        in_specs=[row, row, row, row, vec, vec, pl.BlockSpec(jmat.shape, lambda i: (0, 0))],
        out_specs=row,
        compiler_params=_params("parallel"),
        name="rwkv_post",
    )(y_fwd, y_bwd, bonus, g, gn_g.reshape(1, mix_b), gn_b.reshape(1, mix_b), jmat)


def _attn_body(q_ref, kv_ref, o_ref, *, heads):
    d = q_ref.shape[-1]
    hd = d // heads
    scale = hd ** -0.5
    for h in range(heads):
        q = q_ref[0, :, h * hd:(h + 1) * hd]
        k = kv_ref[0, :, h * hd:(h + 1) * hd]
        v = kv_ref[0, :, d + h * hd:d + (h + 1) * hd]
        s = lax.dot_general(q, k, NT, preferred_element_type=F32) * scale
        e = jnp.exp(s - jnp.max(s, axis=-1, keepdims=True))
        p = e / jnp.sum(e, axis=-1, keepdims=True)
        o_ref[0, :, h * hd:(h + 1) * hd] = _dot(p.astype(BF16), v).astype(o_ref.dtype)


def _attention(q, kv):
    b, t, d = q.shape
    n_mem = kv.shape[1]
    tq = _pick(t, (1024, 512, 256, 128, 64, 32, 16))
    return pl.pallas_call(
        functools.partial(_attn_body, heads=X_HEADS),
        out_shape=jax.ShapeDtypeStruct((b, t, d), BF16),
        grid=(b, t // tq),
        in_specs=[pl.BlockSpec((1, tq, d), lambda bi, i: (bi, i, 0)),
                  pl.BlockSpec((1, n_mem, 2 * d), lambda bi, i: (bi, 0, 0))],
        out_specs=pl.BlockSpec((1, tq, d), lambda bi, i: (bi, i, 0)),
        compiler_params=_params("parallel", "parallel"),
        name="cross_attention",
    )(q, kv)


def _rms(x, g):
    return x * lax.rsqrt(jnp.mean(x * x, axis=-1, keepdims=True) + RMS_EPS) * g


def _ffn_body(x_ref, xp_ref, xn_ref, gin_ref, gout_ref, wg_ref, wv_ref, cwg_ref, cwv_ref, cbg_ref,
              cbv_ref, wd_ref, o_ref, hx_ref):
    i = pl.program_id(1)
    f = pl.program_id(2)
    tm = x_ref.shape[1]
    halo = BF16_ROWS

    @pl.when(f == 0)
    def _():
        g = gin_ref[...]
        zero = jnp.zeros((halo, x_ref.shape[2]), BF16)
        hx_ref[0:halo, :] = jnp.where(i == 0, zero, _rms(xp_ref[0], g).astype(BF16))
        hx_ref[halo + tm:, :] = jnp.where(i == pl.num_programs(1) - 1, zero, _rms(xn_ref[0], g).astype(BF16))

        def norm_rows(c, carry):
            r0 = pl.multiple_of(c * NORM_ROWS, NORM_ROWS)
            hx_ref[pl.ds(halo + r0, NORM_ROWS), :] = _rms(x_ref[0, pl.ds(r0, NORM_ROWS), :], g).astype(BF16)
            o_ref[0, pl.ds(r0, NORM_ROWS), :] = jnp.zeros((NORM_ROWS, x_ref.shape[2]), F32)
            return carry

        lax.fori_loop(0, tm // NORM_ROWS, norm_rows, 0)

    hx = hx_ref[...]
    rows = tm + 2 * halo

    def conv(w_ref, cw_ref, cb_ref):
        z = _dot(hx, w_ref[...])
        prev = pltpu.roll(z, 1, 0)[halo:halo + tm]
        nxt = pltpu.roll(z, rows - 1, 0)[halo:halo + tm]
        cw = cw_ref[...]
        return cw[0:1] * prev + cw[1:2] * z[halo:halo + tm] + cw[2:3] * nxt + cb_ref[...]

    gate = conv(wg_ref, cwg_ref, cbg_ref)
    val = conv(wv_ref, cwv_ref, cbv_ref)
    act = (jax.nn.silu(gate) * val).astype(BF16)
    o_ref[0] += _dot(act, wd_ref[...])

    @pl.when(f == pl.num_programs(2) - 1)
    def _():
        g = gout_ref[...]

        def out_rows(c, carry):
            rs = pl.ds(pl.multiple_of(c * NORM_ROWS, NORM_ROWS), NORM_ROWS)
            o_ref[0, rs, :] = _rms(x_ref[0, rs, :] + o_ref[0, rs, :], g)
            return carry

        lax.fori_loop(0, tm // NORM_ROWS, out_rows, 0)


def _conv_ffn(x, g_in, g_out, w_up, conv_w, conv_b, w_down):
    b, t, d = x.shape
    d_ff = w_down.shape[0]
    tm = _pick(t, (1024, 512, 256, 128))
    tf = _pick(d_ff, (256, 128))
    nf = d_ff // tf
    hb = tm // BF16_ROWS
    nblk = t // BF16_ROWS
    cb = conv_b.reshape(1, 2 * d_ff)
    once = pl.Buffered(1)
    return pl.pallas_call(
        _ffn_body,
        out_shape=jax.ShapeDtypeStruct((b, t, d), F32),
        grid=(b, t // tm, nf),
        in_specs=[
            pl.BlockSpec((1, tm, d), lambda bi, i, f: (bi, i, 0), pipeline_mode=once),
            pl.BlockSpec((1, BF16_ROWS, d), lambda bi, i, f: (bi, jnp.maximum(i * hb - 1, 0), 0)),
            pl.BlockSpec((1, BF16_ROWS, d), lambda bi, i, f: (bi, jnp.minimum((i + 1) * hb, nblk - 1), 0)),
            pl.BlockSpec((1, d), lambda bi, i, f: (0, 0)),
            pl.BlockSpec((1, d), lambda bi, i, f: (0, 0)),
            pl.BlockSpec((d, tf), lambda bi, i, f: (0, f)),
            pl.BlockSpec((d, tf), lambda bi, i, f: (0, nf + f)),
            pl.BlockSpec((3, tf), lambda bi, i, f: (0, f)),
            pl.BlockSpec((3, tf), lambda bi, i, f: (0, nf + f)),
            pl.BlockSpec((1, tf), lambda bi, i, f: (0, f)),
            pl.BlockSpec((1, tf), lambda bi, i, f: (0, nf + f)),
            pl.BlockSpec((tf, d), lambda bi, i, f: (f, 0)),
        ],
        out_specs=pl.BlockSpec((1, tm, d), lambda bi, i, f: (bi, i, 0), pipeline_mode=once),
        scratch_shapes=[pltpu.VMEM((tm + 2 * BF16_ROWS, d), BF16)],
        compiler_params=_params("parallel", "parallel", "arbitrary"),
        name="conv_ffn",
    )(x, x, x, g_in.reshape(1, d), g_out.reshape(1, d), w_up, w_up, conv_w, conv_w, cb, cb, w_down)


def _trunk(x, mem, p):
    b, t, d = x.shape
    m = b * t
    n_mem = mem.shape[1]
    mix_a = p["ln_v_g"].shape[-1]
    mix_b = p["k_k"].shape[-1]
    n = p["r_k"].shape[-1]
    x2d = x.reshape(m, d)

    h1 = _rmsnorm([x2d], p["norm_mix"], BF16)
    za = _matmul([h1], p["w_in"], F32, col0=0, n=2 * mix_a)
    zrkv = _matmul([h1], p["w_in"], F32, col0=2 * mix_a, n=3 * mix_b)
    zl = _matmul([h1], p["w_in"], F32, col0=2 * mix_a + 3 * mix_b, n=p["w_in"].shape[1] - 2 * mix_a - 3 * mix_b)

    ya = _gmlp(za, p["ln_v_g"], p["ln_v_b"], p["w_s"], p["b_s"])

    r, v, kk, g, bonus, lw, kd, a = _rwkv_prep(
        zrkv.reshape(b, t, 3 * mix_b), zl.reshape(b, t, -1), p["mu_shift"], p["w0"], p["w_up_decay"],
        p["a0"], p["w_up_iclr"], p["w_up_gate"], p["k_k"], p["k_a"], p["r_k"], p["jmat"])
    y_fwd = _wkv_scan(r, v, kk, lw, kd, a, n, reverse=False)
    y_bwd = _wkv_scan(r, v, kk, lw, kd, a, n, reverse=True)
    yb = _rwkv_post(y_fwd.reshape(m, mix_b), y_bwd.reshape(m, mix_b), bonus.reshape(m, mix_b),
                    g.reshape(m, mix_b), p["gn_g"], p["gn_b"], p["jmat"], n)

    x1 = _matmul([ya, yb], p["w_out"], F32, residual=x2d)

    hq = _rmsnorm([x1], p["norm_x"], BF16)
    q = _matmul([hq], p["w_q"], BF16)
    memn = _rmsnorm([mem.reshape(b * n_mem, d)], p["norm_mem"], BF16)
    kv = _matmul([memn], p["w_kv"], BF16)
    o = _attention(q.reshape(b, t, d), kv.reshape(b, n_mem, 2 * d))
    x2 = _matmul([o.reshape(m, d)], p["w_o"], F32, residual=x1)

    return _conv_ffn(x2.reshape(b, t, d), p["norm_ffn"], p["norm_out"], p["w_ffn_up"], p["conv_w"],
                     p["conv_b"], p["w_ffn_down"])


def kernel(x_prompt, x_sample, mem_prompt, mem_sample, norm_mix, w_in, mu_shift, ln_v_g, ln_v_b, w_s, b_s, w0, w_up_decay, a0, w_up_iclr, w_up_gate, k_k, k_a, r_k, gn_g, gn_b, w_out, norm_x, norm_mem, w_q, w_kv, w_o, norm_ffn, w_ffn_up, conv_w, conv_b, w_ffn_down, norm_out):
    depth = w_in.shape[0]
    n = r_k.shape[-1]
    head_id = jnp.arange(LANES) // n
    jmat = (head_id[:, None] == head_id[None, :]).astype(BF16)

    layers = []
    for l in range(depth):
        layers.append(dict(
            norm_mix=norm_mix[l], w_in=w_in[l].astype(BF16), mu_shift=mu_shift[l],
            ln_v_g=ln_v_g[l], ln_v_b=ln_v_b[l], w_s=w_s[l], b_s=b_s[l],
            w0=w0[l], w_up_decay=w_up_decay[l], a0=a0[l], w_up_iclr=w_up_iclr[l], w_up_gate=w_up_gate[l],
            k_k=k_k[l], k_a=k_a[l], r_k=r_k[l], gn_g=gn_g[l], gn_b=gn_b[l],
            w_out=w_out[l].astype(BF16), norm_x=norm_x[l], norm_mem=norm_mem[l],
            w_q=w_q[l].astype(BF16), w_kv=w_kv[l].astype(BF16), w_o=w_o[l].astype(BF16),
            norm_ffn=norm_ffn[l], w_ffn_up=w_ffn_up[l].astype(BF16), conv_w=conv_w[l], conv_b=conv_b[l],
            w_ffn_down=w_ffn_down[l].astype(BF16), jmat=jmat))

    (p,) = layers
    p = dict(p, norm_out=norm_out)
    return _trunk(x_prompt, mem_prompt, p), _trunk(x_sample, mem_sample, p)
```

```python
import functools

import jax
import jax.numpy as jnp
from jax import lax
from jax.experimental import pallas as pl
from jax.experimental.pallas import tpu as pltpu

RMS_EPS = 1e-6
LN_EPS = 1e-5
GN_EPS = 64e-5
L2_EPS = 1e-12
LOGW_SCALE = 0.6065306597126334
X_HEADS = 4

LANES = 128
BF16_ROWS = 16
SCAN_CHUNK = 64
SCAN_HEADS = 32
NORM_ROWS = 128
INV_BASE = 8
VMEM_LIMIT = 60 * 1024 * 1024

F32 = jnp.float32
BF16 = jnp.bfloat16


def _params(*sem, flags=None):
    return pltpu.CompilerParams(dimension_semantics=sem, vmem_limit_bytes=VMEM_LIMIT, flags=flags)


def _pick(n, candidates):
    for c in candidates:
        if n % c == 0:
            return c
    raise ValueError(f"no tile for {n} in {candidates}")


def _dot(a, b):
    return jnp.dot(a, b, preferred_element_type=F32)


NN = (((1,), (0,)), ((), ()))
NT = (((1,), (1,)), ((), ()))
TN = (((0,), (0,)), ((), ()))


def _rmsnorm_body(*refs, n_in):
    g_ref, o_ref = refs[n_in], refs[n_in + 1]
    x = refs[0][...]
    for r in refs[1:n_in]:
        x = x + r[...]
    ms = jnp.mean(x * x, axis=-1, keepdims=True)
    o_ref[...] = (x * lax.rsqrt(ms + RMS_EPS) * g_ref[...]).astype(o_ref.dtype)


def _rmsnorm(xs, g, out_dtype):
    m, d = xs[0].shape
    tm = _pick(m, (512, 256, 128, 64, 32, 16, 8))
    row = pl.BlockSpec((tm, d), lambda i: (i, 0))
    return pl.pallas_call(
        functools.partial(_rmsnorm_body, n_in=len(xs)),
        out_shape=jax.ShapeDtypeStruct((m, d), out_dtype),
        grid=(m // tm,),
        in_specs=[row] * len(xs) + [pl.BlockSpec((1, d), lambda i: (0, 0))],
        out_specs=row,
        compiler_params=_params("parallel"),
        name="rmsnorm",
    )(*xs, g.reshape(1, d))


def _mm_body(*refs, n_lhs, has_res):
    a_refs, w_refs = refs[:n_lhs], refs[n_lhs:2 * n_lhs]
    o_ref = refs[-1]
    acc = _dot(a_refs[0][...], w_refs[0][...])
    for a, w in zip(a_refs[1:], w_refs[1:]):
        acc = acc + _dot(a[...], w[...])
    if has_res:
        acc = acc + refs[2 * n_lhs][...]
    o_ref[...] = acc.astype(o_ref.dtype)


def _matmul(lhs, w, out_dtype, residual=None, col0=0, n=None):
    m = lhs[0].shape[0]
    n = w.shape[1] if n is None else n
    tm = _pick(m, (1024, 512, 256, 128, 64, 32, 16, 8))
    tn = _pick(n, (1024, 768, 512, 256, 128))
    if col0 % tn:
        w, col0 = w[:, col0:col0 + n], 0
    cb = col0 // tn
    in_specs = [pl.BlockSpec((tm, a.shape[1]), lambda i, j: (i, 0)) for a in lhs]
    in_specs += [pl.BlockSpec((a.shape[1], tn), lambda i, j, rb=rb: (rb, cb + j)) for rb, a in enumerate(lhs)]
    assert all(a.shape[1] == lhs[0].shape[1] for a in lhs) and len(lhs) * lhs[0].shape[1] == w.shape[0]
    args = list(lhs) + [w] * len(lhs)
    if residual is not None:
        in_specs.append(pl.BlockSpec((tm, tn), lambda i, j: (i, j)))
        args.append(residual)
    return pl.pallas_call(
        functools.partial(_mm_body, n_lhs=len(lhs), has_res=residual is not None),
        out_shape=jax.ShapeDtypeStruct((m, n), out_dtype),
        grid=(m // tm, n // tn),
        in_specs=in_specs,
        out_specs=pl.BlockSpec((tm, tn), lambda i, j: (i, j)),
        compiler_params=_params("parallel", "parallel"),
        name="matmul",
    )(*args)


def _gmlp_body(zu_ref, zv_ref, g_ref, b_ref, ws_ref, bs_ref, o_ref, *, chunk):
    u = jax.nn.gelu(zu_ref[...])
    v = jax.nn.gelu(zv_ref[...])
    mu = jnp.mean(v, axis=-1, keepdims=True)
    var = jnp.mean(jnp.square(v - mu), axis=-1, keepdims=True)
    vn = ((v - mu) * lax.rsqrt(var + LN_EPS) * g_ref[...] + b_ref[...]).astype(BF16)
    rows, width = u.shape
    heads = ws_ref.shape[0]
    hd = width // heads
    bs = bs_ref[...]
    for c in range(rows // chunk):
        rs = slice(c * chunk, (c + 1) * chunk)
        mixed = jnp.concatenate(
            [_dot(ws_ref[h], vn[rs, h * hd:(h + 1) * hd]) for h in range(heads)], axis=1)
        o_ref[rs, :] = (u[rs, :] * (mixed + bs)).astype(o_ref.dtype)


def _gmlp(za, ln_g, ln_b, w_s, b_s):
    m, two_a = za.shape
    mix_a = two_a // 2
    heads, chunk, _ = w_s.shape
    rows = _pick(m, (4 * chunk, 2 * chunk, chunk))
    bs_full = jnp.repeat(b_s.T, mix_a // heads, axis=1)
    vec = pl.BlockSpec((1, mix_a), lambda i: (0, 0))
    return pl.pallas_call(
        functools.partial(_gmlp_body, chunk=chunk),
        out_shape=jax.ShapeDtypeStruct((m, mix_a), BF16),
        grid=(m // rows,),
        in_specs=[pl.BlockSpec((rows, mix_a), lambda i: (i, 0)),
                  pl.BlockSpec((rows, mix_a), lambda i: (i, 1)),
                  vec, vec,
                  pl.BlockSpec((heads, chunk, chunk), lambda i: (0, 0, 0)),
                  pl.BlockSpec((chunk, mix_a), lambda i: (0, 0))],
        out_specs=pl.BlockSpec((rows, mix_a), lambda i: (i, 0)),
        compiler_params=_params("parallel"),
        name="gmlp",
    )(za, za, ln_g.reshape(1, mix_a), ln_b.reshape(1, mix_a), w_s.astype(BF16), bs_full)


def _headsum(x, j_ref):
    tm, w = x.shape
    nb = w // LANES
    xs = jnp.concatenate([x[:, i * LANES:(i + 1) * LANES] for i in range(nb)], axis=0)
    s = _dot(xs.astype(BF16), j_ref[...])
    return jnp.concatenate([s[i * tm:(i + 1) * tm] for i in range(nb)], axis=1)


def _tshift(z, zp, zn, mu, first, last):
    tm = z.shape[0]
    rows = lax.broadcasted_iota(jnp.int32, (8, 1), 0)
    prev_row = jnp.where(first, 0.0, zp[7:8, :])
    next_row = jnp.where(last, 0.0, zn[0:1, :])
    prev = pltpu.roll(z, 1, 0)
    nxt = pltpu.roll(z, tm - 1, 0)
    prev = jnp.concatenate([jnp.where(rows == 0, prev_row, prev[:8]), prev[8:]], axis=0)
    nxt = jnp.concatenate([nxt[:tm - 8], jnp.where(rows == 7, next_row, nxt[tm - 8:])], axis=0)
    return z * (1.0 - mu) + (0.5 * mu) * (prev + nxt)


def _prep_body(zr, zrp, zrn, zk, zkp, zkn, zv, zvp, zvn, zl, zlp, zln,
               mur, muk, muv, mul, w0_ref, wd_ref, a0_ref, wi_ref, wg_ref,
               kk_ref, ka_ref, rk_ref, j_ref,
               r_out, v_out, kk_out, g_out, bonus_out, lw_out, kd_out, a_out,
               *, lora_w, lora_a):
    i = pl.program_id(1)
    first = i == 0
    last = i == pl.num_programs(1) - 1
    r = _tshift(zr[0], zrp[0], zrn[0], mur[...], first, last)
    k = _tshift(zk[0], zkp[0], zkn[0], muk[...], first, last)
    v = _tshift(zv[0], zvp[0], zvn[0], muv[...], first, last)
    lo = _tshift(zl[0], zlp[0], zln[0], mul[...], first, last)
    xw = lo[:, :lora_w]
    xa = lo[:, lora_w:lora_w + lora_a]
    xg = lo[:, lora_w + lora_a:]

    g_out[0] = _dot(jax.nn.sigmoid(xg).astype(BF16), wg_ref[...]).astype(g_out.dtype)
    kkr = k * kk_ref[...]
    ss = _headsum(kkr * kkr, j_ref)
    kk = kkr * lax.rsqrt(jnp.maximum(ss, L2_EPS))
    tw = jnp.tanh(xw).astype(BF16)
    xab = xa.astype(BF16)
    rk = None
    for d in range(2):
        lw_out[d, 0] = -LOGW_SCALE * jax.nn.sigmoid(w0_ref[d:d + 1, :] + _dot(tw, wd_ref[d]))
        a = jax.nn.sigmoid(a0_ref[d:d + 1, :] + _dot(xab, wi_ref[d]))
        kd = k * (1.0 + (a - 1.0) * ka_ref[...])
        a_out[d, 0] = a.astype(a_out.dtype)
        kd_out[d, 0] = kd.astype(kd_out.dtype)
        s = _headsum(r * kd * rk_ref[...], j_ref)
        rk = s if rk is None else rk + s
    r_out[0] = r.astype(r_out.dtype)
    v_out[0] = v.astype(v_out.dtype)
    kk_out[0] = kk.astype(kk_out.dtype)
    bonus_out[0] = (rk * v).astype(bonus_out.dtype)


def _rwkv_prep(zrkv, zl, mu, w0, w_up_decay, a0, w_up_iclr, w_up_gate, k_k, k_a, r_k, jmat):
    b, t, three_b = zrkv.shape
    mix_b = three_b // 3
    nl = zl.shape[-1]
    lora_w, lora_a = w_up_decay.shape[1], w_up_iclr.shape[1]
    tm = _pick(t, (128, 64, 32, 16, 8))
    hb = tm // 8
    nblk8 = t // 8

    def main(w, col):
        return pl.BlockSpec((1, tm, w), lambda bi, i: (bi, i, col))

    def prev(w, col):
        return pl.BlockSpec((1, 8, w), lambda bi, i: (bi, jnp.maximum(i * hb - 1, 0), col))

    def nxt(w, col):
        return pl.BlockSpec((1, 8, w), lambda bi, i: (bi, jnp.minimum((i + 1) * hb, nblk8 - 1), col))

    def const(shape):
        return pl.BlockSpec(shape, lambda bi, i: (0,) * len(shape))

    in_specs, args = [], []
    for col in range(3):
        in_specs += [main(mix_b, col), prev(mix_b, col), nxt(mix_b, col)]
        args += [zrkv] * 3
    in_specs += [main(nl, 0), prev(nl, 0), nxt(nl, 0)]
    args += [zl] * 3
    mu2 = mu.reshape(1, -1)
    in_specs += [pl.BlockSpec((1, mix_b), lambda bi, i, c=c: (0, c)) for c in range(3)]
    args += [mu2[:, :three_b]] * 3
    in_specs.append(const((1, nl)))
    args.append(mu2[:, three_b:])
    consts = [w0, w_up_decay.astype(BF16), a0, w_up_iclr.astype(BF16), w_up_gate.astype(BF16),
              k_k.reshape(1, mix_b), k_a.reshape(1, mix_b), r_k.reshape(1, mix_b), jmat]
    in_specs += [const(c.shape) for c in consts]
    args += consts

    shared = jax.ShapeDtypeStruct((b, t, mix_b), BF16)
    perdir = lambda dt: jax.ShapeDtypeStruct((2, b, t, mix_b), dt)
    o_shared = pl.BlockSpec((1, tm, mix_b), lambda bi, i: (bi, i, 0))
    o_perdir = pl.BlockSpec((2, 1, tm, mix_b), lambda bi, i: (0, bi, i, 0))
    return pl.pallas_call(
        functools.partial(_prep_body, lora_w=lora_w, lora_a=lora_a),
        out_shape=[shared] * 5 + [perdir(F32), perdir(BF16), perdir(BF16)],
        grid=(b, t // tm),
        in_specs=in_specs,
        out_specs=[o_shared] * 5 + [o_perdir] * 3,
        compiler_params=_params("parallel", "parallel"),
        name="rwkv_prep",
    )(*args)


def _mm(a, b, dims):
    return lax.dot_general(a.astype(BF16), b.astype(BF16), dims, preferred_element_type=F32)


def _halves(x, size):
    blocks = x.shape[0] // (2 * size)
    lo = jnp.concatenate([x[2 * j * size:(2 * j + 1) * size] for j in range(blocks)], axis=0)
    hi = jnp.concatenate([x[(2 * j + 1) * size:(2 * j + 2) * size] for j in range(blocks)], axis=0)
    return lo, hi


def _interleave(lo, hi, size):
    pieces = []
    for j in range(lo.shape[0] // size):
        pieces += [lo[j * size:(j + 1) * size], hi[j * size:(j + 1) * size]]
    return jnp.concatenate(pieces, axis=0)


def _tri_inverse(ls, limit, reverse):
    rows = ls[0].shape[0]
    row = lax.broadcasted_iota(jnp.int32, (rows, rows), 0)
    col = lax.broadcasted_iota(jnp.int32, (rows, rows), 1)
    eye = (row == col).astype(F32)
    sh = INV_BASE.bit_length() - 1
    base = (row >> sh) == (col >> sh)
    ps = [jnp.where(base, l, 0.0) for l in ls]
    ts = [jnp.where(row == col, 1.0, p) for p in ps]
    size = 2
    while size < INV_BASE:
        ps = [_mm(p, p, NN) for p in ps]
        ts = [_mm(t, eye + p, NN) for t, p in zip(ts, ps)]
        size *= 2
    size = INV_BASE
    hrow = lax.broadcasted_iota(jnp.int32, (rows // 2, rows), 0)
    hcol = lax.broadcasted_iota(jnp.int32, (rows // 2, rows), 1)
    zero = jnp.zeros((rows // 2, rows), F32)
    while size < limit:
        sh = size.bit_length() - 1
        partner = ((hcol >> (sh + 1)) == (hrow >> sh)) & (((hcol >> sh) & 1) == (1 if reverse else 0))
        act = 0 if reverse else 1
        l_act = [_halves(l, size)[act] for l in ls]
        t_halves = [_halves(t, size) for t in ts]
        xs = [_mm(jnp.where(partner, la, 0.0), t, NN) for la, t in zip(l_act, ts)]
        xs = [_interleave(x, zero, size) if reverse else _interleave(zero, x, size) for x in xs]
        upd = [th[act] + _mm(th[act], x, NN) for th, x in zip(t_halves, xs)]
        ts = [_interleave(u, th[1], size) if reverse else _interleave(th[0], u, size)
              for th, u in zip(t_halves, upd)]
        size *= 2
    return ts


def _scan_body(r_ref, v_ref, kk_ref, lw_ref, kd_ref, a_ref, y_ref, s_ref, *, n, reverse):
    c = pl.program_id(2)
    cs = r_ref.shape[1]

    @pl.when(c == 0)
    def _():
        s_ref[...] = jnp.zeros_like(s_ref)

    sgn = -1 if reverse else 1
    tiles = r_ref.shape[2] // LANES
    hp = LANES // n
    rows = hp * cs

    lw = lw_ref[0, 0]
    rc = lax.broadcasted_iota(jnp.int32, (cs, cs), 0)
    cc = lax.broadcasted_iota(jnp.int32, (cs, cs), 1)
    tri = ((rc - cc) * sgn >= 0).astype(BF16)
    l_hi = lw.astype(BF16)
    l_mid = (lw - l_hi.astype(F32)).astype(BF16)
    l_lo = (lw - l_hi.astype(F32) - l_mid.astype(F32)).astype(BF16)
    cum = _dot(tri, l_hi) + (_dot(tri, l_mid) + _dot(tri, l_lo))
    tot = cum[0:1, :] if reverse else cum[cs - 1:cs, :]
    e_tot = jnp.exp(tot)

    row = lax.broadcasted_iota(jnp.int32, (rows, rows), 0)
    col = lax.broadcasted_iota(jnp.int32, (rows, rows), 1)
    order = ((row & (cs - 1)) - (col & (cs - 1))) * sgn
    strict = order > 0
    incl = order >= 0
    eye_l = (lax.broadcasted_iota(jnp.int32, (LANES, LANES), 0)
             == lax.broadcasted_iota(jnp.int32, (LANES, LANES), 1))
    lane_head = lax.broadcasted_iota(jnp.int32, (cs, LANES), 1) // n
    head_masks = [lane_head == h for h in range(hp)]
    zero_t = jnp.zeros((rows, LANES), BF16)

    def pack(x):
        return jnp.concatenate([jnp.where(mk, x, jnp.zeros_like(x)) for mk in head_masks], axis=0)

    tl = range(tiles)
    cat = jnp.concatenate
    at, rt, bh, kh, vv, aa = [], [], [], [], [], []
    for p in tl:
        sl = slice(p * LANES, (p + 1) * LANES)
        cum_p, lw_p, tot_p = cum[:, sl], lw[:, sl], tot[:, sl]
        kk_p = kk_ref[0, :, sl].astype(F32)
        kd_p = kd_ref[0, 0, :, sl].astype(F32)
        b_p = kk_p * a_ref[0, 0, :, sl].astype(F32)
        e_neg = jnp.exp(-cum_p)
        e_hat = jnp.exp(tot_p - cum_p)
        at.append(pack(-kk_p * jnp.exp(cum_p - lw_p)))
        rt.append(pack(r_ref[0, :, sl].astype(F32) * jnp.exp(cum_p)))
        bh.append(pack((b_p * e_hat).astype(BF16)))
        kh.append(pack((kd_p * e_hat).astype(BF16)))
        vv.append(pack(v_ref[0, :, sl].astype(BF16)))
        bk = cat([pack((b_p * e_neg).astype(BF16)), pack((kd_p * e_neg).astype(BF16))], axis=0)
        aa.append(_mm(cat([at[p], rt[p]], axis=0), bk, NT).astype(BF16))
    aab = [jnp.where(strict, x[:rows, :rows], 0.0) for x in aa]
    aak = [jnp.where(strict, x[:rows, rows:], 0.0) for x in aa]
    aq = [cat([jnp.where(incl, x[rows:, :rows], 0.0), jnp.where(incl, x[rows:, rows:], 0.0)], axis=1)
          for x in aa]
    tinv = _tri_inverse(aab, cs, reverse)
    akv = [_mm(aak[p], vv[p], NN) for p in tl]
    wu = [_mm(tinv[p], cat([at[p], akv[p]], axis=1), NN) for p in tl]
    rhs = [cat([wu[p].astype(BF16), cat([zero_t, vv[p]], axis=1)], axis=0) for p in tl]
    qy = [_mm(aq[p], rhs[p], NN) for p in tl]
    mn = [_mm(cat([bh[p], kh[p]], axis=0), rhs[p], TN) for p in tl]
    lhs = []
    for p in tl:
        m_mat = jnp.where(eye_l, e_tot[:, p * LANES:(p + 1) * LANES], 0.0) + mn[p][:, :LANES]
        lhs.append(cat([m_mat, rt[p] + qy[p][:, :LANES]], axis=0))
    ms = [_mm(lhs[p], s_ref[p], NN) for p in tl]
    outs = []
    for p in tl:
        s_ref[p] = ms[p][:LANES] + mn[p][:, LANES:]
        ybd = ms[p][LANES:] + qy[p][:, LANES:]
        y = ybd[:cs]
        for h in range(1, hp):
            y = y + ybd[h * cs:(h + 1) * cs]
        outs.append(y)
    y_ref[0] = cat(outs, axis=1)


def _wkv_scan(r, v, kk, lw, kd, a, n, reverse):
    b, t, mix_b = r.shape
    heads = mix_b // n
    hg = SCAN_HEADS if heads % SCAN_HEADS == 0 else heads
    cs = SCAN_CHUNK
    nc = t // cs
    w = hg * n
    di = 1 if reverse else 0
    tchunk = (lambda ci: nc - 1 - ci) if reverse else (lambda ci: ci)
    shared = pl.BlockSpec((1, cs, w), lambda bi, hi, ci: (bi, tchunk(ci), hi))
    perdir = pl.BlockSpec((1, 1, cs, w), lambda bi, hi, ci: (di, bi, tchunk(ci), hi))
    return pl.pallas_call(
        functools.partial(_scan_body, n=n, reverse=reverse),
        out_shape=jax.ShapeDtypeStruct((b, t, mix_b), F32),
        grid=(b, heads // hg, nc),
        in_specs=[shared] * 3 + [perdir] * 3,
        out_specs=shared,
        scratch_shapes=[pltpu.VMEM((w // LANES, LANES, LANES), F32)],
        compiler_params=_params("parallel", "parallel", "arbitrary"),
        name="wkv_scan",
    )(r, v, kk, lw, kd, a)


def _post_body(yf_ref, yb_ref, bonus_ref, g_ref, gg_ref, gb_ref, j_ref, o_ref, *, n):
    y = yf_ref[...] + yb_ref[...]
    mean = _headsum(y, j_ref) * (1.0 / n)
    yc = y - mean
    var = _headsum(yc * yc, j_ref) * (1.0 / n)
    yn = yc * lax.rsqrt(var + GN_EPS) * gg_ref[...] + gb_ref[...]
    o_ref[...] = ((yn + bonus_ref[...]) * g_ref[...]).astype(o_ref.dtype)


def _rwkv_post(y_fwd, y_bwd, bonus, g, gn_g, gn_b, jmat, n):
    m, mix_b = y_fwd.shape
    tm = _pick(m, (512, 256, 128, 64, 32, 16, 8))
    row = pl.BlockSpec((tm, mix_b), lambda i: (i, 0))
    vec = pl.BlockSpec((1, mix_b), lambda i: (0, 0))
    return pl.pallas_call(
        functools.partial(_post_body, n=n),
        out_shape=jax.ShapeDtypeStruct((m, mix_b), BF16),
        grid=(m // tm,),
        in_specs=[row, row, row, row, vec, vec, pl.BlockSpec(jmat.shape, lambda i: (0, 0))],
        out_specs=row,
        compiler_params=_params("parallel"),
        name="rwkv_post",
    )(y_fwd, y_bwd, bonus, g, gn_g.reshape(1, mix_b), gn_b.reshape(1, mix_b), jmat)


def _attn_body(q_ref, kv_ref, o_ref, *, heads):
    d = q_ref.shape[-1]
    hd = d // heads
    scale = hd ** -0.5
    cols = [slice(h * hd, (h + 1) * hd) for h in range(heads)]
    ss = [lax.dot_general(q_ref[0, :, c], kv_ref[0, :, c], NT, preferred_element_type=F32) * scale for c in cols]
    es = [jnp.exp(s - jnp.max(s, axis=-1, keepdims=True)) for s in ss]
    ps = [(e / jnp.sum(e, axis=-1, keepdims=True)).astype(BF16) for e in es]
    for h, c in enumerate(cols):
        v = kv_ref[0, :, d + h * hd:d + (h + 1) * hd]
        o_ref[0, :, c] = _dot(ps[h], v).astype(o_ref.dtype)


def _attention(q, kv):
    b, t, d = q.shape
    n_mem = kv.shape[1]
    tq = _pick(t, (1024, 512, 256, 128, 64, 32, 16))
    return pl.pallas_call(
        functools.partial(_attn_body, heads=X_HEADS),
        out_shape=jax.ShapeDtypeStruct((b, t, d), BF16),
        grid=(b, t // tq),
        in_specs=[pl.BlockSpec((1, tq, d), lambda bi, i: (bi, i, 0)),
                  pl.BlockSpec((1, n_mem, 2 * d), lambda bi, i: (bi, 0, 0))],
        out_specs=pl.BlockSpec((1, tq, d), lambda bi, i: (bi, i, 0)),
        compiler_params=_params("parallel", "parallel"),
        name="cross_attention",
    )(q, kv)


def _rms(x, g):
    return x * lax.rsqrt(jnp.mean(x * x, axis=-1, keepdims=True) + RMS_EPS) * g


def _ffn_body(x_ref, xp_ref, xn_ref, gin_ref, gout_ref, wg_ref, wv_ref, cwg_ref, cwv_ref, cbg_ref,
              cbv_ref, wd_ref, o_ref, hx_ref):
    i = pl.program_id(1)
    f = pl.program_id(2)
    tm = x_ref.shape[1]
    halo = BF16_ROWS

    @pl.when(f == 0)
    def _():
        g = gin_ref[...]
        zero = jnp.zeros((halo, x_ref.shape[2]), BF16)
        hx_ref[0:halo, :] = jnp.where(i == 0, zero, _rms(xp_ref[0], g).astype(BF16))
        hx_ref[halo + tm:, :] = jnp.where(i == pl.num_programs(1) - 1, zero, _rms(xn_ref[0], g).astype(BF16))

        def norm_rows(c, carry):
            r0 = pl.multiple_of(c * NORM_ROWS, NORM_ROWS)
            hx_ref[pl.ds(halo + r0, NORM_ROWS), :] = _rms(x_ref[0, pl.ds(r0, NORM_ROWS), :], g).astype(BF16)
            o_ref[0, pl.ds(r0, NORM_ROWS), :] = jnp.zeros((NORM_ROWS, x_ref.shape[2]), F32)
            return carry

        lax.fori_loop(0, tm // NORM_ROWS, norm_rows, 0)

    hx = hx_ref[...]
    rows = tm + 2 * halo

    def conv(w_ref, cw_ref, cb_ref):
        z = _dot(hx, w_ref[...])
        prev = pltpu.roll(z, 1, 0)[halo:halo + tm]
        nxt = pltpu.roll(z, rows - 1, 0)[halo:halo + tm]
        cw = cw_ref[...]
        return cw[0:1] * prev + cw[1:2] * z[halo:halo + tm] + cw[2:3] * nxt + cb_ref[...]

    gate = conv(wg_ref, cwg_ref, cbg_ref)
    val = conv(wv_ref, cwv_ref, cbv_ref)
    act = (jax.nn.silu(gate) * val).astype(BF16)
    o_ref[0] += _dot(act, wd_ref[...])

    @pl.when(f == pl.num_programs(2) - 1)
    def _():
        g = gout_ref[...]

        def out_rows(c, carry):
            rs = pl.ds(pl.multiple_of(c * NORM_ROWS, NORM_ROWS), NORM_ROWS)
            o_ref[0, rs, :] = _rms(x_ref[0, rs, :] + o_ref[0, rs, :], g)
            return carry

        lax.fori_loop(0, tm // NORM_ROWS, out_rows, 0)


def _conv_ffn(x, g_in, g_out, w_up, conv_w, conv_b, w_down):
    b, t, d = x.shape
    d_ff = w_down.shape[0]
    tm = _pick(t, (1024, 512, 256, 128))
    tf = _pick(d_ff, (256, 128))
    nf = d_ff // tf
    hb = tm // BF16_ROWS
    nblk = t // BF16_ROWS
    cb = conv_b.reshape(1, 2 * d_ff)
    once = pl.Buffered(1)
    return pl.pallas_call(
        _ffn_body,
        out_shape=jax.ShapeDtypeStruct((b, t, d), F32),
        grid=(b, t // tm, nf),
        in_specs=[
            pl.BlockSpec((1, tm, d), lambda bi, i, f: (bi, i, 0), pipeline_mode=once),
            pl.BlockSpec((1, BF16_ROWS, d), lambda bi, i, f: (bi, jnp.maximum(i * hb - 1, 0), 0)),
            pl.BlockSpec((1, BF16_ROWS, d), lambda bi, i, f: (bi, jnp.minimum((i + 1) * hb, nblk - 1), 0)),
            pl.BlockSpec((1, d), lambda bi, i, f: (0, 0)),
            pl.BlockSpec((1, d), lambda bi, i, f: (0, 0)),
            pl.BlockSpec((d, tf), lambda bi, i, f: (0, f)),
            pl.BlockSpec((d, tf), lambda bi, i, f: (0, nf + f)),
            pl.BlockSpec((3, tf), lambda bi, i, f: (0, f)),
            pl.BlockSpec((3, tf), lambda bi, i, f: (0, nf + f)),
            pl.BlockSpec((1, tf), lambda bi, i, f: (0, f)),
            pl.BlockSpec((1, tf), lambda bi, i, f: (0, nf + f)),
            pl.BlockSpec((tf, d), lambda bi, i, f: (f, 0)),
        ],
        out_specs=pl.BlockSpec((1, tm, d), lambda bi, i, f: (bi, i, 0), pipeline_mode=once),
        scratch_shapes=[pltpu.VMEM((tm + 2 * BF16_ROWS, d), BF16)],
        compiler_params=_params("parallel", "parallel", "arbitrary"),
        name="conv_ffn",
    )(x, x, x, g_in.reshape(1, d), g_out.reshape(1, d), w_up, w_up, conv_w, conv_w, cb, cb, w_down)


def _trunk(x, mem, p):
    b, t, d = x.shape
    m = b * t
    n_mem = mem.shape[1]
    mix_a = p["ln_v_g"].shape[-1]
    mix_b = p["k_k"].shape[-1]
    n = p["r_k"].shape[-1]
    x2d = x.reshape(m, d)

    h1 = _rmsnorm([x2d], p["norm_mix"], BF16)
    za = _matmul([h1], p["w_in"], F32, col0=0, n=2 * mix_a)
    zrkv = _matmul([h1], p["w_in"], F32, col0=2 * mix_a, n=3 * mix_b)
    zl = _matmul([h1], p["w_in"], F32, col0=2 * mix_a + 3 * mix_b, n=p["w_in"].shape[1] - 2 * mix_a - 3 * mix_b)

    ya = _gmlp(za, p["ln_v_g"], p["ln_v_b"], p["w_s"], p["b_s"])

    r, v, kk, g, bonus, lw, kd, a = _rwkv_prep(
        zrkv.reshape(b, t, 3 * mix_b), zl.reshape(b, t, -1), p["mu_shift"], p["w0"], p["w_up_decay"],
        p["a0"], p["w_up_iclr"], p["w_up_gate"], p["k_k"], p["k_a"], p["r_k"], p["jmat"])
    y_fwd = _wkv_scan(r, v, kk, lw, kd, a, n, reverse=False)
    y_bwd = _wkv_scan(r, v, kk, lw, kd, a, n, reverse=True)
    yb = _rwkv_post(y_fwd.reshape(m, mix_b), y_bwd.reshape(m, mix_b), bonus.reshape(m, mix_b),
                    g.reshape(m, mix_b), p["gn_g"], p["gn_b"], p["jmat"], n)

    x1 = _matmul([ya, yb], p["w_out"], F32, residual=x2d)

    hq = _rmsnorm([x1], p["norm_x"], BF16)
    q = _matmul([hq], p["w_q"], BF16)
    memn = _rmsnorm([mem.reshape(b * n_mem, d)], p["norm_mem"], BF16)
    kv = _matmul([memn], p["w_kv"], BF16)
    o = _attention(q.reshape(b, t, d), kv.reshape(b, n_mem, 2 * d))
    x2 = _matmul([o.reshape(m, d)], p["w_o"], F32, residual=x1)

    return _conv_ffn(x2.reshape(b, t, d), p["norm_ffn"], p["norm_out"], p["w_ffn_up"], p["conv_w"],
                     p["conv_b"], p["w_ffn_down"])


def kernel(x_prompt, x_sample, mem_prompt, mem_sample, norm_mix, w_in, mu_shift, ln_v_g, ln_v_b, w_s, b_s, w0, w_up_decay, a0, w_up_iclr, w_up_gate, k_k, k_a, r_k, gn_g, gn_b, w_out, norm_x, norm_mem, w_q, w_kv, w_o, norm_ffn, w_ffn_up, conv_w, conv_b, w_ffn_down, norm_out):
    depth = w_in.shape[0]
    n = r_k.shape[-1]
    head_id = jnp.arange(LANES) // n
    jmat = (head_id[:, None] == head_id[None, :]).astype(BF16)

    layers = []
    for l in range(depth):
        layers.append(dict(
            norm_mix=norm_mix[l], w_in=w_in[l].astype(BF16), mu_shift=mu_shift[l],
            ln_v_g=ln_v_g[l], ln_v_b=ln_v_b[l], w_s=w_s[l], b_s=b_s[l],
            w0=w0[l], w_up_decay=w_up_decay[l], a0=a0[l], w_up_iclr=w_up_iclr[l], w_up_gate=w_up_gate[l],
            k_k=k_k[l], k_a=k_a[l], r_k=r_k[l], gn_g=gn_g[l], gn_b=gn_b[l],
            w_out=w_out[l].astype(BF16), norm_x=norm_x[l], norm_mem=norm_mem[l],
            w_q=w_q[l].astype(BF16), w_kv=w_kv[l].astype(BF16), w_o=w_o[l].astype(BF16),
            norm_ffn=norm_ffn[l], w_ffn_up=w_ffn_up[l].astype(BF16), conv_w=conv_w[l], conv_b=conv_b[l],
            w_ffn_down=w_ffn_down[l].astype(BF16), jmat=jmat))

    (p,) = layers
    p = dict(p, norm_out=norm_out)
    return _trunk(x_prompt, mem_prompt, p), _trunk(x_sample, mem_sample, p)
```

```python
import functools

import jax
import jax.numpy as jnp
from jax import lax
from jax.experimental import pallas as pl
from jax.experimental.pallas import tpu as pltpu

RMS_EPS = 1e-6
LN_EPS = 1e-5
GN_EPS = 64e-5
L2_EPS = 1e-12
LOGW_SCALE = 0.6065306597126334
X_HEADS = 4

LANES = 128
BF16_ROWS = 16
SCAN_CHUNK = 64
SCAN_HEADS = 32
NORM_ROWS = 128
FFN_CHUNKS = (3, 2, 1)
INV_BASE = 8
VMEM_LIMIT = 60 * 1024 * 1024

F32 = jnp.float32
BF16 = jnp.bfloat16


def _params(*sem, flags=None):
    return pltpu.CompilerParams(dimension_semantics=sem, vmem_limit_bytes=VMEM_LIMIT, flags=flags)


def _pick(n, candidates):
    for c in candidates:
        if n % c == 0:
            return c
    raise ValueError(f"no tile for {n} in {candidates}")


def _dot(a, b):
    return jnp.dot(a, b, preferred_element_type=F32)


NN = (((1,), (0,)), ((), ()))
NT = (((1,), (1,)), ((), ()))
TN = (((0,), (0,)), ((), ()))


def _rmsnorm_body(*refs, n_in):
    g_ref, o_ref = refs[n_in], refs[n_in + 1]
    x = refs[0][...]
    for r in refs[1:n_in]:
        x = x + r[...]
    ms = jnp.mean(x * x, axis=-1, keepdims=True)
    o_ref[...] = (x * lax.rsqrt(ms + RMS_EPS) * g_ref[...]).astype(o_ref.dtype)


def _rmsnorm(xs, g, out_dtype):
    m, d = xs[0].shape
    tm = _pick(m, (512, 256, 128, 64, 32, 16, 8))
    row = pl.BlockSpec((tm, d), lambda i: (i, 0))
    return pl.pallas_call(
        functools.partial(_rmsnorm_body, n_in=len(xs)),
        out_shape=jax.ShapeDtypeStruct((m, d), out_dtype),
        grid=(m // tm,),
        in_specs=[row] * len(xs) + [pl.BlockSpec((1, d), lambda i: (0, 0))],
        out_specs=row,
        compiler_params=_params("parallel"),
        name="rmsnorm",
    )(*xs, g.reshape(1, d))


def _mm_body(*refs, n_lhs, has_res):
    a_refs, w_refs = refs[:n_lhs], refs[n_lhs:2 * n_lhs]
    o_ref = refs[-1]
    acc = _dot(a_refs[0][...], w_refs[0][...])
    for a, w in zip(a_refs[1:], w_refs[1:]):
        acc = acc + _dot(a[...], w[...])
    if has_res:
        acc = acc + refs[2 * n_lhs][...]
    o_ref[...] = acc.astype(o_ref.dtype)


def _matmul(lhs, w, out_dtype, residual=None, col0=0, n=None):
    m = lhs[0].shape[0]
    n = w.shape[1] if n is None else n
    tm = _pick(m, (1024, 512, 256, 128, 64, 32, 16, 8))
    tn = _pick(n, (1024, 768, 512, 256, 128))
    if col0 % tn:
        w, col0 = w[:, col0:col0 + n], 0
    cb = col0 // tn
    in_specs = [pl.BlockSpec((tm, a.shape[1]), lambda i, j: (i, 0)) for a in lhs]
    in_specs += [pl.BlockSpec((a.shape[1], tn), lambda i, j, rb=rb: (rb, cb + j)) for rb, a in enumerate(lhs)]
    assert all(a.shape[1] == lhs[0].shape[1] for a in lhs) and len(lhs) * lhs[0].shape[1] == w.shape[0]
    args = list(lhs) + [w] * len(lhs)
    if residual is not None:
        in_specs.append(pl.BlockSpec((tm, tn), lambda i, j: (i, j)))
        args.append(residual)
    return pl.pallas_call(
        functools.partial(_mm_body, n_lhs=len(lhs), has_res=residual is not None),
        out_shape=jax.ShapeDtypeStruct((m, n), out_dtype),
        grid=(m // tm, n // tn),
        in_specs=in_specs,
        out_specs=pl.BlockSpec((tm, tn), lambda i, j: (i, j)),
        compiler_params=_params("parallel", "parallel"),
        name="matmul",
    )(*args)


def _gmlp_body(zu_ref, zv_ref, g_ref, b_ref, ws_ref, bs_ref, o_ref, *, chunk):
    u = jax.nn.gelu(zu_ref[...])
    v = jax.nn.gelu(zv_ref[...])
    mu = jnp.mean(v, axis=-1, keepdims=True)
    var = jnp.mean(jnp.square(v - mu), axis=-1, keepdims=True)
    vn = ((v - mu) * lax.rsqrt(var + LN_EPS) * g_ref[...] + b_ref[...]).astype(BF16)
    rows, width = u.shape
    heads = ws_ref.shape[0]
    hd = width // heads
    bs = bs_ref[...]
    for c in range(rows // chunk):
        rs = slice(c * chunk, (c + 1) * chunk)
        mixed = jnp.concatenate(
            [_dot(ws_ref[h], vn[rs, h * hd:(h + 1) * hd]) for h in range(heads)], axis=1)
        o_ref[rs, :] = (u[rs, :] * (mixed + bs)).astype(o_ref.dtype)


def _gmlp(za, ln_g, ln_b, w_s, b_s):
    m, two_a = za.shape
    mix_a = two_a // 2
    heads, chunk, _ = w_s.shape
    rows = _pick(m, (4 * chunk, 2 * chunk, chunk))
    bs_full = jnp.repeat(b_s.T, mix_a // heads, axis=1)
    vec = pl.BlockSpec((1, mix_a), lambda i: (0, 0))
    return pl.pallas_call(
        functools.partial(_gmlp_body, chunk=chunk),
        out_shape=jax.ShapeDtypeStruct((m, mix_a), BF16),
        grid=(m // rows,),
        in_specs=[pl.BlockSpec((rows, mix_a), lambda i: (i, 0)),
                  pl.BlockSpec((rows, mix_a), lambda i: (i, 1)),
                  vec, vec,
                  pl.BlockSpec((heads, chunk, chunk), lambda i: (0, 0, 0)),
                  pl.BlockSpec((chunk, mix_a), lambda i: (0, 0))],
        out_specs=pl.BlockSpec((rows, mix_a), lambda i: (i, 0)),
        compiler_params=_params("parallel"),
        name="gmlp",
    )(za, za, ln_g.reshape(1, mix_a), ln_b.reshape(1, mix_a), w_s.astype(BF16), bs_full)


def _headsum(x, j_ref):
    tm, w = x.shape
    nb = w // LANES
    xs = jnp.concatenate([x[:, i * LANES:(i + 1) * LANES] for i in range(nb)], axis=0)
    s = _dot(xs.astype(BF16), j_ref[...])
    return jnp.concatenate([s[i * tm:(i + 1) * tm] for i in range(nb)], axis=1)


def _tshift(z, zp, zn, mu, first, last):
    tm = z.shape[0]
    rows = lax.broadcasted_iota(jnp.int32, (8, 1), 0)
    prev_row = jnp.where(first, 0.0, zp[7:8, :])
    next_row = jnp.where(last, 0.0, zn[0:1, :])
    prev = pltpu.roll(z, 1, 0)
    nxt = pltpu.roll(z, tm - 1, 0)
    prev = jnp.concatenate([jnp.where(rows == 0, prev_row, prev[:8]), prev[8:]], axis=0)
    nxt = jnp.concatenate([nxt[:tm - 8], jnp.where(rows == 7, next_row, nxt[tm - 8:])], axis=0)
    return z * (1.0 - mu) + (0.5 * mu) * (prev + nxt)


def _prep_body(zr, zrp, zrn, zk, zkp, zkn, zv, zvp, zvn, zl, zlp, zln,
               mur, muk, muv, mul, w0_ref, wd_ref, a0_ref, wi_ref, wg_ref,
               kk_ref, ka_ref, rk_ref, j_ref,
               r_out, v_out, kk_out, g_out, bonus_out, lw_out, kd_out, a_out,
               *, lora_w, lora_a):
    i = pl.program_id(1)
    first = i == 0
    last = i == pl.num_programs(1) - 1
    r = _tshift(zr[0], zrp[0], zrn[0], mur[...], first, last)
    k = _tshift(zk[0], zkp[0], zkn[0], muk[...], first, last)
    v = _tshift(zv[0], zvp[0], zvn[0], muv[...], first, last)
    lo = _tshift(zl[0], zlp[0], zln[0], mul[...], first, last)
    xw = lo[:, :lora_w]
    xa = lo[:, lora_w:lora_w + lora_a]
    xg = lo[:, lora_w + lora_a:]

    g_out[0] = _dot(jax.nn.sigmoid(xg).astype(BF16), wg_ref[...]).astype(g_out.dtype)
    kkr = k * kk_ref[...]
    ss = _headsum(kkr * kkr, j_ref)
    kk = kkr * lax.rsqrt(jnp.maximum(ss, L2_EPS))
    tw = jnp.tanh(xw).astype(BF16)
    xab = xa.astype(BF16)
    rk = None
    for d in range(2):
        lw_out[d, 0] = -LOGW_SCALE * jax.nn.sigmoid(w0_ref[d:d + 1, :] + _dot(tw, wd_ref[d]))
        a = jax.nn.sigmoid(a0_ref[d:d + 1, :] + _dot(xab, wi_ref[d]))
        kd = k * (1.0 + (a - 1.0) * ka_ref[...])
        a_out[d, 0] = a.astype(a_out.dtype)
        kd_out[d, 0] = kd.astype(kd_out.dtype)
        s = _headsum(r * kd * rk_ref[...], j_ref)
        rk = s if rk is None else rk + s
    r_out[0] = r.astype(r_out.dtype)
    v_out[0] = v.astype(v_out.dtype)
    kk_out[0] = kk.astype(kk_out.dtype)
    bonus_out[0] = (rk * v).astype(bonus_out.dtype)


def _rwkv_prep(zrkv, zl, mu, w0, w_up_decay, a0, w_up_iclr, w_up_gate, k_k, k_a, r_k, jmat):
    b, t, three_b = zrkv.shape
    mix_b = three_b // 3
    nl = zl.shape[-1]
    lora_w, lora_a = w_up_decay.shape[1], w_up_iclr.shape[1]
    tm = _pick(t, (128, 64, 32, 16, 8))
    hb = tm // 8
    nblk8 = t // 8

    def main(w, col):
        return pl.BlockSpec((1, tm, w), lambda bi, i: (bi, i, col))

    def prev(w, col):
        return pl.BlockSpec((1, 8, w), lambda bi, i: (bi, jnp.maximum(i * hb - 1, 0), col))

    def nxt(w, col):
        return pl.BlockSpec((1, 8, w), lambda bi, i: (bi, jnp.minimum((i + 1) * hb, nblk8 - 1), col))

    def const(shape):
        return pl.BlockSpec(shape, lambda bi, i: (0,) * len(shape))

    in_specs, args = [], []
    for col in range(3):
        in_specs += [main(mix_b, col), prev(mix_b, col), nxt(mix_b, col)]
        args += [zrkv] * 3
    in_specs += [main(nl, 0), prev(nl, 0), nxt(nl, 0)]
    args += [zl] * 3
    mu2 = mu.reshape(1, -1)
    in_specs += [pl.BlockSpec((1, mix_b), lambda bi, i, c=c: (0, c)) for c in range(3)]
    args += [mu2[:, :three_b]] * 3
    in_specs.append(const((1, nl)))
    args.append(mu2[:, three_b:])
    consts = [w0, w_up_decay.astype(BF16), a0, w_up_iclr.astype(BF16), w_up_gate.astype(BF16),
              k_k.reshape(1, mix_b), k_a.reshape(1, mix_b), r_k.reshape(1, mix_b), jmat]
    in_specs += [const(c.shape) for c in consts]
    args += consts

    shared = jax.ShapeDtypeStruct((b, t, mix_b), BF16)
    perdir = lambda dt: jax.ShapeDtypeStruct((2, b, t, mix_b), dt)
    o_shared = pl.BlockSpec((1, tm, mix_b), lambda bi, i: (bi, i, 0))
    o_perdir = pl.BlockSpec((2, 1, tm, mix_b), lambda bi, i: (0, bi, i, 0))
    return pl.pallas_call(
        functools.partial(_prep_body, lora_w=lora_w, lora_a=lora_a),
        out_shape=[shared] * 5 + [perdir(F32), perdir(BF16), perdir(BF16)],
        grid=(b, t // tm),
        in_specs=in_specs,
        out_specs=[o_shared] * 5 + [o_perdir] * 3,
        compiler_params=_params("parallel", "parallel"),
        name="rwkv_prep",
    )(*args)


def _mm(a, b, dims):
    return lax.dot_general(a.astype(BF16), b.astype(BF16), dims, preferred_element_type=F32)


def _halves(x, size):
    blocks = x.shape[0] // (2 * size)
    lo = jnp.concatenate([x[2 * j * size:(2 * j + 1) * size] for j in range(blocks)], axis=0)
    hi = jnp.concatenate([x[(2 * j + 1) * size:(2 * j + 2) * size] for j in range(blocks)], axis=0)
    return lo, hi


def _interleave(lo, hi, size):
    pieces = []
    for j in range(lo.shape[0] // size):
        pieces += [lo[j * size:(j + 1) * size], hi[j * size:(j + 1) * size]]
    return jnp.concatenate(pieces, axis=0)


def _tri_inverse(ls, limit, reverse):
    rows = ls[0].shape[0]
    row = lax.broadcasted_iota(jnp.int32, (rows, rows), 0)
    col = lax.broadcasted_iota(jnp.int32, (rows, rows), 1)
    eye = (row == col).astype(F32)
    sh = INV_BASE.bit_length() - 1
    base = (row >> sh) == (col >> sh)
    ps = [jnp.where(base, l, 0.0) for l in ls]
    ts = [jnp.where(row == col, 1.0, p) for p in ps]
    size = 2
    while size < INV_BASE:
        ps = [_mm(p, p, NN) for p in ps]
        ts = [_mm(t, eye + p, NN) for t, p in zip(ts, ps)]
        size *= 2
    size = INV_BASE
    hrow = lax.broadcasted_iota(jnp.int32, (rows // 2, rows), 0)
    hcol = lax.broadcasted_iota(jnp.int32, (rows // 2, rows), 1)
    zero = jnp.zeros((rows // 2, rows), F32)
    while size < limit:
        sh = size.bit_length() - 1
        partner = ((hcol >> (sh + 1)) == (hrow >> sh)) & (((hcol >> sh) & 1) == (1 if reverse else 0))
        act = 0 if reverse else 1
        l_act = [_halves(l, size)[act] for l in ls]
        t_halves = [_halves(t, size) for t in ts]
        xs = [_mm(jnp.where(partner, la, 0.0), t, NN) for la, t in zip(l_act, ts)]
        xs = [_interleave(x, zero, size) if reverse else _interleave(zero, x, size) for x in xs]
        upd = [th[act] + _mm(th[act], x, NN) for th, x in zip(t_halves, xs)]
        ts = [_interleave(u, th[1], size) if reverse else _interleave(th[0], u, size)
              for th, u in zip(t_halves, upd)]
        size *= 2
    return ts


def _scan_body(r_ref, v_ref, kk_ref, lw_ref, kd_ref, a_ref, y_ref, s_ref, *, n, reverse):
    c = pl.program_id(2)
    cs = r_ref.shape[1]

    @pl.when(c == 0)
    def _():
        s_ref[...] = jnp.zeros_like(s_ref)

    sgn = -1 if reverse else 1
    tiles = r_ref.shape[2] // LANES
    hp = LANES // n
    rows = hp * cs

    lw = lw_ref[0, 0]
    rc = lax.broadcasted_iota(jnp.int32, (cs, cs), 0)
    cc = lax.broadcasted_iota(jnp.int32, (cs, cs), 1)
    tri = ((rc - cc) * sgn >= 0).astype(BF16)
    l_hi = lw.astype(BF16)
    l_mid = (lw - l_hi.astype(F32)).astype(BF16)
    l_lo = (lw - l_hi.astype(F32) - l_mid.astype(F32)).astype(BF16)
    cum = _dot(tri, l_hi) + (_dot(tri, l_mid) + _dot(tri, l_lo))
    tot = cum[0:1, :] if reverse else cum[cs - 1:cs, :]
    e_tot = jnp.exp(tot)

    row = lax.broadcasted_iota(jnp.int32, (rows, rows), 0)
    col = lax.broadcasted_iota(jnp.int32, (rows, rows), 1)
    order = ((row & (cs - 1)) - (col & (cs - 1))) * sgn
    strict = order > 0
    incl = order >= 0
    eye_l = (lax.broadcasted_iota(jnp.int32, (LANES, LANES), 0)
             == lax.broadcasted_iota(jnp.int32, (LANES, LANES), 1))
    lane_head = lax.broadcasted_iota(jnp.int32, (cs, LANES), 1) // n
    head_masks = [lane_head == h for h in range(hp)]
    zero_t = jnp.zeros((rows, LANES), BF16)

    def pack(x):
        return jnp.concatenate([jnp.where(mk, x, jnp.zeros_like(x)) for mk in head_masks], axis=0)

    tl = range(tiles)
    cat = jnp.concatenate
    at, rt, bh, kh, vv, aa = [], [], [], [], [], []
    for p in tl:
        sl = slice(p * LANES, (p + 1) * LANES)
        cum_p, lw_p, tot_p = cum[:, sl], lw[:, sl], tot[:, sl]
        kk_p = kk_ref[0, :, sl].astype(F32)
        kd_p = kd_ref[0, 0, :, sl].astype(F32)
        b_p = kk_p * a_ref[0, 0, :, sl].astype(F32)
        e_neg = jnp.exp(-cum_p)
        e_hat = jnp.exp(tot_p - cum_p)
        at.append(pack(-kk_p * jnp.exp(cum_p - lw_p)))
        rt.append(pack(r_ref[0, :, sl].astype(F32) * jnp.exp(cum_p)))
        bh.append(pack((b_p * e_hat).astype(BF16)))
        kh.append(pack((kd_p * e_hat).astype(BF16)))
        vv.append(pack(v_ref[0, :, sl].astype(BF16)))
        bk = cat([pack((b_p * e_neg).astype(BF16)), pack((kd_p * e_neg).astype(BF16))], axis=0)
        aa.append(_mm(cat([at[p], rt[p]], axis=0), bk, NT).astype(BF16))
    aab = [jnp.where(strict, x[:rows, :rows], 0.0) for x in aa]
    aak = [jnp.where(strict, x[:rows, rows:], 0.0) for x in aa]
    aq = [cat([jnp.where(incl, x[rows:, :rows], 0.0), jnp.where(incl, x[rows:, rows:], 0.0)], axis=1)
          for x in aa]
    tinv = _tri_inverse(aab, cs, reverse)
    akv = [_mm(aak[p], vv[p], NN) for p in tl]
    wu = [_mm(tinv[p], cat([at[p], akv[p]], axis=1), NN) for p in tl]
    rhs = [cat([wu[p].astype(BF16), cat([zero_t, vv[p]], axis=1)], axis=0) for p in tl]
    qy = [_mm(aq[p], rhs[p], NN) for p in tl]
    mn = [_mm(cat([bh[p], kh[p]], axis=0), rhs[p], TN) for p in tl]
    lhs = []
    for p in tl:
        m_mat = jnp.where(eye_l, e_tot[:, p * LANES:(p + 1) * LANES], 0.0) + mn[p][:, :LANES]
        lhs.append(cat([m_mat, rt[p] + qy[p][:, :LANES]], axis=0))
    ms = [_mm(lhs[p], s_ref[p], NN) for p in tl]
    outs = []
    for p in tl:
        s_ref[p] = ms[p][:LANES] + mn[p][:, LANES:]
        ybd = ms[p][LANES:] + qy[p][:, LANES:]
        y = ybd[:cs]
        for h in range(1, hp):
            y = y + ybd[h * cs:(h + 1) * cs]
        outs.append(y)
    y_ref[0] = cat(outs, axis=1)


def _wkv_scan(r, v, kk, lw, kd, a, n, reverse):
    b, t, mix_b = r.shape
    heads = mix_b // n
    hg = SCAN_HEADS if heads % SCAN_HEADS == 0 else heads
    cs = SCAN_CHUNK
    nc = t // cs
    w = hg * n
    di = 1 if reverse else 0
    tchunk = (lambda ci: nc - 1 - ci) if reverse else (lambda ci: ci)
    shared = pl.BlockSpec((1, cs, w), lambda bi, hi, ci: (bi, tchunk(ci), hi))
    perdir = pl.BlockSpec((1, 1, cs, w), lambda bi, hi, ci: (di, bi, tchunk(ci), hi))
    return pl.pallas_call(
        functools.partial(_scan_body, n=n, reverse=reverse),
        out_shape=jax.ShapeDtypeStruct((b, t, mix_b), F32),
        grid=(b, heads // hg, nc),
        in_specs=[shared] * 3 + [perdir] * 3,
        out_specs=shared,
        scratch_shapes=[pltpu.VMEM((w // LANES, LANES, LANES), F32)],
        compiler_params=_params("parallel", "parallel", "arbitrary"),
        name="wkv_scan",
    )(r, v, kk, lw, kd, a)


def _post_body(yf_ref, yb_ref, bonus_ref, g_ref, gg_ref, gb_ref, j_ref, o_ref, *, n):
    y = yf_ref[...] + yb_ref[...]
    mean = _headsum(y, j_ref) * (1.0 / n)
    yc = y - mean
    var = _headsum(yc * yc, j_ref) * (1.0 / n)
    yn = yc * lax.rsqrt(var + GN_EPS) * gg_ref[...] + gb_ref[...]
    o_ref[...] = ((yn + bonus_ref[...]) * g_ref[...]).astype(o_ref.dtype)


def _rwkv_post(y_fwd, y_bwd, bonus, g, gn_g, gn_b, jmat, n):
    m, mix_b = y_fwd.shape
    tm = _pick(m, (512, 256, 128, 64, 32, 16, 8))
    row = pl.BlockSpec((tm, mix_b), lambda i: (i, 0))
    vec = pl.BlockSpec((1, mix_b), lambda i: (0, 0))
    return pl.pallas_call(
        functools.partial(_post_body, n=n),
        out_shape=jax.ShapeDtypeStruct((m, mix_b), BF16),
        grid=(m // tm,),
        in_specs=[row, row, row, row, vec, vec, pl.BlockSpec(jmat.shape, lambda i: (0, 0))],
        out_specs=row,
        compiler_params=_params("parallel"),
        name="rwkv_post",
    )(y_fwd, y_bwd, bonus, g, gn_g.reshape(1, mix_b), gn_b.reshape(1, mix_b), jmat)


def _attn_body(q_ref, kv_ref, o_ref, *, heads):
    d = q_ref.shape[-1]
    hd = d // heads
    scale = hd ** -0.5
    cols = [slice(h * hd, (h + 1) * hd) for h in range(heads)]
    ss = [lax.dot_general(q_ref[0, :, c], kv_ref[0, :, c], NT, preferred_element_type=F32) * scale for c in cols]
    es = [jnp.exp(s - jnp.max(s, axis=-1, keepdims=True)) for s in ss]
    ps = [(e / jnp.sum(e, axis=-1, keepdims=True)).astype(BF16) for e in es]
    for h, c in enumerate(cols):
        v = kv_ref[0, :, d + h * hd:d + (h + 1) * hd]
        o_ref[0, :, c] = _dot(ps[h], v).astype(o_ref.dtype)


def _attention(q, kv):
    b, t, d = q.shape
    n_mem = kv.shape[1]
    tq = _pick(t, (1024, 512, 256, 128, 64, 32, 16))
    return pl.pallas_call(
        functools.partial(_attn_body, heads=X_HEADS),
        out_shape=jax.ShapeDtypeStruct((b, t, d), BF16),
        grid=(b, t // tq),
        in_specs=[pl.BlockSpec((1, tq, d), lambda bi, i: (bi, i, 0)),
                  pl.BlockSpec((1, n_mem, 2 * d), lambda bi, i: (bi, 0, 0))],
        out_specs=pl.BlockSpec((1, tq, d), lambda bi, i: (bi, i, 0)),
        compiler_params=_params("parallel", "parallel"),
        name="cross_attention",
    )(q, kv)


def _rms(x, g):
    return x * lax.rsqrt(jnp.mean(x * x, axis=-1, keepdims=True) + RMS_EPS) * g


def _ffn_body(x_ref, xp_ref, xn_ref, gin_ref, gout_ref, wg_ref, wv_ref, cwg_ref, cwv_ref, cbg_ref,
              cbv_ref, wd_ref, o_ref, hx_ref):
    i = pl.program_id(1)
    f = pl.program_id(2)
    tm = x_ref.shape[1]
    halo = BF16_ROWS

    @pl.when(f == 0)
    def _():
        g = gin_ref[...]
        zero = jnp.zeros((halo, x_ref.shape[2]), BF16)
        hx_ref[0:halo, :] = jnp.where(i == 0, zero, _rms(xp_ref[0], g).astype(BF16))
        hx_ref[halo + tm:, :] = jnp.where(i == pl.num_programs(1) - 1, zero, _rms(xn_ref[0], g).astype(BF16))

        def norm_rows(c, carry):
            r0 = pl.multiple_of(c * NORM_ROWS, NORM_ROWS)
            hx_ref[pl.ds(halo + r0, NORM_ROWS), :] = _rms(x_ref[0, pl.ds(r0, NORM_ROWS), :], g).astype(BF16)
            o_ref[0, pl.ds(r0, NORM_ROWS), :] = jnp.zeros((NORM_ROWS, x_ref.shape[2]), F32)
            return carry

        lax.fori_loop(0, tm // NORM_ROWS, norm_rows, 0)

    chunks = _pick((tm + 2 * halo) // BF16_ROWS, FFN_CHUNKS)
    ch = (tm + 2 * halo) // chunks
    keep = 2 * halo

    def up(c):
        lhs = hx_ref[c * ch:(c + 1) * ch, :]
        return _dot(lhs, wg_ref[...]), _dot(lhs, wv_ref[...])

    def conv(z, cw_ref, cb_ref, n_out):
        prev = pltpu.roll(z, 1, 0)[halo:halo + n_out]
        nxt = pltpu.roll(z, z.shape[0] - 1, 0)[halo:halo + n_out]
        cw = cw_ref[...]
        return cw[0:1] * prev + cw[1:2] * z[halo:halo + n_out] + cw[2:3] * nxt + cb_ref[...]

    def finish(c, zs, prev_zs):
        if c == 0:
            win, r0, n_out = zs, 0, ch - keep
        else:
            win = [jnp.concatenate([p[ch - keep:], z], axis=0) for p, z in zip(prev_zs, zs)]
            r0, n_out = c * ch - keep, ch
        act = (jax.nn.silu(conv(win[0], cwg_ref, cbg_ref, n_out))
               * conv(win[1], cwv_ref, cbv_ref, n_out)).astype(BF16)
        o_ref[0, r0:r0 + n_out, :] += _dot(act, wd_ref[...])

    zs = [up(0)]
    for c in range(1, chunks):
        zs.append(up(c))
        finish(c - 1, zs[c - 1], zs[c - 2] if c >= 2 else None)
    finish(chunks - 1, zs[-1], zs[-2] if chunks >= 2 else None)

    @pl.when(f == pl.num_programs(2) - 1)
    def _():
        g = gout_ref[...]

        def out_rows(c, carry):
            rs = pl.ds(pl.multiple_of(c * NORM_ROWS, NORM_ROWS), NORM_ROWS)
            o_ref[0, rs, :] = _rms(x_ref[0, rs, :] + o_ref[0, rs, :], g)
            return carry

        lax.fori_loop(0, tm // NORM_ROWS, out_rows, 0)


def _conv_ffn(x, g_in, g_out, w_up, conv_w, conv_b, w_down):
    b, t, d = x.shape
    d_ff = w_down.shape[0]
    tm = _pick(t, (1024, 512, 256, 128))
    tf = _pick(d_ff, (256, 128))
    nf = d_ff // tf
    hb = tm // BF16_ROWS
    nblk = t // BF16_ROWS
    cb = conv_b.reshape(1, 2 * d_ff)
    once = pl.Buffered(1)
    return pl.pallas_call(
        _ffn_body,
        out_shape=jax.ShapeDtypeStruct((b, t, d), F32),
        grid=(b, t // tm, nf),
        in_specs=[
            pl.BlockSpec((1, tm, d), lambda bi, i, f: (bi, i, 0), pipeline_mode=once),
            pl.BlockSpec((1, BF16_ROWS, d), lambda bi, i, f: (bi, jnp.maximum(i * hb - 1, 0), 0)),
            pl.BlockSpec((1, BF16_ROWS, d), lambda bi, i, f: (bi, jnp.minimum((i + 1) * hb, nblk - 1), 0)),
            pl.BlockSpec((1, d), lambda bi, i, f: (0, 0)),
            pl.BlockSpec((1, d), lambda bi, i, f: (0, 0)),
            pl.BlockSpec((d, tf), lambda bi, i, f: (0, f)),
            pl.BlockSpec((d, tf), lambda bi, i, f: (0, nf + f)),
            pl.BlockSpec((3, tf), lambda bi, i, f: (0, f)),
            pl.BlockSpec((3, tf), lambda bi, i, f: (0, nf + f)),
            pl.BlockSpec((1, tf), lambda bi, i, f: (0, f)),
            pl.BlockSpec((1, tf), lambda bi, i, f: (0, nf + f)),
            pl.BlockSpec((tf, d), lambda bi, i, f: (f, 0)),
        ],
        out_specs=pl.BlockSpec((1, tm, d), lambda bi, i, f: (bi, i, 0), pipeline_mode=once),
        scratch_shapes=[pltpu.VMEM((tm + 2 * BF16_ROWS, d), BF16)],
        compiler_params=_params("parallel", "parallel", "arbitrary"),
        name="conv_ffn",
    )(x, x, x, g_in.reshape(1, d), g_out.reshape(1, d), w_up, w_up, conv_w, conv_w, cb, cb, w_down)


def _trunk(x, mem, p):
    b, t, d = x.shape
    m = b * t
    n_mem = mem.shape[1]
    mix_a = p["ln_v_g"].shape[-1]
    mix_b = p["k_k"].shape[-1]
    n = p["r_k"].shape[-1]
    x2d = x.reshape(m, d)

    h1 = _rmsnorm([x2d], p["norm_mix"], BF16)
    za = _matmul([h1], p["w_in"], F32, col0=0, n=2 * mix_a)
    zrkv = _matmul([h1], p["w_in"], F32, col0=2 * mix_a, n=3 * mix_b)
    zl = _matmul([h1], p["w_in"], F32, col0=2 * mix_a + 3 * mix_b, n=p["w_in"].shape[1] - 2 * mix_a - 3 * mix_b)

    ya = _gmlp(za, p["ln_v_g"], p["ln_v_b"], p["w_s"], p["b_s"])

    r, v, kk, g, bonus, lw, kd, a = _rwkv_prep(
        zrkv.reshape(b, t, 3 * mix_b), zl.reshape(b, t, -1), p["mu_shift"], p["w0"], p["w_up_decay"],
        p["a0"], p["w_up_iclr"], p["w_up_gate"], p["k_k"], p["k_a"], p["r_k"], p["jmat"])
    y_fwd = _wkv_scan(r, v, kk, lw, kd, a, n, reverse=False)
    y_bwd = _wkv_scan(r, v, kk, lw, kd, a, n, reverse=True)
    yb = _rwkv_post(y_fwd.reshape(m, mix_b), y_bwd.reshape(m, mix_b), bonus.reshape(m, mix_b),
                    g.reshape(m, mix_b), p["gn_g"], p["gn_b"], p["jmat"], n)

    x1 = _matmul([ya, yb], p["w_out"], F32, residual=x2d)

    hq = _rmsnorm([x1], p["norm_x"], BF16)
    q = _matmul([hq], p["w_q"], BF16)
    memn = _rmsnorm([mem.reshape(b * n_mem, d)], p["norm_mem"], BF16)
    kv = _matmul([memn], p["w_kv"], BF16)
    o = _attention(q.reshape(b, t, d), kv.reshape(b, n_mem, 2 * d))
    x2 = _matmul([o.reshape(m, d)], p["w_o"], F32, residual=x1)

    return _conv_ffn(x2.reshape(b, t, d), p["norm_ffn"], p["norm_out"], p["w_ffn_up"], p["conv_w"],
                     p["conv_b"], p["w_ffn_down"])


def kernel(x_prompt, x_sample, mem_prompt, mem_sample, norm_mix, w_in, mu_shift, ln_v_g, ln_v_b, w_s, b_s, w0, w_up_decay, a0, w_up_iclr, w_up_gate, k_k, k_a, r_k, gn_g, gn_b, w_out, norm_x, norm_mem, w_q, w_kv, w_o, norm_ffn, w_ffn_up, conv_w, conv_b, w_ffn_down, norm_out):
    depth = w_in.shape[0]
    n = r_k.shape[-1]
    head_id = jnp.arange(LANES) // n
    jmat = (head_id[:, None] == head_id[None, :]).astype(BF16)

    layers = []
    for l in range(depth):
        layers.append(dict(
            norm_mix=norm_mix[l], w_in=w_in[l].astype(BF16), mu_shift=mu_shift[l],
            ln_v_g=ln_v_g[l], ln_v_b=ln_v_b[l], w_s=w_s[l], b_s=b_s[l],
            w0=w0[l], w_up_decay=w_up_decay[l], a0=a0[l], w_up_iclr=w_up_iclr[l], w_up_gate=w_up_gate[l],
            k_k=k_k[l], k_a=k_a[l], r_k=r_k[l], gn_g=gn_g[l], gn_b=gn_b[l],
            w_out=w_out[l].astype(BF16), norm_x=norm_x[l], norm_mem=norm_mem[l],
            w_q=w_q[l].astype(BF16), w_kv=w_kv[l].astype(BF16), w_o=w_o[l].astype(BF16),
            norm_ffn=norm_ffn[l], w_ffn_up=w_ffn_up[l].astype(BF16), conv_w=conv_w[l], conv_b=conv_b[l],
            w_ffn_down=w_ffn_down[l].astype(BF16), jmat=jmat))

    (p,) = layers
    p = dict(p, norm_out=norm_out)
    return _trunk(x_prompt, mem_prompt, p), _trunk(x_sample, mem_sample, p)
```

```python
import functools

import jax
import jax.numpy as jnp
from jax import lax
from jax.experimental import pallas as pl
from jax.experimental.pallas import tpu as pltpu

RMS_EPS = 1e-6
LN_EPS = 1e-5
GN_EPS = 64e-5
L2_EPS = 1e-12
LOGW_SCALE = 0.6065306597126334
X_HEADS = 4

LANES = 128
BF16_ROWS = 16
SCAN_CHUNK = 64
SCAN_HEADS = 32
SCAN_GROUP = 4
NORM_ROWS = 128
INV_BASE = 8
VMEM_LIMIT = 60 * 1024 * 1024

F32 = jnp.float32
BF16 = jnp.bfloat16


def _params(*sem, flags=None):
    return pltpu.CompilerParams(dimension_semantics=sem, vmem_limit_bytes=VMEM_LIMIT, flags=flags)


def _pick(n, candidates):
    for c in candidates:
        if n % c == 0:
            return c
    raise ValueError(f"no tile for {n} in {candidates}")


def _dot(a, b):
    return jnp.dot(a, b, preferred_element_type=F32)


NN = (((1,), (0,)), ((), ()))
NT = (((1,), (1,)), ((), ()))
TN = (((0,), (0,)), ((), ()))


def _rmsnorm_body(*refs, n_in):
    g_ref, o_ref = refs[n_in], refs[n_in + 1]
    x = refs[0][...]
    for r in refs[1:n_in]:
        x = x + r[...]
    ms = jnp.mean(x * x, axis=-1, keepdims=True)
    o_ref[...] = (x * lax.rsqrt(ms + RMS_EPS) * g_ref[...]).astype(o_ref.dtype)


def _rmsnorm(xs, g, out_dtype):
    m, d = xs[0].shape
    tm = _pick(m, (512, 256, 128, 64, 32, 16, 8))
    row = pl.BlockSpec((tm, d), lambda i: (i, 0))
    return pl.pallas_call(
        functools.partial(_rmsnorm_body, n_in=len(xs)),
        out_shape=jax.ShapeDtypeStruct((m, d), out_dtype),
        grid=(m // tm,),
        in_specs=[row] * len(xs) + [pl.BlockSpec((1, d), lambda i: (0, 0))],
        out_specs=row,
        compiler_params=_params("parallel"),
        name="rmsnorm",
    )(*xs, g.reshape(1, d))


def _mm_body(*refs, n_lhs, has_res):
    a_refs, w_refs = refs[:n_lhs], refs[n_lhs:2 * n_lhs]
    o_ref = refs[-1]
    acc = _dot(a_refs[0][...], w_refs[0][...])
    for a, w in zip(a_refs[1:], w_refs[1:]):
        acc = acc + _dot(a[...], w[...])
    if has_res:
        acc = acc + refs[2 * n_lhs][...]
    o_ref[...] = acc.astype(o_ref.dtype)


def _matmul(lhs, w, out_dtype, residual=None, col0=0, n=None):
    m = lhs[0].shape[0]
    n = w.shape[1] if n is None else n
    tm = _pick(m, (1024, 512, 256, 128, 64, 32, 16, 8))
    tn = _pick(n, (1024, 768, 512, 256, 128))
    if col0 % tn:
        w, col0 = w[:, col0:col0 + n], 0
    cb = col0 // tn
    in_specs = [pl.BlockSpec((tm, a.shape[1]), lambda i, j: (i, 0)) for a in lhs]
    in_specs += [pl.BlockSpec((a.shape[1], tn), lambda i, j, rb=rb: (rb, cb + j)) for rb, a in enumerate(lhs)]
    assert all(a.shape[1] == lhs[0].shape[1] for a in lhs) and len(lhs) * lhs[0].shape[1] == w.shape[0]
    args = list(lhs) + [w] * len(lhs)
    if residual is not None:
        in_specs.append(pl.BlockSpec((tm, tn), lambda i, j: (i, j)))
        args.append(residual)
    return pl.pallas_call(
        functools.partial(_mm_body, n_lhs=len(lhs), has_res=residual is not None),
        out_shape=jax.ShapeDtypeStruct((m, n), out_dtype),
        grid=(m // tm, n // tn),
        in_specs=in_specs,
        out_specs=pl.BlockSpec((tm, tn), lambda i, j: (i, j)),
        compiler_params=_params("parallel", "parallel"),
        name="matmul",
    )(*args)


def _gmlp_body(zu_ref, zv_ref, g_ref, b_ref, ws_ref, bs_ref, o_ref, *, chunk):
    u = jax.nn.gelu(zu_ref[...])
    v = jax.nn.gelu(zv_ref[...])
    mu = jnp.mean(v, axis=-1, keepdims=True)
    var = jnp.mean(jnp.square(v - mu), axis=-1, keepdims=True)
    vn = ((v - mu) * lax.rsqrt(var + LN_EPS) * g_ref[...] + b_ref[...]).astype(BF16)
    rows, width = u.shape
    heads = ws_ref.shape[0]
    hd = width // heads
    bs = bs_ref[...]
    for c in range(rows // chunk):
        rs = slice(c * chunk, (c + 1) * chunk)
        mixed = jnp.concatenate(
            [_dot(ws_ref[h], vn[rs, h * hd:(h + 1) * hd]) for h in range(heads)], axis=1)
        o_ref[rs, :] = (u[rs, :] * (mixed + bs)).astype(o_ref.dtype)


def _gmlp(za, ln_g, ln_b, w_s, b_s):
    m, two_a = za.shape
    mix_a = two_a // 2
    heads, chunk, _ = w_s.shape
    rows = _pick(m, (4 * chunk, 2 * chunk, chunk))
    bs_full = jnp.repeat(b_s.T, mix_a // heads, axis=1)
    vec = pl.BlockSpec((1, mix_a), lambda i: (0, 0))
    return pl.pallas_call(
        functools.partial(_gmlp_body, chunk=chunk),
        out_shape=jax.ShapeDtypeStruct((m, mix_a), BF16),
        grid=(m // rows,),
        in_specs=[pl.BlockSpec((rows, mix_a), lambda i: (i, 0)),
                  pl.BlockSpec((rows, mix_a), lambda i: (i, 1)),
                  vec, vec,
                  pl.BlockSpec((heads, chunk, chunk), lambda i: (0, 0, 0)),
                  pl.BlockSpec((chunk, mix_a), lambda i: (0, 0))],
        out_specs=pl.BlockSpec((rows, mix_a), lambda i: (i, 0)),
        compiler_params=_params("parallel"),
        name="gmlp",
    )(za, za, ln_g.reshape(1, mix_a), ln_b.reshape(1, mix_a), w_s.astype(BF16), bs_full)


def _headsum(x, j_ref):
    tm, w = x.shape
    nb = w // LANES
    xs = jnp.concatenate([x[:, i * LANES:(i + 1) * LANES] for i in range(nb)], axis=0)
    s = _dot(xs.astype(BF16), j_ref[...])
    return jnp.concatenate([s[i * tm:(i + 1) * tm] for i in range(nb)], axis=1)


def _tshift(z, zp, zn, mu, first, last):
    tm = z.shape[0]
    rows = lax.broadcasted_iota(jnp.int32, (8, 1), 0)
    prev_row = jnp.where(first, 0.0, zp[7:8, :])
    next_row = jnp.where(last, 0.0, zn[0:1, :])
    prev = pltpu.roll(z, 1, 0)
    nxt = pltpu.roll(z, tm - 1, 0)
    prev = jnp.concatenate([jnp.where(rows == 0, prev_row, prev[:8]), prev[8:]], axis=0)
    nxt = jnp.concatenate([nxt[:tm - 8], jnp.where(rows == 7, next_row, nxt[tm - 8:])], axis=0)
    return z * (1.0 - mu) + (0.5 * mu) * (prev + nxt)


def _prep_body(zr, zrp, zrn, zk, zkp, zkn, zv, zvp, zvn, zl, zlp, zln,
               mur, muk, muv, mul, w0_ref, wd_ref, a0_ref, wi_ref, wg_ref,
               kk_ref, ka_ref, rk_ref, j_ref,
               r_out, v_out, kk_out, g_out, bonus_out, lw_out, kd_out, a_out,
               *, lora_w, lora_a):
    i = pl.program_id(1)
    first = i == 0
    last = i == pl.num_programs(1) - 1
    r = _tshift(zr[0], zrp[0], zrn[0], mur[...], first, last)
    k = _tshift(zk[0], zkp[0], zkn[0], muk[...], first, last)
    v = _tshift(zv[0], zvp[0], zvn[0], muv[...], first, last)
    lo = _tshift(zl[0], zlp[0], zln[0], mul[...], first, last)
    xw = lo[:, :lora_w]
    xa = lo[:, lora_w:lora_w + lora_a]
    xg = lo[:, lora_w + lora_a:]

    g_out[0] = _dot(jax.nn.sigmoid(xg).astype(BF16), wg_ref[...]).astype(g_out.dtype)
    kkr = k * kk_ref[...]
    ss = _headsum(kkr * kkr, j_ref)
    kk = kkr * lax.rsqrt(jnp.maximum(ss, L2_EPS))
    tw = jnp.tanh(xw).astype(BF16)
    xab = xa.astype(BF16)
    rk = None
    for d in range(2):
        lw_out[d, 0] = -LOGW_SCALE * jax.nn.sigmoid(w0_ref[d:d + 1, :] + _dot(tw, wd_ref[d]))
        a = jax.nn.sigmoid(a0_ref[d:d + 1, :] + _dot(xab, wi_ref[d]))
        kd = k * (1.0 + (a - 1.0) * ka_ref[...])
        a_out[d, 0] = a.astype(a_out.dtype)
        kd_out[d, 0] = kd.astype(kd_out.dtype)
        s = _headsum(r * kd * rk_ref[...], j_ref)
        rk = s if rk is None else rk + s
    r_out[0] = r.astype(r_out.dtype)
    v_out[0] = v.astype(v_out.dtype)
    kk_out[0] = kk.astype(kk_out.dtype)
    bonus_out[0] = (rk * v).astype(bonus_out.dtype)


def _rwkv_prep(zrkv, zl, mu, w0, w_up_decay, a0, w_up_iclr, w_up_gate, k_k, k_a, r_k, jmat):
    b, t, three_b = zrkv.shape
    mix_b = three_b // 3
    nl = zl.shape[-1]
    lora_w, lora_a = w_up_decay.shape[1], w_up_iclr.shape[1]
    tm = _pick(t, (128, 64, 32, 16, 8))
    hb = tm // 8
    nblk8 = t // 8

    def main(w, col):
        return pl.BlockSpec((1, tm, w), lambda bi, i: (bi, i, col))

    def prev(w, col):
        return pl.BlockSpec((1, 8, w), lambda bi, i: (bi, jnp.maximum(i * hb - 1, 0), col))

    def nxt(w, col):
        return pl.BlockSpec((1, 8, w), lambda bi, i: (bi, jnp.minimum((i + 1) * hb, nblk8 - 1), col))

    def const(shape):
        return pl.BlockSpec(shape, lambda bi, i: (0,) * len(shape))

    in_specs, args = [], []
    for col in range(3):
        in_specs += [main(mix_b, col), prev(mix_b, col), nxt(mix_b, col)]
        args += [zrkv] * 3
    in_specs += [main(nl, 0), prev(nl, 0), nxt(nl, 0)]
    args += [zl] * 3
    mu2 = mu.reshape(1, -1)
    in_specs += [pl.BlockSpec((1, mix_b), lambda bi, i, c=c: (0, c)) for c in range(3)]
    args += [mu2[:, :three_b]] * 3
    in_specs.append(const((1, nl)))
    args.append(mu2[:, three_b:])
    consts = [w0, w_up_decay.astype(BF16), a0, w_up_iclr.astype(BF16), w_up_gate.astype(BF16),
              k_k.reshape(1, mix_b), k_a.reshape(1, mix_b), r_k.reshape(1, mix_b), jmat]
    in_specs += [const(c.shape) for c in consts]
    args += consts

    shared = jax.ShapeDtypeStruct((b, t, mix_b), BF16)
    perdir = lambda dt: jax.ShapeDtypeStruct((2, b, t, mix_b), dt)
    o_shared = pl.BlockSpec((1, tm, mix_b), lambda bi, i: (bi, i, 0))
    o_perdir = pl.BlockSpec((2, 1, tm, mix_b), lambda bi, i: (0, bi, i, 0))
    return pl.pallas_call(
        functools.partial(_prep_body, lora_w=lora_w, lora_a=lora_a),
        out_shape=[shared] * 5 + [perdir(F32), perdir(BF16), perdir(BF16)],
        grid=(b, t // tm),
        in_specs=in_specs,
        out_specs=[o_shared] * 5 + [o_perdir] * 3,
        compiler_params=_params("parallel", "parallel"),
        name="rwkv_prep",
    )(*args)


def _mm(a, b, dims):
    return lax.dot_general(a.astype(BF16), b.astype(BF16), dims, preferred_element_type=F32)


def _halves(x, size):
    blocks = x.shape[0] // (2 * size)
    lo = jnp.concatenate([x[2 * j * size:(2 * j + 1) * size] for j in range(blocks)], axis=0)
    hi = jnp.concatenate([x[(2 * j + 1) * size:(2 * j + 2) * size] for j in range(blocks)], axis=0)
    return lo, hi


def _interleave(lo, hi, size):
    pieces = []
    for j in range(lo.shape[0] // size):
        pieces += [lo[j * size:(j + 1) * size], hi[j * size:(j + 1) * size]]
    return jnp.concatenate(pieces, axis=0)


def _tri_inverse(ls, limit, reverse):
    rows = ls[0].shape[0]
    row = lax.broadcasted_iota(jnp.int32, (rows, rows), 0)
    col = lax.broadcasted_iota(jnp.int32, (rows, rows), 1)
    eye = (row == col).astype(F32)
    sh = INV_BASE.bit_length() - 1
    base = (row >> sh) == (col >> sh)
    ps = [jnp.where(base, l, 0.0) for l in ls]
    ts = [jnp.where(row == col, 1.0, p) for p in ps]
    size = 2
    while size < INV_BASE:
        ps = [_mm(p, p, NN) for p in ps]
        ts = [_mm(t, eye + p, NN) for t, p in zip(ts, ps)]
        size *= 2
    size = INV_BASE
    hrow = lax.broadcasted_iota(jnp.int32, (rows // 2, rows), 0)
    hcol = lax.broadcasted_iota(jnp.int32, (rows // 2, rows), 1)
    zero = jnp.zeros((rows // 2, rows), F32)
    while size < limit:
        sh = size.bit_length() - 1
        partner = ((hcol >> (sh + 1)) == (hrow >> sh)) & (((hcol >> sh) & 1) == (1 if reverse else 0))
        act = 0 if reverse else 1
        l_act = [_halves(l, size)[act] for l in ls]
        t_halves = [_halves(t, size) for t in ts]
        xs = [_mm(jnp.where(partner, la, 0.0), t, NN) for la, t in zip(l_act, ts)]
        xs = [_interleave(x, zero, size) if reverse else _interleave(zero, x, size) for x in xs]
        upd = [th[act] + _mm(th[act], x, NN) for th, x in zip(t_halves, xs)]
        ts = [_interleave(u, th[1], size) if reverse else _interleave(th[0], u, size)
              for th, u in zip(t_halves, upd)]
        size *= 2
    return ts


def _scan_body(r_ref, v_ref, kk_ref, lw_ref, kd_ref, a_ref, y_ref, s_ref, *, n, reverse):
    cs = SCAN_CHUNK
    groups = r_ref.shape[1] // cs
    tiles = r_ref.shape[2] // LANES

    @pl.when(pl.program_id(2) == 0)
    def _():
        s_ref[...] = jnp.zeros_like(s_ref)

    state = [s_ref[p] for p in range(tiles)]
    for gi in (range(groups - 1, -1, -1) if reverse else range(groups)):
        state = _scan_chunk(r_ref, v_ref, kk_ref, lw_ref, kd_ref, a_ref, y_ref, state, gi * cs,
                            n=n, reverse=reverse)
    for p in range(tiles):
        s_ref[p] = state[p]


def _scan_chunk(r_ref, v_ref, kk_ref, lw_ref, kd_ref, a_ref, y_ref, state, r0, *, n, reverse):
    cs = SCAN_CHUNK
    rs = slice(r0, r0 + cs)
    sgn = -1 if reverse else 1
    tiles = r_ref.shape[2] // LANES
    hp = LANES // n
    rows = hp * cs

    lw = lw_ref[0, 0, rs, :]
    rc = lax.broadcasted_iota(jnp.int32, (cs, cs), 0)
    cc = lax.broadcasted_iota(jnp.int32, (cs, cs), 1)
    tri = ((rc - cc) * sgn >= 0).astype(BF16)
    l_hi = lw.astype(BF16)
    l_mid = (lw - l_hi.astype(F32)).astype(BF16)
    l_lo = (lw - l_hi.astype(F32) - l_mid.astype(F32)).astype(BF16)
    cum = _dot(tri, l_hi) + (_dot(tri, l_mid) + _dot(tri, l_lo))
    tot = cum[0:1, :] if reverse else cum[cs - 1:cs, :]
    e_tot = jnp.exp(tot)

    row = lax.broadcasted_iota(jnp.int32, (rows, rows), 0)
    col = lax.broadcasted_iota(jnp.int32, (rows, rows), 1)
    order = ((row & (cs - 1)) - (col & (cs - 1))) * sgn
    strict = order > 0
    incl = order >= 0
    eye_l = (lax.broadcasted_iota(jnp.int32, (LANES, LANES), 0)
             == lax.broadcasted_iota(jnp.int32, (LANES, LANES), 1))
    lane_head = lax.broadcasted_iota(jnp.int32, (cs, LANES), 1) // n
    head_masks = [lane_head == h for h in range(hp)]
    zero_t = jnp.zeros((rows, LANES), BF16)

    def pack(x):
        return jnp.concatenate([jnp.where(mk, x, jnp.zeros_like(x)) for mk in head_masks], axis=0)

    tl = range(tiles)
    cat = jnp.concatenate
    at, rt, bh, kh, vv, aa = [], [], [], [], [], []
    for p in tl:
        sl = slice(p * LANES, (p + 1) * LANES)
        cum_p, lw_p, tot_p = cum[:, sl], lw[:, sl], tot[:, sl]
        kk_p = kk_ref[0, rs, sl].astype(F32)
        kd_p = kd_ref[0, 0, rs, sl].astype(F32)
        b_p = kk_p * a_ref[0, 0, rs, sl].astype(F32)
        e_neg = jnp.exp(-cum_p)
        e_hat = jnp.exp(tot_p - cum_p)
        at.append(pack(-kk_p * jnp.exp(cum_p - lw_p)))
        rt.append(pack(r_ref[0, rs, sl].astype(F32) * jnp.exp(cum_p)))
        bh.append(pack((b_p * e_hat).astype(BF16)))
        kh.append(pack((kd_p * e_hat).astype(BF16)))
        vv.append(pack(v_ref[0, rs, sl].astype(BF16)))
        bk = cat([pack((b_p * e_neg).astype(BF16)), pack((kd_p * e_neg).astype(BF16))], axis=0)
        aa.append(_mm(cat([at[p], rt[p]], axis=0), bk, NT).astype(BF16))
    aab = [jnp.where(strict, x[:rows, :rows], 0.0) for x in aa]
    aak = [jnp.where(strict, x[:rows, rows:], 0.0) for x in aa]
    aq = [cat([jnp.where(incl, x[rows:, :rows], 0.0), jnp.where(incl, x[rows:, rows:], 0.0)], axis=1)
          for x in aa]
    tinv = _tri_inverse(aab, cs, reverse)
    akv = [_mm(aak[p], vv[p], NN) for p in tl]
    wu = [_mm(tinv[p], cat([at[p], akv[p]], axis=1), NN) for p in tl]
    rhs = [cat([wu[p].astype(BF16), cat([zero_t, vv[p]], axis=1)], axis=0) for p in tl]
    qy = [_mm(aq[p], rhs[p], NN) for p in tl]
    mn = [_mm(cat([bh[p], kh[p]], axis=0), rhs[p], TN) for p in tl]
    lhs = []
    for p in tl:
        m_mat = jnp.where(eye_l, e_tot[:, p * LANES:(p + 1) * LANES], 0.0) + mn[p][:, :LANES]
        lhs.append(cat([m_mat, rt[p] + qy[p][:, :LANES]], axis=0))
    ms = [_mm(lhs[p], state[p], NN) for p in tl]
    outs, new_state = [], []
    for p in tl:
        new_state.append(ms[p][:LANES] + mn[p][:, LANES:])
        ybd = ms[p][LANES:] + qy[p][:, LANES:]
        y = ybd[:cs]
        for h in range(1, hp):
            y = y + ybd[h * cs:(h + 1) * cs]
        outs.append(y)
    y_ref[0, rs, :] = cat(outs, axis=1)
    return new_state


def _wkv_scan(r, v, kk, lw, kd, a, n, reverse):
    b, t, mix_b = r.shape
    heads = mix_b // n
    hg = SCAN_HEADS if heads % SCAN_HEADS == 0 else heads
    cs = SCAN_CHUNK * _pick(t // SCAN_CHUNK, (SCAN_GROUP, 1))
    nc = t // cs
    w = hg * n
    di = 1 if reverse else 0
    tchunk = (lambda ci: nc - 1 - ci) if reverse else (lambda ci: ci)
    shared = pl.BlockSpec((1, cs, w), lambda bi, hi, ci: (bi, tchunk(ci), hi))
    perdir = pl.BlockSpec((1, 1, cs, w), lambda bi, hi, ci: (di, bi, tchunk(ci), hi))
    return pl.pallas_call(
        functools.partial(_scan_body, n=n, reverse=reverse),
        out_shape=jax.ShapeDtypeStruct((b, t, mix_b), F32),
        grid=(b, heads // hg, nc),
        in_specs=[shared] * 3 + [perdir] * 3,
        out_specs=shared,
        scratch_shapes=[pltpu.VMEM((w // LANES, LANES, LANES), F32)],
        compiler_params=_params("parallel", "parallel", "arbitrary"),
        name="wkv_scan",
    )(r, v, kk, lw, kd, a)


def _post_body(yf_ref, yb_ref, bonus_ref, g_ref, gg_ref, gb_ref, j_ref, o_ref, *, n):
    y = yf_ref[...] + yb_ref[...]
    mean = _headsum(y, j_ref) * (1.0 / n)
    yc = y - mean
    var = _headsum(yc * yc, j_ref) * (1.0 / n)
    yn = yc * lax.rsqrt(var + GN_EPS) * gg_ref[...] + gb_ref[...]
    o_ref[...] = ((yn + bonus_ref[...]) * g_ref[...]).astype(o_ref.dtype)


def _rwkv_post(y_fwd, y_bwd, bonus, g, gn_g, gn_b, jmat, n):
    m, mix_b = y_fwd.shape
    tm = _pick(m, (512, 256, 128, 64, 32, 16, 8))
    row = pl.BlockSpec((tm, mix_b), lambda i: (i, 0))
    vec = pl.BlockSpec((1, mix_b), lambda i: (0, 0))
    return pl.pallas_call(
        functools.partial(_post_body, n=n),
        out_shape=jax.ShapeDtypeStruct((m, mix_b), BF16),
        grid=(m // tm,),
        in_specs=[row, row, row, row, vec, vec, pl.BlockSpec(jmat.shape, lambda i: (0, 0))],
        out_specs=row,
        compiler_params=_params("parallel"),
        name="rwkv_post",
    )(y_fwd, y_bwd, bonus, g, gn_g.reshape(1, mix_b), gn_b.reshape(1, mix_b), jmat)


def _attn_body(q_ref, kv_ref, o_ref, *, heads):
    d = q_ref.shape[-1]
    hd = d // heads
    scale = hd ** -0.5
    cols = [slice(h * hd, (h + 1) * hd) for h in range(heads)]
    ss = [lax.dot_general(q_ref[0, :, c], kv_ref[0, :, c], NT, preferred_element_type=F32) * scale for c in cols]
    es = [jnp.exp(s - jnp.max(s, axis=-1, keepdims=True)) for s in ss]
    ps = [(e / jnp.sum(e, axis=-1, keepdims=True)).astype(BF16) for e in es]
    for h, c in enumerate(cols):
        v = kv_ref[0, :, d + h * hd:d + (h + 1) * hd]
        o_ref[0, :, c] = _dot(ps[h], v).astype(o_ref.dtype)


def _attention(q, kv):
    b, t, d = q.shape
    n_mem = kv.shape[1]
    tq = _pick(t, (1024, 512, 256, 128, 64, 32, 16))
    return pl.pallas_call(
        functools.partial(_attn_body, heads=X_HEADS),
        out_shape=jax.ShapeDtypeStruct((b, t, d), BF16),
        grid=(b, t // tq),
        in_specs=[pl.BlockSpec((1, tq, d), lambda bi, i: (bi, i, 0)),
                  pl.BlockSpec((1, n_mem, 2 * d), lambda bi, i: (bi, 0, 0))],
        out_specs=pl.BlockSpec((1, tq, d), lambda bi, i: (bi, i, 0)),
        compiler_params=_params("parallel", "parallel"),
        name="cross_attention",
    )(q, kv)


def _rms(x, g):
    return x * lax.rsqrt(jnp.mean(x * x, axis=-1, keepdims=True) + RMS_EPS) * g


def _ffn_body(x_ref, xp_ref, xn_ref, gin_ref, gout_ref, wg_ref, wv_ref, cwg_ref, cwv_ref, cbg_ref,
              cbv_ref, wd_ref, o_ref, hx_ref):
    i = pl.program_id(1)
    f = pl.program_id(2)
    tm = x_ref.shape[1]
    halo = BF16_ROWS

    @pl.when(f == 0)
    def _():
        g = gin_ref[...]
        zero = jnp.zeros((halo, x_ref.shape[2]), BF16)
        hx_ref[0:halo, :] = jnp.where(i == 0, zero, _rms(xp_ref[0], g).astype(BF16))
        hx_ref[halo + tm:, :] = jnp.where(i == pl.num_programs(1) - 1, zero, _rms(xn_ref[0], g).astype(BF16))

        def norm_rows(c, carry):
            r0 = pl.multiple_of(c * NORM_ROWS, NORM_ROWS)
            hx_ref[pl.ds(halo + r0, NORM_ROWS), :] = _rms(x_ref[0, pl.ds(r0, NORM_ROWS), :], g).astype(BF16)
            o_ref[0, pl.ds(r0, NORM_ROWS), :] = jnp.zeros((NORM_ROWS, x_ref.shape[2]), F32)
            return carry

        lax.fori_loop(0, tm // NORM_ROWS, norm_rows, 0)

    hx = hx_ref[...]
    rows = tm + 2 * halo

    def conv(w_ref, cw_ref, cb_ref):
        z = _dot(hx, w_ref[...])
        prev = pltpu.roll(z, 1, 0)[halo:halo + tm]
        nxt = pltpu.roll(z, rows - 1, 0)[halo:halo + tm]
        cw = cw_ref[...]
        return cw[0:1] * prev + cw[1:2] * z[halo:halo + tm] + cw[2:3] * nxt + cb_ref[...]

    gate = conv(wg_ref, cwg_ref, cbg_ref)
    val = conv(wv_ref, cwv_ref, cbv_ref)
    act = (jax.nn.silu(gate) * val).astype(BF16)
    o_ref[0] += _dot(act, wd_ref[...])

    @pl.when(f == pl.num_programs(2) - 1)
    def _():
        g = gout_ref[...]

        def out_rows(c, carry):
            rs = pl.ds(pl.multiple_of(c * NORM_ROWS, NORM_ROWS), NORM_ROWS)
            o_ref[0, rs, :] = _rms(x_ref[0, rs, :] + o_ref[0, rs, :], g)
            return carry

        lax.fori_loop(0, tm // NORM_ROWS, out_rows, 0)


def _conv_ffn(x, g_in, g_out, w_up, conv_w, conv_b, w_down):
    b, t, d = x.shape
    d_ff = w_down.shape[0]
    tm = _pick(t, (1024, 512, 256, 128))
    tf = _pick(d_ff, (256, 128))
    nf = d_ff // tf
    hb = tm // BF16_ROWS
    nblk = t // BF16_ROWS
    cb = conv_b.reshape(1, 2 * d_ff)
    once = pl.Buffered(1)
    return pl.pallas_call(
        _ffn_body,
        out_shape=jax.ShapeDtypeStruct((b, t, d), F32),
        grid=(b, t // tm, nf),
        in_specs=[
            pl.BlockSpec((1, tm, d), lambda bi, i, f: (bi, i, 0), pipeline_mode=once),
            pl.BlockSpec((1, BF16_ROWS, d), lambda bi, i, f: (bi, jnp.maximum(i * hb - 1, 0), 0)),
            pl.BlockSpec((1, BF16_ROWS, d), lambda bi, i, f: (bi, jnp.minimum((i + 1) * hb, nblk - 1), 0)),
            pl.BlockSpec((1, d), lambda bi, i, f: (0, 0)),
            pl.BlockSpec((1, d), lambda bi, i, f: (0, 0)),
            pl.BlockSpec((d, tf), lambda bi, i, f: (0, f)),
            pl.BlockSpec((d, tf), lambda bi, i, f: (0, nf + f)),
            pl.BlockSpec((3, tf), lambda bi, i, f: (0, f)),
            pl.BlockSpec((3, tf), lambda bi, i, f: (0, nf + f)),
            pl.BlockSpec((1, tf), lambda bi, i, f: (0, f)),
            pl.BlockSpec((1, tf), lambda bi, i, f: (0, nf + f)),
            pl.BlockSpec((tf, d), lambda bi, i, f: (f, 0)),
        ],
        out_specs=pl.BlockSpec((1, tm, d), lambda bi, i, f: (bi, i, 0), pipeline_mode=once),
        scratch_shapes=[pltpu.VMEM((tm + 2 * BF16_ROWS, d), BF16)],
        compiler_params=_params("parallel", "parallel", "arbitrary"),
        name="conv_ffn",
    )(x, x, x, g_in.reshape(1, d), g_out.reshape(1, d), w_up, w_up, conv_w, conv_w, cb, cb, w_down)


def _trunk(x, mem, p):
    b, t, d = x.shape
    m = b * t
    n_mem = mem.shape[1]
    mix_a = p["ln_v_g"].shape[-1]
    mix_b = p["k_k"].shape[-1]
    n = p["r_k"].shape[-1]
    x2d = x.reshape(m, d)

    h1 = _rmsnorm([x2d], p["norm_mix"], BF16)
    za = _matmul([h1], p["w_in"], F32, col0=0, n=2 * mix_a)
    zrkv = _matmul([h1], p["w_in"], F32, col0=2 * mix_a, n=3 * mix_b)
    zl = _matmul([h1], p["w_in"], F32, col0=2 * mix_a + 3 * mix_b, n=p["w_in"].shape[1] - 2 * mix_a - 3 * mix_b)

    ya = _gmlp(za, p["ln_v_g"], p["ln_v_b"], p["w_s"], p["b_s"])

    r, v, kk, g, bonus, lw, kd, a = _rwkv_prep(
        zrkv.reshape(b, t, 3 * mix_b), zl.reshape(b, t, -1), p["mu_shift"], p["w0"], p["w_up_decay"],
        p["a0"], p["w_up_iclr"], p["w_up_gate"], p["k_k"], p["k_a"], p["r_k"], p["jmat"])
    y_fwd = _wkv_scan(r, v, kk, lw, kd, a, n, reverse=False)
    y_bwd = _wkv_scan(r, v, kk, lw, kd, a, n, reverse=True)
    yb = _rwkv_post(y_fwd.reshape(m, mix_b), y_bwd.reshape(m, mix_b), bonus.reshape(m, mix_b),
                    g.reshape(m, mix_b), p["gn_g"], p["gn_b"], p["jmat"], n)

    x1 = _matmul([ya, yb], p["w_out"], F32, residual=x2d)

    hq = _rmsnorm([x1], p["norm_x"], BF16)
    q = _matmul([hq], p["w_q"], BF16)
    memn = _rmsnorm([mem.reshape(b * n_mem, d)], p["norm_mem"], BF16)
    kv = _matmul([memn], p["w_kv"], BF16)
    o = _attention(q.reshape(b, t, d), kv.reshape(b, n_mem, 2 * d))
    x2 = _matmul([o.reshape(m, d)], p["w_o"], F32, residual=x1)

    return _conv_ffn(x2.reshape(b, t, d), p["norm_ffn"], p["norm_out"], p["w_ffn_up"], p["conv_w"],
                     p["conv_b"], p["w_ffn_down"])


def kernel(x_prompt, x_sample, mem_prompt, mem_sample, norm_mix, w_in, mu_shift, ln_v_g, ln_v_b, w_s, b_s, w0, w_up_decay, a0, w_up_iclr, w_up_gate, k_k, k_a, r_k, gn_g, gn_b, w_out, norm_x, norm_mem, w_q, w_kv, w_o, norm_ffn, w_ffn_up, conv_w, conv_b, w_ffn_down, norm_out):
    depth = w_in.shape[0]
    n = r_k.shape[-1]
    head_id = jnp.arange(LANES) // n
    jmat = (head_id[:, None] == head_id[None, :]).astype(BF16)

    layers = []
    for l in range(depth):
        layers.append(dict(
            norm_mix=norm_mix[l], w_in=w_in[l].astype(BF16), mu_shift=mu_shift[l],
            ln_v_g=ln_v_g[l], ln_v_b=ln_v_b[l], w_s=w_s[l], b_s=b_s[l],
            w0=w0[l], w_up_decay=w_up_decay[l], a0=a0[l], w_up_iclr=w_up_iclr[l], w_up_gate=w_up_gate[l],
            k_k=k_k[l], k_a=k_a[l], r_k=r_k[l], gn_g=gn_g[l], gn_b=gn_b[l],
            w_out=w_out[l].astype(BF16), norm_x=norm_x[l], norm_mem=norm_mem[l],
            w_q=w_q[l].astype(BF16), w_kv=w_kv[l].astype(BF16), w_o=w_o[l].astype(BF16),
            norm_ffn=norm_ffn[l], w_ffn_up=w_ffn_up[l].astype(BF16), conv_w=conv_w[l], conv_b=conv_b[l],
            w_ffn_down=w_ffn_down[l].astype(BF16), jmat=jmat))

    (p,) = layers
    p = dict(p, norm_out=norm_out)
    return _trunk(x_prompt, mem_prompt, p), _trunk(x_sample, mem_sample, p)
```
